```python
import functools
import jax, jax.numpy as jnp
from jax import lax
import numpy as np

D_MODEL = 1024
BATCH = 4
SEQ = 4096
DEPTH = 2
DEC_BATCH = 128
DEC_SEQ = 8
PAST_LEN = 16384
PAGE_SIZE = 128

HEAD_DIM = 64
N_HEADS = 8
N_KV_HEADS = 2
KV_WIDTH = N_KV_HEADS * HEAD_DIM
WINDOW = 128
ATTN_BLOCK = 128
ROPE_THETA = 500000.0
ROPE_DIMS = HEAD_DIM // 4
ATTN_WIDTH = N_HEADS * HEAD_DIM
GMLP_WIDTH = D_MODEL - ATTN_WIDTH
GMLP_GROUPS = 4
GMLP_GW = GMLP_WIDTH // GMLP_GROUPS
CHUNK = 128
MIX_WIDTH = ATTN_WIDTH + GMLP_WIDTH
IN_WIDTH = ATTN_WIDTH + 2 * KV_WIDTH + 2 * GMLP_WIDTH
IN_SPLITS = (ATTN_WIDTH, ATTN_WIDTH + KV_WIDTH, ATTN_WIDTH + 2 * KV_WIDTH,
             ATTN_WIDTH + 2 * KV_WIDTH + GMLP_WIDTH)
D_FF = 2816
CONV_W = 3
EPS = 1e-6
NEG = -1e30

kernel_name = 'hymba_swa_sink_gmlp_convffn_adaln_step'


def _rmsnorm(x, g):
    xf = x.astype(jnp.float32)
    y = xf * lax.rsqrt(jnp.mean(xf * xf, axis=-1, keepdims=True) + EPS)
    return (y * g.astype(jnp.float32)).astype(x.dtype)


def _layernorm(x, g, b):
    xf = x.astype(jnp.float32)
    mu = jnp.mean(xf, axis=-1, keepdims=True)
    xc = xf - mu
    y = xc * lax.rsqrt(jnp.mean(xc * xc, axis=-1, keepdims=True) + EPS)
    return (y * g.astype(jnp.float32) + b.astype(jnp.float32)).astype(x.dtype)


def _rope(x, pos):
    half = ROPE_DIMS // 2
    inv = ROPE_THETA ** (-jnp.arange(0, ROPE_DIMS, 2, dtype=jnp.float32) / ROPE_DIMS)
    ang = pos.astype(jnp.float32)[:, None] * inv[None, :]
    cos = jnp.cos(ang)[:, None, :]
    sin = jnp.sin(ang)[:, None, :]
    xr = x[..., :ROPE_DIMS].astype(jnp.float32)
    x1, x2 = xr[..., :half], xr[..., half:]
    rot = jnp.concatenate([x1 * cos - x2 * sin, x2 * cos + x1 * sin], axis=-1).astype(x.dtype)
    return jnp.concatenate([rot, x[..., ROPE_DIMS:]], axis=-1)


def _sink_attention(q, k, v, mask, sinks):
    lead = q.shape[:-3]
    tq = q.shape[-3]
    grp = N_HEADS // N_KV_HEADS
    qg = q.reshape(*lead, tq, N_KV_HEADS, grp, HEAD_DIM)
    s = jnp.einsum('...qkgd,...skd->...kgqs', qg, k).astype(jnp.float32) * (HEAD_DIM ** -0.5)
    s = jnp.where(mask, s, NEG)
    sink = jnp.broadcast_to(sinks.astype(jnp.float32).reshape(N_KV_HEADS, grp, 1, 1), s.shape[:-1] + (1,))
    p = jax.nn.softmax(jnp.concatenate([s, sink], axis=-1), axis=-1)[..., :-1]
    o = jnp.einsum('...kgqs,...skd->...qkgd', p.astype(v.dtype), v)
    return o.reshape(*lead, tq, N_HEADS * HEAD_DIM)


def _swa_prompt(q, k, v, sinks):
    b, t = q.shape[:2]
    nb = t // ATTN_BLOCK
    qb = q.reshape(b, nb, ATTN_BLOCK, N_HEADS, HEAD_DIM)

    def band(z):
        zb = z.reshape(b, nb, ATTN_BLOCK, N_KV_HEADS, HEAD_DIM)
        prev = jnp.pad(zb[:, :-1], ((0, 0), (1, 0), (0, 0), (0, 0), (0, 0)))
        return jnp.concatenate([prev, zb], axis=2)

    blk = jnp.arange(nb)[:, None]
    qpos = blk * ATTN_BLOCK + jnp.arange(ATTN_BLOCK)[None, :]
    kpos = (blk - 1) * ATTN_BLOCK + jnp.arange(2 * ATTN_BLOCK)[None, :]
    diff = qpos[:, :, None] - kpos[:, None, :]
    mask = (diff >= 0) & (diff < WINDOW) & (kpos[:, None, :] >= 0)
    o = _sink_attention(qb, band(k), band(v), mask[:, None, None], sinks)
    return o.reshape(b, t, ATTN_WIDTH)


def _swa_sample(q, k, v, sinks, k_cache, v_cache):
    b, t = q.shape[:2]
    wc = k_cache.shape[1]
    kk = jnp.concatenate([k_cache.astype(k.dtype), k], axis=1)
    vv = jnp.concatenate([v_cache.astype(v.dtype), v], axis=1)
    qpos = PAST_LEN + jnp.arange(t)
    kpos = jnp.concatenate([PAST_LEN - wc + jnp.arange(wc), qpos])
    diff = qpos[:, None] - kpos[None, :]
    mask = (diff >= 0) & (diff < WINDOW)
    return _sink_attention(q, kk, vv, mask, sinks).reshape(b, t, ATTN_WIDTH)


def _spatial_gate(u, v, w_s, b_s):
    b, l = u.shape[:2]
    lp = -(-l // CHUNK) * CHUNK
    vp = jnp.pad(v, ((0, 0), (0, lp - l), (0, 0))).reshape(b, lp // CHUNK, CHUNK, GMLP_GROUPS, GMLP_GW)
    causal = jnp.tril(jnp.ones((CHUNK, CHUNK), dtype=bool))
    w = jnp.where(causal[None], w_s, jnp.zeros((), w_s.dtype))
    z = jnp.einsum('gts,bcsgw->bctgw', w, vp) + b_s.T[None, None, :, :, None]
    return u * z.reshape(b, lp, GMLP_WIDTH)[:, :l]


def _layer(x, c, pos, attend, conv_prev, p):
    b, t, _ = x.shape
    mod = jnp.dot(jax.nn.silu(c), p['w_ada']) + p['b_ada']
    sh_a, sc_a, g_a, sh_f, sc_f, g_f = [m[:, None, :] for m in jnp.split(mod, 6, axis=-1)]
    h = _rmsnorm(x, p['g_attn']) * (1 + sc_a) + sh_a
    q, k, v, zu, zv = jnp.split(h @ p['w_in'], IN_SPLITS, axis=-1)
    q = _rope(_rmsnorm(q.reshape(b, t, N_HEADS, HEAD_DIM), p['g_q']), pos)
    k = _rope(_rmsnorm(k.reshape(b, t, N_KV_HEADS, HEAD_DIM), p['g_k']), pos)
    v = v.reshape(b, t, N_KV_HEADS, HEAD_DIM)
    a_out = attend(q, k, v, p['sinks'])
    u = jax.nn.gelu(zu)
    vg = _layernorm(jax.nn.gelu(zv).reshape(b, t, GMLP_GROUPS, GMLP_GW), p['ln_g'], p['ln_b'])
    vg = vg.reshape(b, t, GMLP_WIDTH)
    s_out = _spatial_gate(u, vg, p['w_s'], p['b_s'])
    x = x + g_a * (jnp.concatenate([a_out, s_out], axis=-1) @ p['w_out'])
    h = _rmsnorm(x, p['g_ffn']) * (1 + sc_f) + sh_f
    hu = h @ p['w_ffn_in']
    ext = jnp.concatenate([conv_prev.astype(hu.dtype), hu], axis=1)
    conv = p['conv_b'] + sum(p['conv_w'][j] * ext[:, j:j + t] for j in range(CONV_W))
    gate, up = jnp.split(conv, 2, axis=-1)
    x = x + g_f * ((jax.nn.silu(gate) * up) @ p['w_ffn_out'])
    return x, k, v, vg, ext[:, -(CONV_W - 1):]


def setup_inputs(seed: int = 0) -> dict:
    key = jax.random.key(seed)
    ks = jax.random.split(key, 32)
    f32 = jnp.float32
    nrm = lambda k, shape, s: jax.random.normal(k, shape, f32) * s
    return {
        'x_prompt': nrm(ks[0], (BATCH, SEQ, D_MODEL), 1.0),
        'x_sample': nrm(ks[1], (DEC_BATCH, DEC_SEQ, D_MODEL), 1.0),
        'cache_k': nrm(ks[2], (DEPTH, DEC_BATCH, WINDOW, N_KV_HEADS, HEAD_DIM), 1.0),
        'cache_v': nrm(ks[3], (DEPTH, DEC_BATCH, WINDOW, N_KV_HEADS, HEAD_DIM), 1.0),
        'cache_conv': nrm(ks[4], (DEPTH, DEC_BATCH, CONV_W - 1, 2 * D_FF), 1.0),
        'c_prompt': nrm(ks[5], (BATCH, D_MODEL), 1.0),
        'c_sample': nrm(ks[6], (DEC_BATCH, D_MODEL), 1.0),
        'w_ada': nrm(ks[7], (DEPTH, D_MODEL, 6 * D_MODEL), 0.5 * D_MODEL ** -0.5),
        'b_ada': nrm(ks[8], (DEPTH, 6 * D_MODEL), 0.02),
        'g_attn': 1.0 + nrm(ks[9], (DEPTH, D_MODEL), 0.02),
        'w_in': nrm(ks[10], (DEPTH, D_MODEL, IN_WIDTH), D_MODEL ** -0.5),
        'g_q': 1.0 + nrm(ks[11], (DEPTH, HEAD_DIM), 0.02),
        'g_k': 1.0 + nrm(ks[12], (DEPTH, HEAD_DIM), 0.02),
        'sinks': nrm(ks[13], (DEPTH, N_HEADS), 1.0),
        'ln_g': 1.0 + nrm(ks[14], (DEPTH, GMLP_GROUPS, GMLP_GW), 0.02),
        'ln_b': nrm(ks[15], (DEPTH, GMLP_GROUPS, GMLP_GW), 0.02),
        'w_s': nrm(ks[16], (DEPTH, GMLP_GROUPS, CHUNK, CHUNK), CHUNK ** -0.5),
        'b_s': 1.0 + nrm(ks[17], (DEPTH, GMLP_GROUPS, CHUNK), 0.1),
        'w_out': nrm(ks[18], (DEPTH, MIX_WIDTH, D_MODEL), MIX_WIDTH ** -0.5),
        'g_ffn': 1.0 + nrm(ks[19], (DEPTH, D_MODEL), 0.02),
        'w_ffn_in': nrm(ks[20], (DEPTH, D_MODEL, 2 * D_FF), D_MODEL ** -0.5),
        'conv_w': nrm(ks[21], (DEPTH, CONV_W, 2 * D_FF), CONV_W ** -0.5),
        'conv_b': nrm(ks[22], (DEPTH, 2 * D_FF), 0.02),
        'w_ffn_out': nrm(ks[23], (DEPTH, D_FF, D_MODEL), D_FF ** -0.5),
    }


def reference(x_prompt, x_sample, cache_k, cache_v, cache_conv, c_prompt, c_sample,
              w_ada, b_ada, g_attn, w_in, g_q, g_k, sinks, ln_g, ln_b, w_s, b_s, w_out,
              g_ffn, w_ffn_in, conv_w, conv_b, w_ffn_out):
    pos_p = jnp.arange(x_prompt.shape[1], dtype=jnp.int32)
    pos_s = PAST_LEN + jnp.arange(x_sample.shape[1], dtype=jnp.int32)
    xp, xs = x_prompt, x_sample
    kp_l, vp_l, cp_l, ks_l, vs_l, gs_l, cs_l = [], [], [], [], [], [], []
    for l in range(DEPTH):
        p = {'w_ada': w_ada[l], 'b_ada': b_ada[l], 'g_attn': g_attn[l], 'w_in': w_in[l],
             'g_q': g_q[l], 'g_k': g_k[l], 'sinks': sinks[l], 'ln_g': ln_g[l], 'ln_b': ln_b[l],
             'w_s': w_s[l], 'b_s': b_s[l], 'w_out': w_out[l], 'g_ffn': g_ffn[l],
             'w_ffn_in': w_ffn_in[l], 'conv_w': conv_w[l], 'conv_b': conv_b[l],
             'w_ffn_out': w_ffn_out[l]}
        zero_conv = jnp.zeros((xp.shape[0], CONV_W - 1, 2 * D_FF), xp.dtype)
        xp, kp, vp, _, cp = _layer(xp, c_prompt, pos_p, _swa_prompt, zero_conv, p)
        kp_l.append(kp[:, -WINDOW:])
        vp_l.append(vp[:, -WINDOW:])
        cp_l.append(cp)
        attend_s = functools.partial(_swa_sample, k_cache=cache_k[l], v_cache=cache_v[l])
        xs, ks_, vs_, gs_, cs_ = _layer(xs, c_sample, pos_s, attend_s, cache_conv[l], p)
        ks_l.append(ks_)
        vs_l.append(vs_)
        gs_l.append(gs_)
        cs_l.append(cs_)
    new_k_prompt = jnp.stack(kp_l)
    new_v_prompt = jnp.stack(vp_l)
    new_conv_prompt = jnp.stack(cp_l)
    new_k_sample = jnp.stack(ks_l)
    new_v_sample = jnp.stack(vs_l)
    new_gmlp_v_sample = jnp.stack(gs_l)
    new_conv_sample = jnp.stack(cs_l)
    return (xp, xs, new_k_prompt, new_v_prompt, new_conv_prompt,
            new_k_sample, new_v_sample, new_gmlp_v_sample, new_conv_sample)
```

```python
import jax
import jax.numpy as jnp
from jax import lax
from jax.experimental import pallas as pl
from jax.experimental.pallas import tpu as pltpu

D_MODEL = 1024
DEPTH = 2
HEAD_DIM = 64
N_HEADS = 8
N_KV_HEADS = 2
KV_WIDTH = N_KV_HEADS * HEAD_DIM
WINDOW = 128
ROPE_THETA = 500000.0
ROPE_DIMS = HEAD_DIM // 4
ATTN_WIDTH = N_HEADS * HEAD_DIM
GMLP_WIDTH = D_MODEL - ATTN_WIDTH
GMLP_GROUPS = 4
GMLP_GW = GMLP_WIDTH // GMLP_GROUPS
CHUNK = 128
IN_WIDTH = ATTN_WIDTH + 2 * KV_WIDTH + 2 * GMLP_WIDTH
D_FF = 2816
CONV_W = 3
EPS = 1e-6
NEG = -1e30
PAST_LEN = 16384

LANES = 128
SUBLANES = 8
N_PAIRS = N_HEADS // 2
VMEM_LIMIT = 56 * 1024 * 1024

PROMPT_TILE = 512
SAMPLE_SEQS = 64
FFN_CHUNKS = 11
ADA_TILE = 1536

F32 = jnp.float32
BF16 = jnp.bfloat16


def _dot(a, b):
    return jnp.dot(a, b, preferred_element_type=F32)


def _dot_nt(a, b):
    return lax.dot_general(a, b, (((1,), (1,)), ((), ())), preferred_element_type=F32)


def _const_spec(shape, single_buffer=False):
    nd = len(shape)
    if single_buffer:
        return pl.BlockSpec(shape, lambda *_: (0,) * nd, pipeline_mode=pl.Buffered(1))
    return pl.BlockSpec(shape, lambda *_: (0,) * nd)


def _ada_kernel(c_ref, w_ref, b_ref, o_ref):
    c = c_ref[...]
    a = (c * jax.nn.sigmoid(c)).astype(BF16)
    o_ref[0] = _dot(a, w_ref[0].astype(BF16)) + b_ref[0]


def _ada_call(c_all, w_ada, b_ada):
    n = c_all.shape[0]
    return pl.pallas_call(
        _ada_kernel,
        grid=(DEPTH, 6 * D_MODEL // ADA_TILE),
        in_specs=[
            pl.BlockSpec((n, D_MODEL), lambda l, j: (0, 0)),
            pl.BlockSpec((1, D_MODEL, ADA_TILE), lambda l, j: (l, 0, j)),
            pl.BlockSpec((1, 1, ADA_TILE), lambda l, j: (l, 0, j)),
        ],
        out_specs=pl.BlockSpec((1, n, ADA_TILE), lambda l, j: (l, 0, j)),
        out_shape=jax.ShapeDtypeStruct((DEPTH, n, 6 * D_MODEL), F32),
        compiler_params=pltpu.CompilerParams(
            dimension_semantics=("arbitrary", "arbitrary"), vmem_limit_bytes=VMEM_LIMIT),
        name="ada_mod",
    )(c_all, w_ada, b_ada.reshape(DEPTH, 1, 6 * D_MODEL))


def _modulated_rmsnorm(x, gain, scale, shift):
    r = lax.rsqrt(jnp.mean(x * x, axis=-1, keepdims=True) + EPS)
    return (x * r) * (gain * (1.0 + scale)) + shift


def _head_rmsnorm(z, ones_bd, gain):
    ssq = _dot((z * z).astype(BF16), ones_bd)
    return z * lax.rsqrt(ssq * (1.0 / HEAD_DIM) + EPS) * gain


def _rope(z, cos_t, sin_dn, sin_up):
    half = ROPE_DIMS // 2
    cols = []
    for p in range(z.shape[-1] // LANES):
        zp = z[:, p * LANES:(p + 1) * LANES]
        cols.append(zp * cos_t
                    + pltpu.roll(zp, LANES - half, axis=1) * sin_dn
                    + pltpu.roll(zp, half, axis=1) * sin_up)
    return cols[0] if len(cols) == 1 else jnp.concatenate(cols, axis=1)


def _split_heads(z):
    lane = lax.broadcasted_iota(jnp.int32, z.shape, 1)
    lo = lane < HEAD_DIM
    zs = pltpu.roll(z, HEAD_DIM, axis=1)
    zero = jnp.zeros_like(z)
    return (jnp.where(lo, z, zero), jnp.where(lo, zero, zs),
            jnp.where(lo, zs, zero), jnp.where(lo, zero, z))


def _gelu_layernorm(zv, ln_g, ln_b):
    gv = jax.nn.gelu(zv)
    cols = []
    for g in range(GMLP_GROUPS):
        xg = gv[:, g * GMLP_GW:(g + 1) * GMLP_GW]
        mu = jnp.mean(xg, axis=-1, keepdims=True)
        xc = xg - mu
        var = jnp.mean(xc * xc, axis=-1, keepdims=True)
        cols.append(xc * lax.rsqrt(var + EPS))
    return jnp.concatenate(cols, axis=1) * ln_g + ln_b


def _spatial_gate_into(mix_ref, u, vg, ws_ref, bs_ref):
    vgb = vg.astype(BF16)
    for g in range(GMLP_GROUPS):
        w = ws_ref[g]
        bias = bs_ref[:, g:g + 1]
        cs = slice(g * GMLP_GW, (g + 1) * GMLP_GW)
        for c in range(u.shape[0] // CHUNK):
            rs = slice(c * CHUNK, (c + 1) * CHUNK)
            z = _dot(w, vgb[rs, cs]) + bias
            mix_ref[rs, ATTN_WIDTH + g * GMLP_GW:ATTN_WIDTH + (g + 1) * GMLP_GW] = (
                u[rs, cs] * z).astype(BF16)


def _lane_pair(a, b, shape):
    lane = lax.broadcasted_iota(jnp.int32, shape, 1)
    return jnp.where(lane < HEAD_DIM, a, b)


def _expand_rows(dst_ref, src, reps):
    for c in range(dst_ref.shape[0]):
        piece = src[:, c * LANES:(c + 1) * LANES]
        for t in range(reps):
            dst_ref[c, pl.ds(t, src.shape[0], stride=reps), :] = piece


def _read_cols(ref, rows=slice(None)):
    return jnp.concatenate([ref[c, rows, :] for c in range(ref.shape[0])], axis=1)


def _mix_prompt_kernel(x_ref, sh_ref, sc_ref, ga_ref, gattn_ref, win_ref, gq_ref, gk_ref,
                       bdq_ref, bdk_ref, rope_ref, sinks_ref, lng_ref, lnb_ref, ws_ref, bs_ref,
                       wout_ref, y_ref, kout_ref, vout_ref, kprep, vprep, mix_ref):
    tt = x_ref.shape[1]
    t = pl.program_id(1)

    @pl.when(t == 0)
    def _():
        kprep[:, 0:WINDOW, :] = jnp.zeros((4, WINDOW, LANES), BF16)
        vprep[:, 0:WINDOW, :] = jnp.zeros((4, WINDOW, LANES), BF16)

    x = x_ref[0]
    h = _modulated_rmsnorm(x, gattn_ref[...], sc_ref[0], sh_ref[0])
    proj = _dot(h.astype(BF16), win_ref[...])

    cos_t, sin_dn, sin_up = rope_ref[0], rope_ref[1], rope_ref[2]
    q = _rope(_head_rmsnorm(proj[:, :ATTN_WIDTH], bdq_ref[...], gq_ref[...]), cos_t, sin_dn, sin_up)
    qb = (q * (HEAD_DIM ** -0.5)).astype(BF16)
    k = _rope(_head_rmsnorm(proj[:, ATTN_WIDTH:ATTN_WIDTH + KV_WIDTH], bdk_ref[...], gk_ref[...]),
              cos_t, sin_dn, sin_up)
    v = proj[:, ATTN_WIDTH + KV_WIDTH:ATTN_WIDTH + 2 * KV_WIDTH]
    kout_ref[0] = k[tt - WINDOW:, :]
    vout_ref[0] = v[tt - WINDOW:, :]
    for i, (kz, vz) in enumerate(zip(_split_heads(k), _split_heads(v))):
        kprep[i, WINDOW:, :] = kz.astype(BF16)
        vprep[i, WINDOW:, :] = vz.astype(BF16)

    row = lax.broadcasted_iota(jnp.int32, (WINDOW, 2 * WINDOW), 0)
    col = lax.broadcasted_iota(jnp.int32, (WINDOW, 2 * WINDOW), 1)
    band = (col - row >= 1) & (col - row <= WINDOW)
    band_first = band & (col + jnp.minimum(t, 1) * WINDOW >= WINDOW)

    for i in range(tt // WINDOW):
        mask = band_first if i == 0 else band
        rows = slice(i * WINDOW, (i + 1) * WINDOW)
        keys = slice(i * WINDOW, (i + 2) * WINDOW)
        for p in range(N_PAIRS):
            kv = p // 2
            kblk = jnp.concatenate([kprep[2 * kv, keys, :], kprep[2 * kv + 1, keys, :]], axis=0)
            vblk = jnp.concatenate([vprep[2 * kv, keys, :], vprep[2 * kv + 1, keys, :]], axis=0)
            s = _dot_nt(qb[rows, p * LANES:(p + 1) * LANES], kblk)
            es, invs = [], []
            for hh in range(2):
                sink = sinks_ref[2 * p + hh]
                sh_ = jnp.where(mask, s[:, hh * 2 * WINDOW:(hh + 1) * 2 * WINDOW], NEG)
                m = jnp.maximum(jnp.max(sh_, axis=-1, keepdims=True), sink)
                e = jnp.exp(sh_ - m)
                es.append(e)
                invs.append(1.0 / (jnp.sum(e, axis=-1, keepdims=True) + jnp.exp(sink - m)))
            o = _dot(jnp.concatenate(es, axis=1).astype(BF16), vblk)
            o = o * _lane_pair(invs[0], invs[1], o.shape)
            mix_ref[rows, p * LANES:(p + 1) * LANES] = o.astype(BF16)

    kprep[:, 0:WINDOW, :] = kprep[:, tt:tt + WINDOW, :]
    vprep[:, 0:WINDOW, :] = vprep[:, tt:tt + WINDOW, :]

    u = jax.nn.gelu(proj[:, ATTN_WIDTH + 2 * KV_WIDTH:ATTN_WIDTH + 2 * KV_WIDTH + GMLP_WIDTH])
    vg = _gelu_layernorm(proj[:, IN_WIDTH - GMLP_WIDTH:], lng_ref[...], lnb_ref[...])
    _spatial_gate_into(mix_ref, u, vg, ws_ref, bs_ref)

    y_ref[0] = x + ga_ref[0] * _dot(mix_ref[...], wout_ref[...])


def _mix_prompt_call(x, sh, sc, ga, lp, rope):
    b, t, _ = x.shape
    tt = PROMPT_TILE
    row_spec = pl.BlockSpec((1, 1, D_MODEL), lambda i, j: (i, 0, 0))
    kv_spec = pl.BlockSpec((1, WINDOW, KV_WIDTH), lambda i, j: (i, 0, 0))
    return pl.pallas_call(
        _mix_prompt_kernel,
        grid=(b, t // tt),
        in_specs=[
            pl.BlockSpec((1, tt, D_MODEL), lambda i, j: (i, j, 0)),
            row_spec, row_spec, row_spec,
            _const_spec((1, D_MODEL)),
            _const_spec((D_MODEL, IN_WIDTH)),
            _const_spec((1, ATTN_WIDTH)),
            _const_spec((1, KV_WIDTH)),
            _const_spec((ATTN_WIDTH, ATTN_WIDTH)),
            _const_spec((KV_WIDTH, KV_WIDTH)),
            pl.BlockSpec((3, tt, LANES), lambda i, j: (0, j, 0)),
            pl.BlockSpec(memory_space=pltpu.SMEM),
            _const_spec((1, GMLP_WIDTH)),
            _const_spec((1, GMLP_WIDTH)),
            _const_spec((GMLP_GROUPS, CHUNK, CHUNK)),
            _const_spec((CHUNK, GMLP_GROUPS)),
            _const_spec((D_MODEL, D_MODEL)),
        ],
        out_specs=[pl.BlockSpec((1, tt, D_MODEL), lambda i, j: (i, j, 0)), kv_spec, kv_spec],
        out_shape=[
            jax.ShapeDtypeStruct((b, t, D_MODEL), F32),
            jax.ShapeDtypeStruct((b, WINDOW, KV_WIDTH), F32),
            jax.ShapeDtypeStruct((b, WINDOW, KV_WIDTH), F32),
        ],
        scratch_shapes=[
            pltpu.VMEM((4, tt + WINDOW, LANES), BF16),
            pltpu.VMEM((4, tt + WINDOW, LANES), BF16),
            pltpu.VMEM((tt, D_MODEL), BF16),
        ],
        compiler_params=pltpu.CompilerParams(
            dimension_semantics=("arbitrary", "arbitrary"), vmem_limit_bytes=VMEM_LIMIT),
        name="mix_prompt",
    )(x, sh, sc, ga, lp["g_attn"], lp["w_in"], lp["g_q"], lp["g_k"], lp["bd_q"], lp["bd_k"],
      rope, lp["sinks"], lp["ln_g"], lp["ln_b"], lp["ws_prompt"], lp["bs_prompt"], lp["w_out"])


def _mix_sample_kernel(x_ref, sh_ref, sc_ref, ga_ref, ck_ref, cv_ref, gattn_ref, win_ref, gq_ref,
                       gk_ref, bdq_ref, bdk_ref, rope_ref, sinks_ref, lng_ref, lnb_ref, ws_ref,
                       bs_ref, wout_ref, y_ref, kout_ref, vout_ref, vgout_ref,
                       mod_scr, q_scr, s_scr, e_scr, o_scr, mix_ref):
    n = x_ref.shape[0]
    sb = sh_ref.shape[0]
    ts = n // sb
    _expand_rows(mod_scr.at[0], sh_ref[...], ts)
    _expand_rows(mod_scr.at[1], sc_ref[...], ts)
    x = x_ref[...]
    h = _modulated_rmsnorm(x, gattn_ref[...], _read_cols(mod_scr.at[1]), _read_cols(mod_scr.at[0]))
    proj = _dot(h.astype(BF16), win_ref[...])
    _expand_rows(mod_scr.at[0], ga_ref[...], ts)

    cos_t, sin_dn, sin_up = rope_ref[0], rope_ref[1], rope_ref[2]
    q = _rope(_head_rmsnorm(proj[:, :ATTN_WIDTH], bdq_ref[...], gq_ref[...]), cos_t, sin_dn, sin_up)
    q = q * (HEAD_DIM ** -0.5)
    q_scr[...] = q
    qb = q.astype(BF16)
    k = _rope(_head_rmsnorm(proj[:, ATTN_WIDTH:ATTN_WIDTH + KV_WIDTH], bdk_ref[...], gk_ref[...]),
              cos_t, sin_dn, sin_up)
    v = proj[:, ATTN_WIDTH + KV_WIDTH:ATTN_WIDTH + 2 * KV_WIDTH]
    kout_ref[...] = k
    vout_ref[...] = v

    def cache_scores(b, carry):
        rows = pl.ds(pl.multiple_of(b * ts, ts), ts)
        kparts = _split_heads(ck_ref[b])
        for kv in range(N_KV_HEADS):
            kblk = jnp.concatenate([kparts[2 * kv], kparts[2 * kv + 1]], axis=0).astype(BF16)
            q16 = jnp.concatenate(
                [q_scr[rows, (2 * kv) * LANES:(2 * kv + 1) * LANES],
                 q_scr[rows, (2 * kv + 1) * LANES:(2 * kv + 2) * LANES]], axis=0).astype(BF16)
            s = _dot_nt(q16, kblk)
            s_scr[2 * kv, rows, :] = s[:ts]
            s_scr[2 * kv + 1, rows, :] = s[ts:]
        return carry

    lax.fori_loop(0, sb, cache_scores, 0)

    knew = [z.astype(BF16) for z in _split_heads(k)]
    vnew = [z.astype(BF16) for z in _split_heads(v)]
    grp = CHUNK
    rown = lax.broadcasted_iota(jnp.int32, (grp, grp), 0)
    coln = lax.broadcasted_iota(jnp.int32, (grp, grp), 1)
    seq_bits = ts.bit_length() - 1
    mask_new = ((rown >> seq_bits) == (coln >> seq_bits)) & (coln <= rown)
    mask_cache = (lax.broadcasted_iota(jnp.int32, (grp, WINDOW), 1)
                  > (lax.broadcasted_iota(jnp.int32, (grp, WINDOW), 0) & (ts - 1)))

    for p in range(N_PAIRS):
        kv = p // 2
        for g0 in range(0, n, grp):
            rows = slice(g0, g0 + grp)
            kblk = jnp.concatenate([knew[2 * kv][rows], knew[2 * kv + 1][rows]], axis=0)
            vblk = jnp.concatenate([vnew[2 * kv][rows], vnew[2 * kv + 1][rows]], axis=0)
            s_new = _dot_nt(qb[rows, p * LANES:(p + 1) * LANES], kblk)
            e_new, invs = [], []
            for hh in range(2):
                sink = sinks_ref[2 * p + hh]
                sn = jnp.where(mask_new, s_new[:, hh * grp:(hh + 1) * grp], NEG)
                sc_ = jnp.where(mask_cache, s_scr[p, rows, hh * WINDOW:(hh + 1) * WINDOW], NEG)
                m = jnp.maximum(jnp.maximum(jnp.max(sn, axis=-1, keepdims=True),
                                            jnp.max(sc_, axis=-1, keepdims=True)), sink)
                en = jnp.exp(sn - m)
                ec = jnp.exp(sc_ - m)
                den = (jnp.sum(en, axis=-1, keepdims=True) + jnp.sum(ec, axis=-1, keepdims=True)
                       + jnp.exp(sink - m))
                e_new.append(en)
                invs.append(1.0 / den)
                e_scr[p, rows, hh * WINDOW:(hh + 1) * WINDOW] = ec
            o_scr[p, rows, :] = _dot(jnp.concatenate(e_new, axis=1).astype(BF16), vblk)
            o_scr[N_PAIRS + p, rows, :] = _lane_pair(invs[0], invs[1], (grp, LANES))

    def cache_pv(b, carry):
        rows = pl.ds(pl.multiple_of(b * ts, ts), ts)
        vparts = _split_heads(cv_ref[b])
        for kv in range(N_KV_HEADS):
            vblk = jnp.concatenate([vparts[2 * kv], vparts[2 * kv + 1]], axis=0).astype(BF16)
            e16 = jnp.concatenate([e_scr[2 * kv, rows, :], e_scr[2 * kv + 1, rows, :]],
                                  axis=0).astype(BF16)
            o = _dot(e16, vblk)
            o_scr[2 * kv, rows, :] += o[:ts]
            o_scr[2 * kv + 1, rows, :] += o[ts:]
        return carry

    lax.fori_loop(0, sb, cache_pv, 0)

    for p in range(N_PAIRS):
        mix_ref[:, p * LANES:(p + 1) * LANES] = (o_scr[p] * o_scr[N_PAIRS + p]).astype(BF16)

    u = jax.nn.gelu(proj[:, ATTN_WIDTH + 2 * KV_WIDTH:ATTN_WIDTH + 2 * KV_WIDTH + GMLP_WIDTH])
    vg = _gelu_layernorm(proj[:, IN_WIDTH - GMLP_WIDTH:], lng_ref[...], lnb_ref[...])
    vgout_ref[...] = vg
    _spatial_gate_into(mix_ref, u, vg, ws_ref, bs_ref)

    y_ref[...] = x + _read_cols(mod_scr.at[0]) * _dot(mix_ref[...], wout_ref[...])


def _mix_sample_call(x, sh, sc, ga, ck, cv, lp, rope, ts):
    nt = x.shape[0]
    sb = SAMPLE_SEQS
    n = sb * ts
    tok_spec = lambda w: pl.BlockSpec((n, w), lambda i: (i, 0))
    row_spec = pl.BlockSpec((sb, D_MODEL), lambda i: (i, 0))
    cache_spec = pl.BlockSpec((sb, WINDOW, KV_WIDTH), lambda i: (i, 0, 0))
    return pl.pallas_call(
        _mix_sample_kernel,
        grid=(nt // n,),
        in_specs=[
            tok_spec(D_MODEL), row_spec, row_spec, row_spec, cache_spec, cache_spec,
            _const_spec((1, D_MODEL)),
            _const_spec((D_MODEL, IN_WIDTH)),
            _const_spec((1, ATTN_WIDTH)),
            _const_spec((1, KV_WIDTH)),
            _const_spec((ATTN_WIDTH, ATTN_WIDTH)),
            _const_spec((KV_WIDTH, KV_WIDTH)),
            _const_spec((3, n, LANES)),
            pl.BlockSpec(memory_space=pltpu.SMEM),
            _const_spec((1, GMLP_WIDTH)),
            _const_spec((1, GMLP_WIDTH)),
            _const_spec((GMLP_GROUPS, CHUNK, CHUNK)),
            _const_spec((CHUNK, GMLP_GROUPS)),
            _const_spec((D_MODEL, D_MODEL)),
        ],
        out_specs=[tok_spec(D_MODEL), tok_spec(KV_WIDTH), tok_spec(KV_WIDTH), tok_spec(GMLP_WIDTH)],
        out_shape=[
            jax.ShapeDtypeStruct((nt, D_MODEL), F32),
            jax.ShapeDtypeStruct((nt, KV_WIDTH), F32),
            jax.ShapeDtypeStruct((nt, KV_WIDTH), F32),
            jax.ShapeDtypeStruct((nt, GMLP_WIDTH), F32),
        ],
        scratch_shapes=[
            pltpu.VMEM((2, D_MODEL // LANES, n, LANES), F32),
            pltpu.VMEM((n, ATTN_WIDTH), F32),
            pltpu.VMEM((N_PAIRS, n, 2 * WINDOW), F32),
            pltpu.VMEM((N_PAIRS, n, 2 * WINDOW), F32),
            pltpu.VMEM((2 * N_PAIRS, n, LANES), F32),
            pltpu.VMEM((n, D_MODEL), BF16),
        ],
        compiler_params=pltpu.CompilerParams(
            dimension_semantics=("arbitrary",), vmem_limit_bytes=VMEM_LIMIT),
        name="mix_sample",
    )(x, sh, sc, ga, ck, cv, lp["g_attn"], lp["w_in"], lp["g_q"], lp["g_k"], lp["bd_q"],
      lp["bd_k"], rope, lp["sinks"], lp["ln_g"], lp["ln_b"], lp["ws_sample"], lp["bs_sample"],
      lp["w_out"])


def _gated_down(conv, wo):
    cc = conv.shape[-1] // 2
    act = jax.nn.silu(conv[:, :cc]) * conv[:, cc:]
    return _dot(act.astype(BF16), wo)


def _ffn_prompt_kernel(x_ref, sh_ref, sc_ref, gf_ref, gffn_ref, wgu_ref, cw_ref, cb_ref, wo_ref,
                       y_ref, cout_ref, hbuf, carry, acc_ref):
    tt = x_ref.shape[1]
    nj, _, cc2 = wgu_ref.shape
    cc = cc2 // 2
    pad = SUBLANES
    t = pl.program_id(1)

    @pl.when(t == 0)
    def _():
        carry[...] = jnp.zeros(carry.shape, F32)

    x = x_ref[0]
    hb = _modulated_rmsnorm(x, gffn_ref[...], sc_ref[0], sh_ref[0]).astype(BF16)
    for j in range(nj):
        hu = _dot(hb, wgu_ref[j])
        hbuf[0:pad, :] = carry[j]
        hbuf[pad:, :] = hu
        carry[j] = hbuf[tt:tt + pad, :]
        cw = cw_ref[j]
        conv = (cb_ref[j] + cw[0:1] * hbuf[pad - 2:pad - 2 + tt, :]
                + cw[1:2] * hbuf[pad - 1:pad - 1 + tt, :] + cw[2:3] * hu)
        contrib = _gated_down(conv, wo_ref[j])
        if j == 0:
            acc_ref[...] = contrib
        else:
            acc_ref[...] += contrib
        last = hbuf[tt + pad - (CONV_W - 1):tt + pad, :]
        cout_ref[0, :, j * cc:(j + 1) * cc] = last[:, :cc]
        cout_ref[0, :, D_FF + j * cc:D_FF + (j + 1) * cc] = last[:, cc:]
    y_ref[0] = x + gf_ref[0] * acc_ref[...]


def _ffn_weight_specs():
    nj, cc = FFN_CHUNKS, D_FF // FFN_CHUNKS
    return [
        _const_spec((1, D_MODEL)),
        _const_spec((nj, D_MODEL, 2 * cc), single_buffer=True),
        _const_spec((nj, CONV_W, 2 * cc)),
        _const_spec((nj, 1, 2 * cc)),
        _const_spec((nj, cc, D_MODEL), single_buffer=True),
    ]


def _ffn_prompt_call(x, sh, sc, gf, lp):
    b, t, _ = x.shape
    tt = PROMPT_TILE
    nj, cc = FFN_CHUNKS, D_FF // FFN_CHUNKS
    row_spec = pl.BlockSpec((1, 1, D_MODEL), lambda i, j: (i, 0, 0))
    return pl.pallas_call(
        _ffn_prompt_kernel,
        grid=(b, t // tt),
        in_specs=[pl.BlockSpec((1, tt, D_MODEL), lambda i, j: (i, j, 0)),
                  row_spec, row_spec, row_spec] + _ffn_weight_specs(),
        out_specs=[
            pl.BlockSpec((1, tt, D_MODEL), lambda i, j: (i, j, 0)),
            pl.BlockSpec((1, CONV_W - 1, 2 * D_FF), lambda i, j: (i, 0, 0)),
        ],
        out_shape=[
            jax.ShapeDtypeStruct((b, t, D_MODEL), F32),
            jax.ShapeDtypeStruct((b, CONV_W - 1, 2 * D_FF), F32),
        ],
        scratch_shapes=[
            pltpu.VMEM((tt + SUBLANES, 2 * cc), F32),
            pltpu.VMEM((nj, SUBLANES, 2 * cc), F32),
            pltpu.VMEM((tt, D_MODEL), F32),
        ],
        compiler_params=pltpu.CompilerParams(
            dimension_semantics=("arbitrary", "arbitrary"), vmem_limit_bytes=VMEM_LIMIT),
        name="ffn_prompt",
    )(x, sh, sc, gf, lp["g_ffn"], lp["w_gu"], lp["conv_w"], lp["conv_b"], lp["w_down"])


def _ffn_sample_kernel(x_ref, sh_ref, sc_ref, gf_ref, c0_ref, c1_ref, gffn_ref, wgu_ref, cw_ref,
                       cb_ref, wo_ref, y_ref, o0_ref, o1_ref, mod_scr, h1, h2, acc_ref):
    n = x_ref.shape[0]
    sb = sh_ref.shape[0]
    ts = n // sb
    nj, _, cc2 = wgu_ref.shape
    cc = cc2 // 2
    pad = SUBLANES
    _expand_rows(mod_scr.at[0], sh_ref[...], ts)
    _expand_rows(mod_scr.at[1], sc_ref[...], ts)
    x = x_ref[...]
    hb = _modulated_rmsnorm(x, gffn_ref[...], _read_cols(mod_scr.at[1]),
                            _read_cols(mod_scr.at[0])).astype(BF16)
    _expand_rows(mod_scr.at[0], gf_ref[...], ts)
    for j in range(nj):
        offs = [half * D_FF + j * cc + c * LANES for half in range(2) for c in range(cc // LANES)]
        hu = _dot(hb, wgu_ref[j])
        for c, off in enumerate(offs):
            piece = hu[:, c * LANES:(c + 1) * LANES]
            h1[c, pad:, :] = piece
            h2[c, pad:, :] = piece
            o0_ref[:, off:off + LANES] = h2[c, pl.ds(pad + ts - 2, sb, stride=ts), :]
            o1_ref[:, off:off + LANES] = h2[c, pl.ds(pad + ts - 1, sb, stride=ts), :]
            c0 = c0_ref[:, off:off + LANES]
            c1 = c1_ref[:, off:off + LANES]
            h1[c, pl.ds(pad - 1, sb, stride=ts), :] = c1
            h2[c, pl.ds(pad - 2, sb, stride=ts), :] = c0
            h2[c, pl.ds(pad - 1, sb, stride=ts), :] = c1
        cw = cw_ref[j]
        conv = (cb_ref[j] + cw[0:1] * _read_cols(h2, slice(pad - 2, pad - 2 + n))
                + cw[1:2] * _read_cols(h1, slice(pad - 1, pad - 1 + n)) + cw[2:3] * hu)
        contrib = _gated_down(conv, wo_ref[j])
        if j == 0:
            acc_ref[...] = contrib
        else:
            acc_ref[...] += contrib
    y_ref[...] = x + _read_cols(mod_scr.at[0]) * acc_ref[...]


def _ffn_sample_call(x, sh, sc, gf, c0, c1, lp, ts):
    nt = x.shape[0]
    sb = SAMPLE_SEQS
    n = sb * ts
    cc = D_FF // FFN_CHUNKS
    tok_spec = pl.BlockSpec((n, D_MODEL), lambda i: (i, 0))
    row_spec = pl.BlockSpec((sb, D_MODEL), lambda i: (i, 0))
    conv_spec = pl.BlockSpec((sb, 2 * D_FF), lambda i: (i, 0))
    return pl.pallas_call(
        _ffn_sample_kernel,
        grid=(nt // n,),
        in_specs=[tok_spec, row_spec, row_spec, row_spec, conv_spec, conv_spec]
        + _ffn_weight_specs(),
        out_specs=[tok_spec, conv_spec, conv_spec],
        out_shape=[
            jax.ShapeDtypeStruct((nt, D_MODEL), F32),
            jax.ShapeDtypeStruct(c0.shape, F32),
            jax.ShapeDtypeStruct(c1.shape, F32),
        ],
        scratch_shapes=[
            pltpu.VMEM((2, D_MODEL // LANES, n, LANES), F32),
            pltpu.VMEM((2 * cc // LANES, n + SUBLANES, LANES), F32),
            pltpu.VMEM((2 * cc // LANES, n + SUBLANES, LANES), F32),
            pltpu.VMEM((n, D_MODEL), F32),
        ],
        compiler_params=pltpu.CompilerParams(
            dimension_semantics=("arbitrary",), vmem_limit_bytes=VMEM_LIMIT),
        name="ffn_sample",
    )(x, sh, sc, gf, c0, c1, lp["g_ffn"], lp["w_gu"], lp["conv_w"], lp["conv_b"], lp["w_down"])


def _rope_tables(pos):
    half = ROPE_DIMS // 2
    inv = ROPE_THETA ** (-jnp.arange(0, ROPE_DIMS, 2, dtype=F32) / ROPE_DIMS)
    ang = pos.astype(F32)[:, None] * inv[None, :]
    cos, sin = jnp.cos(ang), jnp.sin(ang)
    n = pos.shape[0]
    rest = jnp.zeros((n, HEAD_DIM - ROPE_DIMS), F32)
    zeros = jnp.zeros((n, half), F32)
    cos_t = jnp.concatenate([cos, cos, rest + 1.0], axis=1)
    sin_dn = jnp.concatenate([-sin, zeros, rest], axis=1)
    sin_up = jnp.concatenate([zeros, sin, rest], axis=1)
    return jnp.stack([jnp.tile(z, (1, LANES // HEAD_DIM)) for z in (cos_t, sin_dn, sin_up)])


def _block_diag_ones(width):
    idx = jnp.arange(width) // HEAD_DIM
    return (idx[:, None] == idx[None, :]).astype(BF16)


def _layer_params(l, dec_seq, w_in, g_attn, g_q, g_k, sinks, ln_g, ln_b, w_s, b_s, w_out,
                  g_ffn, w_ffn_in, conv_w, conv_b, w_ffn_out):
    nj, cc = FFN_CHUNKS, D_FF // FFN_CHUNKS
    causal = jnp.tril(jnp.ones((CHUNK, CHUNK), dtype=bool))
    ws = jnp.where(causal[None], w_s[l], 0.0)
    seqs_per_chunk = CHUNK // dec_seq
    eye = jnp.eye(seqs_per_chunk, dtype=F32)
    ws_sample = jnp.einsum("ab,gts->gatbs", eye, ws[:, :dec_seq, :dec_seq]).reshape(
        GMLP_GROUPS, CHUNK, CHUNK)

    def chunked(w):
        return jnp.moveaxis(w.reshape(w.shape[:-1] + (nj, cc)), -2, 0)

    def gate_up(w):
        return jnp.concatenate([chunked(w[..., :D_FF]), chunked(w[..., D_FF:])], axis=-1)

    return {
        "g_attn": g_attn[l][None], "g_ffn": g_ffn[l][None],
        "w_in": w_in[l].astype(BF16), "w_out": w_out[l].astype(BF16),
        "g_q": jnp.tile(g_q[l], N_HEADS)[None], "g_k": jnp.tile(g_k[l], N_KV_HEADS)[None],
        "bd_q": _block_diag_ones(ATTN_WIDTH), "bd_k": _block_diag_ones(KV_WIDTH),
        "sinks": sinks[l],
        "ln_g": ln_g[l].reshape(1, GMLP_WIDTH), "ln_b": ln_b[l].reshape(1, GMLP_WIDTH),
        "ws_prompt": ws.astype(BF16), "bs_prompt": b_s[l].T,
        "ws_sample": ws_sample.astype(BF16),
        "bs_sample": jnp.tile(b_s[l][:, :dec_seq], (1, seqs_per_chunk)).T,
        "w_gu": gate_up(w_ffn_in[l]).astype(BF16),
        "conv_w": gate_up(conv_w[l]), "conv_b": gate_up(conv_b[l][None]),
        "w_down": w_ffn_out[l].reshape(nj, cc, D_MODEL).astype(BF16),
    }


def kernel(x_prompt, x_sample, cache_k, cache_v, cache_conv, c_prompt, c_sample, w_ada, b_ada,
           g_attn, w_in, g_q, g_k, sinks, ln_g, ln_b, w_s, b_s, w_out, g_ffn, w_ffn_in, conv_w,
           conv_b, w_ffn_out):
    nbp, seq, _ = x_prompt.shape
    nbs, dec_seq, _ = x_sample.shape
    assert seq % PROMPT_TILE == 0 and PROMPT_TILE % CHUNK == 0 and nbs % SAMPLE_SEQS == 0
    assert dec_seq == SUBLANES and CHUNK % dec_seq == 0 and cache_k.shape[2] == WINDOW

    n_c = nbp + nbs
    c_all = jnp.concatenate([c_prompt, c_sample, jnp.zeros((-n_c % SUBLANES, D_MODEL), F32)])
    mod = _ada_call(c_all, w_ada, b_ada)
    mod_p = mod[:, :nbp].reshape(DEPTH, nbp, 6, 1, D_MODEL)
    mod_s = mod[:, nbp:n_c].reshape(DEPTH, nbs, 6, D_MODEL)

    rope_p = _rope_tables(jnp.arange(seq, dtype=jnp.int32))
    rope_s = jnp.tile(_rope_tables(PAST_LEN + jnp.arange(dec_seq, dtype=jnp.int32)),
                      (1, SAMPLE_SEQS, 1))

    xp = x_prompt
    xs = x_sample.reshape(nbs * dec_seq, D_MODEL)
    outs = [[] for _ in range(7)]
    for l in range(DEPTH):
        lp = _layer_params(l, dec_seq, w_in, g_attn, g_q, g_k, sinks, ln_g, ln_b, w_s, b_s, w_out,
                           g_ffn, w_ffn_in, conv_w, conv_b, w_ffn_out)
        mp = [mod_p[l, :, i] for i in range(6)]
        ms = [mod_s[l, :, i] for i in range(6)]

        xp, kp, vp = _mix_prompt_call(xp, mp[0], mp[1], mp[2], lp, rope_p)
        xp, cp = _ffn_prompt_call(xp, mp[3], mp[4], mp[5], lp)

        ck = cache_k[l].reshape(nbs, WINDOW, KV_WIDTH)
        cv = cache_v[l].reshape(nbs, WINDOW, KV_WIDTH)
        xs, ks, vs, gs = _mix_sample_call(xs, ms[0], ms[1], ms[2], ck, cv, lp, rope_s, dec_seq)
        xs, cs0, cs1 = _ffn_sample_call(xs, ms[3], ms[4], ms[5], cache_conv[l, :, 0],
                                        cache_conv[l, :, 1], lp, dec_seq)

        outs[0].append(kp.reshape(nbp, WINDOW, N_KV_HEADS, HEAD_DIM))
        outs[1].append(vp.reshape(nbp, WINDOW, N_KV_HEADS, HEAD_DIM))
        outs[2].append(cp)
        outs[3].append(ks.reshape(nbs, dec_seq, N_KV_HEADS, HEAD_DIM))
        outs[4].append(vs.reshape(nbs, dec_seq, N_KV_HEADS, HEAD_DIM))
        outs[5].append(gs.reshape(nbs, dec_seq, GMLP_WIDTH))
        outs[6].append(jnp.stack([cs0, cs1], axis=1))
    return (xp, xs.reshape(nbs, dec_seq, D_MODEL)) + tuple(jnp.stack(o) for o in outs)
```

```python
import jax
import jax.numpy as jnp
from jax import lax
from jax.experimental import pallas as pl
from jax.experimental.pallas import tpu as pltpu

D_MODEL = 1024
DEPTH = 2
HEAD_DIM = 64
N_HEADS = 8
N_KV_HEADS = 2
KV_WIDTH = N_KV_HEADS * HEAD_DIM
WINDOW = 128
ROPE_THETA = 500000.0
ROPE_DIMS = HEAD_DIM // 4
ATTN_WIDTH = N_HEADS * HEAD_DIM
GMLP_WIDTH = D_MODEL - ATTN_WIDTH
GMLP_GROUPS = 4
GMLP_GW = GMLP_WIDTH // GMLP_GROUPS
CHUNK = 128
IN_WIDTH = ATTN_WIDTH + 2 * KV_WIDTH + 2 * GMLP_WIDTH
D_FF = 2816
CONV_W = 3
EPS = 1e-6
NEG = -1e30
PAST_LEN = 16384

LANES = 128
SUBLANES = 8
N_PAIRS = N_HEADS // 2
VMEM_LIMIT = 56 * 1024 * 1024

PROMPT_TILE = 512
FFN_TILE = 1024
SAMPLE_SEQS = 64
FFN_CHUNKS = 11
DOWN_GROUP = 2
ADA_TILE = 1536

F32 = jnp.float32
BF16 = jnp.bfloat16


def _dot(a, b):
    return jnp.dot(a, b, preferred_element_type=F32)


def _dot_nt(a, b):
    return lax.dot_general(a, b, (((1,), (1,)), ((), ())), preferred_element_type=F32)


def _const_spec(shape, single_buffer=False):
    nd = len(shape)
    if single_buffer:
        return pl.BlockSpec(shape, lambda *_: (0,) * nd, pipeline_mode=pl.Buffered(1))
    return pl.BlockSpec(shape, lambda *_: (0,) * nd)


def _ada_kernel(c_ref, w_ref, b_ref, o_ref):
    c = c_ref[...]
    a = (c * jax.nn.sigmoid(c)).astype(BF16)
    o_ref[0] = _dot(a, w_ref[0].astype(BF16)) + b_ref[0]


def _ada_call(c_all, w_ada, b_ada):
    n = c_all.shape[0]
    return pl.pallas_call(
        _ada_kernel,
        grid=(DEPTH, 6 * D_MODEL // ADA_TILE),
        in_specs=[
            pl.BlockSpec((n, D_MODEL), lambda l, j: (0, 0)),
            pl.BlockSpec((1, D_MODEL, ADA_TILE), lambda l, j: (l, 0, j)),
            pl.BlockSpec((1, 1, ADA_TILE), lambda l, j: (l, 0, j)),
        ],
        out_specs=pl.BlockSpec((1, n, ADA_TILE), lambda l, j: (l, 0, j)),
        out_shape=jax.ShapeDtypeStruct((DEPTH, n, 6 * D_MODEL), F32),
        compiler_params=pltpu.CompilerParams(
            dimension_semantics=("arbitrary", "arbitrary"), vmem_limit_bytes=VMEM_LIMIT),
        name="ada_mod",
    )(c_all, w_ada, b_ada.reshape(DEPTH, 1, 6 * D_MODEL))


def _modulated_rmsnorm(x, gain, scale, shift):
    r = lax.rsqrt(jnp.mean(x * x, axis=-1, keepdims=True) + EPS)
    return (x * r) * (gain * (1.0 + scale)) + shift


def _head_rmsnorm(z, ones_bd, gain):
    ssq = _dot((z * z).astype(BF16), ones_bd)
    return z * lax.rsqrt(ssq * (1.0 / HEAD_DIM) + EPS) * gain


def _rope(z, cos_t, sin_dn, sin_up):
    half = ROPE_DIMS // 2
    cols = []
    for p in range(z.shape[-1] // LANES):
        zp = z[:, p * LANES:(p + 1) * LANES]
        cols.append(zp * cos_t
                    + pltpu.roll(zp, LANES - half, axis=1) * sin_dn
                    + pltpu.roll(zp, half, axis=1) * sin_up)
    return cols[0] if len(cols) == 1 else jnp.concatenate(cols, axis=1)


def _split_heads(z):
    lane = lax.broadcasted_iota(jnp.int32, z.shape, 1)
    lo = lane < HEAD_DIM
    zs = pltpu.roll(z, HEAD_DIM, axis=1)
    zero = jnp.zeros_like(z)
    return (jnp.where(lo, z, zero), jnp.where(lo, zero, zs),
            jnp.where(lo, zs, zero), jnp.where(lo, zero, z))


def _gelu_layernorm(zv, ln_g, ln_b):
    gv = jax.nn.gelu(zv)
    cols = []
    for g in range(GMLP_GROUPS):
        xg = gv[:, g * GMLP_GW:(g + 1) * GMLP_GW]
        mu = jnp.mean(xg, axis=-1, keepdims=True)
        xc = xg - mu
        var = jnp.mean(xc * xc, axis=-1, keepdims=True)
        cols.append(xc * lax.rsqrt(var + EPS))
    return jnp.concatenate(cols, axis=1) * ln_g + ln_b


def _spatial_gate_into(mix_ref, u, vg, ws_ref, bs_ref):
    vgb = vg.astype(BF16)
    for g in range(GMLP_GROUPS):
        w = ws_ref[g]
        bias = bs_ref[:, g:g + 1]
        cs = slice(g * GMLP_GW, (g + 1) * GMLP_GW)
        for c in range(u.shape[0] // CHUNK):
            rs = slice(c * CHUNK, (c + 1) * CHUNK)
            z = _dot(w, vgb[rs, cs]) + bias
            mix_ref[rs, ATTN_WIDTH + g * GMLP_GW:ATTN_WIDTH + (g + 1) * GMLP_GW] = (
                u[rs, cs] * z).astype(BF16)


def _lane_pair(a, b, shape):
    lane = lax.broadcasted_iota(jnp.int32, shape, 1)
    return jnp.where(lane < HEAD_DIM, a, b)


def _expand_rows(dst_ref, src, reps):
    for c in range(dst_ref.shape[0]):
        piece = src[:, c * LANES:(c + 1) * LANES]
        for t in range(reps):
            dst_ref[c, pl.ds(t, src.shape[0], stride=reps), :] = piece


def _read_cols(ref, rows=slice(None)):
    return jnp.concatenate([ref[c, rows, :] for c in range(ref.shape[0])], axis=1)


def _mix_prompt_kernel(x_ref, sh_ref, sc_ref, ga_ref, gattn_ref, win_ref, gq_ref, gk_ref,
                       bdq_ref, bdk_ref, rope_ref, sinks_ref, lng_ref, lnb_ref, ws_ref, bs_ref,
                       wout_ref, y_ref, kout_ref, vout_ref, kprep, vprep, mix_ref):
    tt = x_ref.shape[1]
    t = pl.program_id(1)

    @pl.when(t == 0)
    def _():
        kprep[:, 0:WINDOW, :] = jnp.zeros((4, WINDOW, LANES), BF16)
        vprep[:, 0:WINDOW, :] = jnp.zeros((4, WINDOW, LANES), BF16)

    x = x_ref[0]
    h = _modulated_rmsnorm(x, gattn_ref[...], sc_ref[0], sh_ref[0])
    proj = _dot(h.astype(BF16), win_ref[...])

    cos_t, sin_dn, sin_up = rope_ref[0], rope_ref[1], rope_ref[2]
    q = _rope(_head_rmsnorm(proj[:, :ATTN_WIDTH], bdq_ref[...], gq_ref[...]), cos_t, sin_dn, sin_up)
    qb = (q * (HEAD_DIM ** -0.5)).astype(BF16)
    k = _rope(_head_rmsnorm(proj[:, ATTN_WIDTH:ATTN_WIDTH + KV_WIDTH], bdk_ref[...], gk_ref[...]),
              cos_t, sin_dn, sin_up)
    v = proj[:, ATTN_WIDTH + KV_WIDTH:ATTN_WIDTH + 2 * KV_WIDTH]
    kout_ref[0] = k[tt - WINDOW:, :]
    vout_ref[0] = v[tt - WINDOW:, :]
    for i, (kz, vz) in enumerate(zip(_split_heads(k), _split_heads(v))):
        kprep[i, WINDOW:, :] = kz.astype(BF16)
        vprep[i, WINDOW:, :] = vz.astype(BF16)

    row = lax.broadcasted_iota(jnp.int32, (WINDOW, 2 * WINDOW), 0)
    col = lax.broadcasted_iota(jnp.int32, (WINDOW, 2 * WINDOW), 1)
    band = (col - row >= 1) & (col - row <= WINDOW)
    band_first = band & (col + jnp.minimum(t, 1) * WINDOW >= WINDOW)

    for i in range(tt // WINDOW):
        mask = band_first if i == 0 else band
        rows = slice(i * WINDOW, (i + 1) * WINDOW)
        keys = slice(i * WINDOW, (i + 2) * WINDOW)
        for p in range(N_PAIRS):
            kv = p // 2
            kblk = jnp.concatenate([kprep[2 * kv, keys, :], kprep[2 * kv + 1, keys, :]], axis=0)
            vblk = jnp.concatenate([vprep[2 * kv, keys, :], vprep[2 * kv + 1, keys, :]], axis=0)
            s = _dot_nt(qb[rows, p * LANES:(p + 1) * LANES], kblk)
            es, invs = [], []
            for hh in range(2):
                sink = sinks_ref[2 * p + hh]
                sh_ = jnp.where(mask, s[:, hh * 2 * WINDOW:(hh + 1) * 2 * WINDOW], NEG)
                m = jnp.maximum(jnp.max(sh_, axis=-1, keepdims=True), sink)
                e = jnp.exp(sh_ - m)
                es.append(e)
                invs.append(1.0 / (jnp.sum(e, axis=-1, keepdims=True) + jnp.exp(sink - m)))
            o = _dot(jnp.concatenate(es, axis=1).astype(BF16), vblk)
            o = o * _lane_pair(invs[0], invs[1], o.shape)
            mix_ref[rows, p * LANES:(p + 1) * LANES] = o.astype(BF16)

    kprep[:, 0:WINDOW, :] = kprep[:, tt:tt + WINDOW, :]
    vprep[:, 0:WINDOW, :] = vprep[:, tt:tt + WINDOW, :]

    u = jax.nn.gelu(proj[:, ATTN_WIDTH + 2 * KV_WIDTH:ATTN_WIDTH + 2 * KV_WIDTH + GMLP_WIDTH])
    vg = _gelu_layernorm(proj[:, IN_WIDTH - GMLP_WIDTH:], lng_ref[...], lnb_ref[...])
    _spatial_gate_into(mix_ref, u, vg, ws_ref, bs_ref)

    y_ref[0] = x + ga_ref[0] * _dot(mix_ref[...], wout_ref[...])


def _mix_prompt_call(x, sh, sc, ga, lp, rope):
    b, t, _ = x.shape
    tt = PROMPT_TILE
    row_spec = pl.BlockSpec((1, 1, D_MODEL), lambda i, j: (i, 0, 0))
    kv_spec = pl.BlockSpec((1, WINDOW, KV_WIDTH), lambda i, j: (i, 0, 0))
    return pl.pallas_call(
        _mix_prompt_kernel,
        grid=(b, t // tt),
        in_specs=[
            pl.BlockSpec((1, tt, D_MODEL), lambda i, j: (i, j, 0)),
            row_spec, row_spec, row_spec,
            _const_spec((1, D_MODEL)),
            _const_spec((D_MODEL, IN_WIDTH)),
            _const_spec((1, ATTN_WIDTH)),
            _const_spec((1, KV_WIDTH)),
            _const_spec((ATTN_WIDTH, ATTN_WIDTH)),
            _const_spec((KV_WIDTH, KV_WIDTH)),
            pl.BlockSpec((3, tt, LANES), lambda i, j: (0, j, 0)),
            pl.BlockSpec(memory_space=pltpu.SMEM),
            _const_spec((1, GMLP_WIDTH)),
            _const_spec((1, GMLP_WIDTH)),
            _const_spec((GMLP_GROUPS, CHUNK, CHUNK)),
            _const_spec((CHUNK, GMLP_GROUPS)),
            _const_spec((D_MODEL, D_MODEL)),
        ],
        out_specs=[pl.BlockSpec((1, tt, D_MODEL), lambda i, j: (i, j, 0)), kv_spec, kv_spec],
        out_shape=[
            jax.ShapeDtypeStruct((b, t, D_MODEL), F32),
            jax.ShapeDtypeStruct((b, WINDOW, KV_WIDTH), F32),
            jax.ShapeDtypeStruct((b, WINDOW, KV_WIDTH), F32),
        ],
        scratch_shapes=[
            pltpu.VMEM((4, tt + WINDOW, LANES), BF16),
            pltpu.VMEM((4, tt + WINDOW, LANES), BF16),
            pltpu.VMEM((tt, D_MODEL), BF16),
        ],
        compiler_params=pltpu.CompilerParams(
            dimension_semantics=("arbitrary", "arbitrary"), vmem_limit_bytes=VMEM_LIMIT),
        name="mix_prompt",
    )(x, sh, sc, ga, lp["g_attn"], lp["w_in"], lp["g_q"], lp["g_k"], lp["bd_q"], lp["bd_k"],
      rope, lp["sinks"], lp["ln_g"], lp["ln_b"], lp["ws_prompt"], lp["bs_prompt"], lp["w_out"])


def _mix_sample_kernel(x_ref, sh_ref, sc_ref, ga_ref, ck_ref, cv_ref, gattn_ref, win_ref, gq_ref,
                       gk_ref, bdq_ref, bdk_ref, rope_ref, sinks_ref, lng_ref, lnb_ref, ws_ref,
                       bs_ref, wout_ref, y_ref, kout_ref, vout_ref, vgout_ref,
                       mod_scr, q_scr, s_scr, e_scr, o_scr, mix_ref):
    n = x_ref.shape[0]
    sb = sh_ref.shape[0]
    ts = n // sb
    _expand_rows(mod_scr.at[0], sh_ref[...], ts)
    _expand_rows(mod_scr.at[1], sc_ref[...], ts)
    x = x_ref[...]
    h = _modulated_rmsnorm(x, gattn_ref[...], _read_cols(mod_scr.at[1]), _read_cols(mod_scr.at[0]))
    proj = _dot(h.astype(BF16), win_ref[...])
    _expand_rows(mod_scr.at[0], ga_ref[...], ts)

    cos_t, sin_dn, sin_up = rope_ref[0], rope_ref[1], rope_ref[2]
    q = _rope(_head_rmsnorm(proj[:, :ATTN_WIDTH], bdq_ref[...], gq_ref[...]), cos_t, sin_dn, sin_up)
    q = q * (HEAD_DIM ** -0.5)
    q_scr[...] = q
    qb = q.astype(BF16)
    k = _rope(_head_rmsnorm(proj[:, ATTN_WIDTH:ATTN_WIDTH + KV_WIDTH], bdk_ref[...], gk_ref[...]),
              cos_t, sin_dn, sin_up)
    v = proj[:, ATTN_WIDTH + KV_WIDTH:ATTN_WIDTH + 2 * KV_WIDTH]
    kout_ref[...] = k
    vout_ref[...] = v

    def cache_scores(b, carry):
        rows = pl.ds(pl.multiple_of(b * ts, ts), ts)
        kparts = _split_heads(ck_ref[b])
        for kv in range(N_KV_HEADS):
            kblk = jnp.concatenate([kparts[2 * kv], kparts[2 * kv + 1]], axis=0).astype(BF16)
            q16 = jnp.concatenate(
                [q_scr[rows, (2 * kv) * LANES:(2 * kv + 1) * LANES],
                 q_scr[rows, (2 * kv + 1) * LANES:(2 * kv + 2) * LANES]], axis=0).astype(BF16)
            s = _dot_nt(q16, kblk)
            s_scr[2 * kv, rows, :] = s[:ts]
            s_scr[2 * kv + 1, rows, :] = s[ts:]
        return carry

    lax.fori_loop(0, sb, cache_scores, 0)

    knew = [z.astype(BF16) for z in _split_heads(k)]
    vnew = [z.astype(BF16) for z in _split_heads(v)]
    grp = CHUNK
    rown = lax.broadcasted_iota(jnp.int32, (grp, grp), 0)
    coln = lax.broadcasted_iota(jnp.int32, (grp, grp), 1)
    seq_bits = ts.bit_length() - 1
    mask_new = ((rown >> seq_bits) == (coln >> seq_bits)) & (coln <= rown)
    mask_cache = (lax.broadcasted_iota(jnp.int32, (grp, WINDOW), 1)
                  > (lax.broadcasted_iota(jnp.int32, (grp, WINDOW), 0) & (ts - 1)))

    for p in range(N_PAIRS):
        kv = p // 2
        for g0 in range(0, n, grp):
            rows = slice(g0, g0 + grp)
            kblk = jnp.concatenate([knew[2 * kv][rows], knew[2 * kv + 1][rows]], axis=0)
            vblk = jnp.concatenate([vnew[2 * kv][rows], vnew[2 * kv + 1][rows]], axis=0)
            s_new = _dot_nt(qb[rows, p * LANES:(p + 1) * LANES], kblk)
            e_new, invs = [], []
            for hh in range(2):
                sink = sinks_ref[2 * p + hh]
                sn = jnp.where(mask_new, s_new[:, hh * grp:(hh + 1) * grp], NEG)
                sc_ = jnp.where(mask_cache, s_scr[p, rows, hh * WINDOW:(hh + 1) * WINDOW], NEG)
                m = jnp.maximum(jnp.maximum(jnp.max(sn, axis=-1, keepdims=True),
                                            jnp.max(sc_, axis=-1, keepdims=True)), sink)
                en = jnp.exp(sn - m)
                ec = jnp.exp(sc_ - m)
                den = (jnp.sum(en, axis=-1, keepdims=True) + jnp.sum(ec, axis=-1, keepdims=True)
                       + jnp.exp(sink - m))
                e_new.append(en)
                invs.append(1.0 / den)
                e_scr[p, rows, hh * WINDOW:(hh + 1) * WINDOW] = ec
            o_scr[p, rows, :] = _dot(jnp.concatenate(e_new, axis=1).astype(BF16), vblk)
            o_scr[N_PAIRS + p, rows, :] = _lane_pair(invs[0], invs[1], (grp, LANES))

    def cache_pv(b, carry):
        rows = pl.ds(pl.multiple_of(b * ts, ts), ts)
        vparts = _split_heads(cv_ref[b])
        for kv in range(N_KV_HEADS):
            vblk = jnp.concatenate([vparts[2 * kv], vparts[2 * kv + 1]], axis=0).astype(BF16)
            e16 = jnp.concatenate([e_scr[2 * kv, rows, :], e_scr[2 * kv + 1, rows, :]],
                                  axis=0).astype(BF16)
            o = _dot(e16, vblk)
            o_scr[2 * kv, rows, :] += o[:ts]
            o_scr[2 * kv + 1, rows, :] += o[ts:]
        return carry

    lax.fori_loop(0, sb, cache_pv, 0)

    for p in range(N_PAIRS):
        mix_ref[:, p * LANES:(p + 1) * LANES] = (o_scr[p] * o_scr[N_PAIRS + p]).astype(BF16)

    u = jax.nn.gelu(proj[:, ATTN_WIDTH + 2 * KV_WIDTH:ATTN_WIDTH + 2 * KV_WIDTH + GMLP_WIDTH])
    vg = _gelu_layernorm(proj[:, IN_WIDTH - GMLP_WIDTH:], lng_ref[...], lnb_ref[...])
    vgout_ref[...] = vg
    _spatial_gate_into(mix_ref, u, vg, ws_ref, bs_ref)

    y_ref[...] = x + _read_cols(mod_scr.at[0]) * _dot(mix_ref[...], wout_ref[...])


def _mix_sample_call(x, sh, sc, ga, ck, cv, lp, rope, ts):
    nt = x.shape[0]
    sb = SAMPLE_SEQS
    n = sb * ts
    tok_spec = lambda w: pl.BlockSpec((n, w), lambda i: (i, 0))
    row_spec = pl.BlockSpec((sb, D_MODEL), lambda i: (i, 0))
    cache_spec = pl.BlockSpec((sb, WINDOW, KV_WIDTH), lambda i: (i, 0, 0))
    return pl.pallas_call(
        _mix_sample_kernel,
        grid=(nt // n,),
        in_specs=[
            tok_spec(D_MODEL), row_spec, row_spec, row_spec, cache_spec, cache_spec,
            _const_spec((1, D_MODEL)),
            _const_spec((D_MODEL, IN_WIDTH)),
            _const_spec((1, ATTN_WIDTH)),
            _const_spec((1, KV_WIDTH)),
            _const_spec((ATTN_WIDTH, ATTN_WIDTH)),
            _const_spec((KV_WIDTH, KV_WIDTH)),
            _const_spec((3, n, LANES)),
            pl.BlockSpec(memory_space=pltpu.SMEM),
            _const_spec((1, GMLP_WIDTH)),
            _const_spec((1, GMLP_WIDTH)),
            _const_spec((GMLP_GROUPS, CHUNK, CHUNK)),
            _const_spec((CHUNK, GMLP_GROUPS)),
            _const_spec((D_MODEL, D_MODEL)),
        ],
        out_specs=[tok_spec(D_MODEL), tok_spec(KV_WIDTH), tok_spec(KV_WIDTH), tok_spec(GMLP_WIDTH)],
        out_shape=[
            jax.ShapeDtypeStruct((nt, D_MODEL), F32),
            jax.ShapeDtypeStruct((nt, KV_WIDTH), F32),
            jax.ShapeDtypeStruct((nt, KV_WIDTH), F32),
            jax.ShapeDtypeStruct((nt, GMLP_WIDTH), F32),
        ],
        scratch_shapes=[
            pltpu.VMEM((2, D_MODEL // LANES, n, LANES), F32),
            pltpu.VMEM((n, ATTN_WIDTH), F32),
            pltpu.VMEM((N_PAIRS, n, 2 * WINDOW), F32),
            pltpu.VMEM((N_PAIRS, n, 2 * WINDOW), F32),
            pltpu.VMEM((2 * N_PAIRS, n, LANES), F32),
            pltpu.VMEM((n, D_MODEL), BF16),
        ],
        compiler_params=pltpu.CompilerParams(
            dimension_semantics=("arbitrary",), vmem_limit_bytes=VMEM_LIMIT),
        name="mix_sample",
    )(x, sh, sc, ga, ck, cv, lp["g_attn"], lp["w_in"], lp["g_q"], lp["g_k"], lp["bd_q"],
      lp["bd_k"], rope, lp["sinks"], lp["ln_g"], lp["ln_b"], lp["ws_sample"], lp["bs_sample"],
      lp["w_out"])


def _shifted_rows(hu, before):
    rows, c = hu.shape
    hu3 = hu.reshape(rows // SUBLANES, SUBLANES, c)
    above = jnp.concatenate([before[None], hu3[:-1]], axis=0)
    sub = lax.broadcasted_iota(jnp.int32, (1, SUBLANES, c), 1)
    p1 = pltpu.roll(jnp.where(sub >= SUBLANES - 1, above, hu3), 1, axis=1)
    p2 = pltpu.roll(jnp.where(sub >= SUBLANES - 2, above, hu3), 2, axis=1)
    return p1.reshape(rows, c), p2.reshape(rows, c)


def _conv_gate(hu, prev1, prev2, cw, cb):
    conv = cb + cw[0:1] * prev2 + cw[1:2] * prev1 + cw[2:3] * hu
    cc = conv.shape[-1] // 2
    return (jax.nn.silu(conv[:, :cc]) * conv[:, cc:]).astype(BF16)


def _down_groups(nj):
    bounds = list(range(0, nj, DOWN_GROUP)) + [nj]
    return list(zip(bounds[:-1], bounds[1:]))


def _ffn_prompt_kernel(x_ref, sh_ref, sc_ref, gf_ref, gffn_ref, wgu_ref, cw_ref, cb_ref, wo_ref,
                       y_ref, cout_ref, carry, act_scr, acc_ref):
    tt = x_ref.shape[1]
    nj, _, cc2 = wgu_ref.shape
    cc = cc2 // 2
    t = pl.program_id(1)

    @pl.when(t == 0)
    def _():
        carry[...] = jnp.zeros(carry.shape, F32)

    x = x_ref[0]
    hb = _modulated_rmsnorm(x, gffn_ref[...], sc_ref[0], sh_ref[0]).astype(BF16)
    def down(j0, j1):
        contrib = _dot(act_scr[:, j0 * cc:j1 * cc], wo_ref[j0 * cc:j1 * cc, :])
        if j0 == 0:
            acc_ref[...] = contrib
        else:
            acc_ref[...] += contrib

    groups = _down_groups(nj)
    closed_at = {j1: (j0, j1) for j0, j1 in groups[:-1]}
    for j in range(nj):
        hu = _dot(hb, wgu_ref[j])
        if j in closed_at:
            down(*closed_at[j])
        prev1, prev2 = _shifted_rows(hu, carry[j])
        carry[j] = hu[tt - SUBLANES:, :]
        act_scr[:, j * cc:(j + 1) * cc] = _conv_gate(hu, prev1, prev2, cw_ref[j], cb_ref[j])
        last = hu[tt - (CONV_W - 1):, :]
        cout_ref[0, :, j * cc:(j + 1) * cc] = last[:, :cc]
        cout_ref[0, :, D_FF + j * cc:D_FF + (j + 1) * cc] = last[:, cc:]
    down(*groups[-1])
    y_ref[0] = x + gf_ref[0] * acc_ref[...]


def _ffn_weight_specs():
    nj, cc = FFN_CHUNKS, D_FF // FFN_CHUNKS
    return [
        _const_spec((1, D_MODEL)),
        _const_spec((nj, D_MODEL, 2 * cc), single_buffer=True),
        _const_spec((nj, CONV_W, 2 * cc)),
        _const_spec((nj, 1, 2 * cc)),
        _const_spec((D_FF, D_MODEL), single_buffer=True),
    ]


def _ffn_prompt_call(x, sh, sc, gf, lp):
    b, t, _ = x.shape
    tt = FFN_TILE
    nj, cc = FFN_CHUNKS, D_FF // FFN_CHUNKS
    row_spec = pl.BlockSpec((1, 1, D_MODEL), lambda i, j: (i, 0, 0))
    return pl.pallas_call(
        _ffn_prompt_kernel,
        grid=(b, t // tt),
        in_specs=[pl.BlockSpec((1, tt, D_MODEL), lambda i, j: (i, j, 0)),
                  row_spec, row_spec, row_spec] + _ffn_weight_specs(),
        out_specs=[
            pl.BlockSpec((1, tt, D_MODEL), lambda i, j: (i, j, 0)),
            pl.BlockSpec((1, CONV_W - 1, 2 * D_FF), lambda i, j: (i, 0, 0)),
        ],
        out_shape=[
            jax.ShapeDtypeStruct((b, t, D_MODEL), F32),
            jax.ShapeDtypeStruct((b, CONV_W - 1, 2 * D_FF), F32),
        ],
        scratch_shapes=[
            pltpu.VMEM((nj, SUBLANES, 2 * cc), F32),
            pltpu.VMEM((tt, D_FF), BF16),
            pltpu.VMEM((tt, D_MODEL), F32),
        ],
        compiler_params=pltpu.CompilerParams(
            dimension_semantics=("arbitrary", "arbitrary"), vmem_limit_bytes=VMEM_LIMIT),
        name="ffn_prompt",
    )(x, sh, sc, gf, lp["g_ffn"], lp["w_gu"], lp["conv_w"], lp["conv_b"], lp["w_down"])


def _ffn_sample_kernel(x_ref, sh_ref, sc_ref, gf_ref, c0_ref, c1_ref, gffn_ref, wgu_ref, cw_ref,
                       cb_ref, wo_ref, y_ref, o0_ref, o1_ref, mod_scr, h1, h2, acc_ref):
    n = x_ref.shape[0]
    sb = sh_ref.shape[0]
    ts = n // sb
    nj, _, cc2 = wgu_ref.shape
    cc = cc2 // 2
    pad = SUBLANES
    _expand_rows(mod_scr.at[0], sh_ref[...], ts)
    _expand_rows(mod_scr.at[1], sc_ref[...], ts)
    x = x_ref[...]
    hb = _modulated_rmsnorm(x, gffn_ref[...], _read_cols(mod_scr.at[1]),
                            _read_cols(mod_scr.at[0])).astype(BF16)
    _expand_rows(mod_scr.at[0], gf_ref[...], ts)
    for j in range(nj):
        offs = [half * D_FF + j * cc + c * LANES for half in range(2) for c in range(cc // LANES)]
        hu = _dot(hb, wgu_ref[j])
        for c, off in enumerate(offs):
            piece = hu[:, c * LANES:(c + 1) * LANES]
            h1[c, pad:, :] = piece
            h2[c, pad:, :] = piece
            o0_ref[:, off:off + LANES] = h2[c, pl.ds(pad + ts - 2, sb, stride=ts), :]
            o1_ref[:, off:off + LANES] = h2[c, pl.ds(pad + ts - 1, sb, stride=ts), :]
            c0 = c0_ref[:, off:off + LANES]
            c1 = c1_ref[:, off:off + LANES]
            h1[c, pl.ds(pad - 1, sb, stride=ts), :] = c1
            h2[c, pl.ds(pad - 2, sb, stride=ts), :] = c0
            h2[c, pl.ds(pad - 1, sb, stride=ts), :] = c1
        act = _conv_gate(hu, _read_cols(h1, slice(pad - 1, pad - 1 + n)),
                         _read_cols(h2, slice(pad - 2, pad - 2 + n)), cw_ref[j], cb_ref[j])
        contrib = _dot(act, wo_ref[j * cc:(j + 1) * cc, :])
        if j == 0:
            acc_ref[...] = contrib
        else:
            acc_ref[...] += contrib
    y_ref[...] = x + _read_cols(mod_scr.at[0]) * acc_ref[...]


def _ffn_sample_call(x, sh, sc, gf, c0, c1, lp, ts):
    nt = x.shape[0]
    sb = SAMPLE_SEQS
    n = sb * ts
    cc = D_FF // FFN_CHUNKS
    tok_spec = pl.BlockSpec((n, D_MODEL), lambda i: (i, 0))
    row_spec = pl.BlockSpec((sb, D_MODEL), lambda i: (i, 0))
    conv_spec = pl.BlockSpec((sb, 2 * D_FF), lambda i: (i, 0))
    return pl.pallas_call(
        _ffn_sample_kernel,
        grid=(nt // n,),
        in_specs=[tok_spec, row_spec, row_spec, row_spec, conv_spec, conv_spec]
        + _ffn_weight_specs(),
        out_specs=[tok_spec, conv_spec, conv_spec],
        out_shape=[
            jax.ShapeDtypeStruct((nt, D_MODEL), F32),
            jax.ShapeDtypeStruct(c0.shape, F32),
            jax.ShapeDtypeStruct(c1.shape, F32),
        ],
        scratch_shapes=[
            pltpu.VMEM((2, D_MODEL // LANES, n, LANES), F32),
            pltpu.VMEM((2 * cc // LANES, n + SUBLANES, LANES), F32),
            pltpu.VMEM((2 * cc // LANES, n + SUBLANES, LANES), F32),
            pltpu.VMEM((n, D_MODEL), F32),
        ],
        compiler_params=pltpu.CompilerParams(
            dimension_semantics=("arbitrary",), vmem_limit_bytes=VMEM_LIMIT),
        name="ffn_sample",
    )(x, sh, sc, gf, c0, c1, lp["g_ffn"], lp["w_gu"], lp["conv_w"], lp["conv_b"], lp["w_down"])


def _rope_tables(pos):
    half = ROPE_DIMS // 2
    inv = ROPE_THETA ** (-jnp.arange(0, ROPE_DIMS, 2, dtype=F32) / ROPE_DIMS)
    ang = pos.astype(F32)[:, None] * inv[None, :]
    cos, sin = jnp.cos(ang), jnp.sin(ang)
    n = pos.shape[0]
    rest = jnp.zeros((n, HEAD_DIM - ROPE_DIMS), F32)
    zeros = jnp.zeros((n, half), F32)
    cos_t = jnp.concatenate([cos, cos, rest + 1.0], axis=1)
    sin_dn = jnp.concatenate([-sin, zeros, rest], axis=1)
    sin_up = jnp.concatenate([zeros, sin, rest], axis=1)
    return jnp.stack([jnp.tile(z, (1, LANES // HEAD_DIM)) for z in (cos_t, sin_dn, sin_up)])


def _block_diag_ones(width):
    idx = jnp.arange(width) // HEAD_DIM
    return (idx[:, None] == idx[None, :]).astype(BF16)


def _layer_params(l, dec_seq, w_in, g_attn, g_q, g_k, sinks, ln_g, ln_b, w_s, b_s, w_out,
                  g_ffn, w_ffn_in, conv_w, conv_b, w_ffn_out):
    nj, cc = FFN_CHUNKS, D_FF // FFN_CHUNKS
    causal = jnp.tril(jnp.ones((CHUNK, CHUNK), dtype=bool))
    ws = jnp.where(causal[None], w_s[l], 0.0)
    seqs_per_chunk = CHUNK // dec_seq
    eye = jnp.eye(seqs_per_chunk, dtype=F32)
    ws_sample = jnp.einsum("ab,gts->gatbs", eye, ws[:, :dec_seq, :dec_seq]).reshape(
        GMLP_GROUPS, CHUNK, CHUNK)

    def chunked(w):
        return jnp.moveaxis(w.reshape(w.shape[:-1] + (nj, cc)), -2, 0)

    def gate_up(w):
        return jnp.concatenate([chunked(w[..., :D_FF]), chunked(w[..., D_FF:])], axis=-1)

    return {
        "g_attn": g_attn[l][None], "g_ffn": g_ffn[l][None],
        "w_in": w_in[l].astype(BF16), "w_out": w_out[l].astype(BF16),
        "g_q": jnp.tile(g_q[l], N_HEADS)[None], "g_k": jnp.tile(g_k[l], N_KV_HEADS)[None],
        "bd_q": _block_diag_ones(ATTN_WIDTH), "bd_k": _block_diag_ones(KV_WIDTH),
        "sinks": sinks[l],
        "ln_g": ln_g[l].reshape(1, GMLP_WIDTH), "ln_b": ln_b[l].reshape(1, GMLP_WIDTH),
        "ws_prompt": ws.astype(BF16), "bs_prompt": b_s[l].T,
        "ws_sample": ws_sample.astype(BF16),
        "bs_sample": jnp.tile(b_s[l][:, :dec_seq], (1, seqs_per_chunk)).T,
        "w_gu": gate_up(w_ffn_in[l]).astype(BF16),
        "conv_w": gate_up(conv_w[l]), "conv_b": gate_up(conv_b[l][None]),
        "w_down": w_ffn_out[l].astype(BF16),
    }


def kernel(x_prompt, x_sample, cache_k, cache_v, cache_conv, c_prompt, c_sample, w_ada, b_ada,
           g_attn, w_in, g_q, g_k, sinks, ln_g, ln_b, w_s, b_s, w_out, g_ffn, w_ffn_in, conv_w,
           conv_b, w_ffn_out):
    nbp, seq, _ = x_prompt.shape
    nbs, dec_seq, _ = x_sample.shape
    assert seq % PROMPT_TILE == 0 and PROMPT_TILE % CHUNK == 0 and nbs % SAMPLE_SEQS == 0
    assert seq % FFN_TILE == 0 and FFN_TILE % SUBLANES == 0
    assert dec_seq == SUBLANES and CHUNK % dec_seq == 0 and cache_k.shape[2] == WINDOW

    n_c = nbp + nbs
    c_all = jnp.concatenate([c_prompt, c_sample, jnp.zeros((-n_c % SUBLANES, D_MODEL), F32)])
    mod = _ada_call(c_all, w_ada, b_ada)
    mod_p = mod[:, :nbp].reshape(DEPTH, nbp, 6, 1, D_MODEL)
    mod_s = mod[:, nbp:n_c].reshape(DEPTH, nbs, 6, D_MODEL)

    rope_p = _rope_tables(jnp.arange(seq, dtype=jnp.int32))
    rope_s = jnp.tile(_rope_tables(PAST_LEN + jnp.arange(dec_seq, dtype=jnp.int32)),
                      (1, SAMPLE_SEQS, 1))

    xp = x_prompt
    xs = x_sample.reshape(nbs * dec_seq, D_MODEL)
    outs = [[] for _ in range(7)]
    for l in range(DEPTH):
        lp = _layer_params(l, dec_seq, w_in, g_attn, g_q, g_k, sinks, ln_g, ln_b, w_s, b_s, w_out,
                           g_ffn, w_ffn_in, conv_w, conv_b, w_ffn_out)
        mp = [mod_p[l, :, i] for i in range(6)]
        ms = [mod_s[l, :, i] for i in range(6)]

        xp, kp, vp = _mix_prompt_call(xp, mp[0], mp[1], mp[2], lp, rope_p)
        xp, cp = _ffn_prompt_call(xp, mp[3], mp[4], mp[5], lp)

        ck = cache_k[l].reshape(nbs, WINDOW, KV_WIDTH)
        cv = cache_v[l].reshape(nbs, WINDOW, KV_WIDTH)
        xs, ks, vs, gs = _mix_sample_call(xs, ms[0], ms[1], ms[2], ck, cv, lp, rope_s, dec_seq)
        xs, cs0, cs1 = _ffn_sample_call(xs, ms[3], ms[4], ms[5], cache_conv[l, :, 0],
                                        cache_conv[l, :, 1], lp, dec_seq)

        outs[0].append(kp.reshape(nbp, WINDOW, N_KV_HEADS, HEAD_DIM))
        outs[1].append(vp.reshape(nbp, WINDOW, N_KV_HEADS, HEAD_DIM))
        outs[2].append(cp)
        outs[3].append(ks.reshape(nbs, dec_seq, N_KV_HEADS, HEAD_DIM))
        outs[4].append(vs.reshape(nbs, dec_seq, N_KV_HEADS, HEAD_DIM))
        outs[5].append(gs.reshape(nbs, dec_seq, GMLP_WIDTH))
        outs[6].append(jnp.stack([cs0, cs1], axis=1))
    return (xp, xs.reshape(nbs, dec_seq, D_MODEL)) + tuple(jnp.stack(o) for o in outs)
```

```python
import functools

import jax
import jax.numpy as jnp
from jax import lax
from jax.experimental import pallas as pl
from jax.experimental.pallas import tpu as pltpu

D_MODEL = 1024
DEPTH = 2
HEAD_DIM = 64
N_HEADS = 8
N_KV_HEADS = 2
KV_WIDTH = N_KV_HEADS * HEAD_DIM
WINDOW = 128
ROPE_THETA = 500000.0
ROPE_DIMS = HEAD_DIM // 4
ATTN_WIDTH = N_HEADS * HEAD_DIM
GMLP_WIDTH = D_MODEL - ATTN_WIDTH
GMLP_GROUPS = 4
GMLP_GW = GMLP_WIDTH // GMLP_GROUPS
CHUNK = 128
IN_WIDTH = ATTN_WIDTH + 2 * KV_WIDTH + 2 * GMLP_WIDTH
D_FF = 2816
CONV_W = 3
N_MOD = 6
EPS = 1e-6
NEG = -1e30
PAST_LEN = 16384

LANES = 128
SUBLANES = 8
N_PAIRS = N_HEADS // 2
VMEM_LIMIT = 56 * 1024 * 1024

PROMPT_TILE = 512
FFN_TILE = 1024
SAMPLE_SEQS = 64
FFN_CHUNK = 256
DOWN_GROUP = 2
ADA_TILE = 1536

F32 = jnp.float32
BF16 = jnp.bfloat16


def _dot(a, b):
    return jnp.dot(a, b, preferred_element_type=F32)


def _dot_nt(a, b):
    return lax.dot_general(a, b, (((1,), (1,)), ((), ())), preferred_element_type=F32)


def _const_spec(shape):
    nd = len(shape)
    return pl.BlockSpec(shape, lambda *_: (0,) * nd)


def _layer_spec(layer, tail, single_buffer=False):
    nd = len(tail)
    kw = {"pipeline_mode": pl.Buffered(1)} if single_buffer else {}
    return pl.BlockSpec((None,) + tuple(tail), lambda *_: (layer,) + (0,) * nd, **kw)


def _skip_aliased(body, n_in, n_alias):
    if n_alias == 0:
        return body
    return lambda *refs: body(*refs[:n_in], *refs[n_in + n_alias:])


def _layer_call(body, layer, prev, *, grid, in_specs, args, out_specs, out_shapes, scratch, name):
    n_in = len(args)
    aliased = [] if prev is None else list(prev)
    return pl.pallas_call(
        _skip_aliased(body, n_in, len(aliased)),
        grid=grid,
        in_specs=list(in_specs) + [pl.BlockSpec(memory_space=pl.ANY)] * len(aliased),
        out_specs=out_specs,
        out_shape=out_shapes,
        input_output_aliases={n_in + i: 1 + i for i in range(len(aliased))},
        scratch_shapes=scratch,
        compiler_params=pltpu.CompilerParams(
            dimension_semantics=("arbitrary",) * len(grid), vmem_limit_bytes=VMEM_LIMIT),
        name=name,
    )(*args, *aliased)


def _ada_kernel(c_ref, w_ref, b_ref, o_ref):
    c = c_ref[...]
    a = (c * jax.nn.sigmoid(c)).astype(BF16)
    o_ref[0] = _dot(a, w_ref[0].astype(BF16)) + b_ref[0]


def _ada_call(c_all, w_ada, b_ada):
    n = c_all.shape[0]
    return pl.pallas_call(
        _ada_kernel,
        grid=(DEPTH, N_MOD * D_MODEL // ADA_TILE),
        in_specs=[
            pl.BlockSpec((n, D_MODEL), lambda l, j: (0, 0)),
            pl.BlockSpec((1, D_MODEL, ADA_TILE), lambda l, j: (l, 0, j)),
            pl.BlockSpec((1, 1, ADA_TILE), lambda l, j: (l, 0, j)),
        ],
        out_specs=pl.BlockSpec((1, n, ADA_TILE), lambda l, j: (l, 0, j)),
        out_shape=jax.ShapeDtypeStruct((DEPTH, n, N_MOD * D_MODEL), F32),
        compiler_params=pltpu.CompilerParams(
            dimension_semantics=("arbitrary", "arbitrary"), vmem_limit_bytes=VMEM_LIMIT),
        name="ada_mod",
    )(c_all, w_ada, b_ada.reshape(DEPTH, 1, N_MOD * D_MODEL))


def _modulated_rmsnorm(x, gain, scale, shift):
    r = lax.rsqrt(jnp.mean(x * x, axis=-1, keepdims=True) + EPS)
    return (x * r) * (gain * (1.0 + scale)) + shift


def _head_rmsnorm(z, ones_bd, gain):
    ssq = _dot((z * z).astype(BF16), ones_bd)
    return z * lax.rsqrt(ssq * (1.0 / HEAD_DIM) + EPS) * gain


def _rope(z, cos_t, sin_dn, sin_up):
    half = ROPE_DIMS // 2
    cols = []
    for p in range(z.shape[-1] // LANES):
        zp = z[:, p * LANES:(p + 1) * LANES]
        cols.append(zp * cos_t
                    + pltpu.roll(zp, LANES - half, axis=1) * sin_dn
                    + pltpu.roll(zp, half, axis=1) * sin_up)
    return cols[0] if len(cols) == 1 else jnp.concatenate(cols, axis=1)


def _split_heads(z):
    lane = lax.broadcasted_iota(jnp.int32, z.shape, 1)
    lo = lane < HEAD_DIM
    zs = pltpu.roll(z, HEAD_DIM, axis=1)
    zero = jnp.zeros_like(z)
    return (jnp.where(lo, z, zero), jnp.where(lo, zero, zs),
            jnp.where(lo, zs, zero), jnp.where(lo, zero, z))


def _gelu_layernorm(zv, ln_g, ln_b):
    gv = jax.nn.gelu(zv)
    cols = []
    for g in range(GMLP_GROUPS):
        xg = gv[:, g * GMLP_GW:(g + 1) * GMLP_GW]
        mu = jnp.mean(xg, axis=-1, keepdims=True)
        xc = xg - mu
        var = jnp.mean(xc * xc, axis=-1, keepdims=True)
        cols.append(xc * lax.rsqrt(var + EPS))
    return jnp.concatenate(cols, axis=1) * ln_g + ln_b


def _spatial_gate_into(mix_ref, u, vg, ws_ref, bs_ref):
    vgb = vg.astype(BF16)
    for g in range(GMLP_GROUPS):
        w = ws_ref[g]
        bias = bs_ref[:, g:g + 1]
        cs = slice(g * GMLP_GW, (g + 1) * GMLP_GW)
        for c in range(u.shape[0] // CHUNK):
            rs = slice(c * CHUNK, (c + 1) * CHUNK)
            z = _dot(w, vgb[rs, cs]) + bias
            mix_ref[rs, ATTN_WIDTH + g * GMLP_GW:ATTN_WIDTH + (g + 1) * GMLP_GW] = (
                u[rs, cs] * z).astype(BF16)


def _lane_pair(a, b, shape):
    lane = lax.broadcasted_iota(jnp.int32, shape, 1)
    return jnp.where(lane < HEAD_DIM, a, b)


def _expand_rows(dst_ref, src, reps):
    for c in range(dst_ref.shape[0]):
        piece = src[:, c * LANES:(c + 1) * LANES]
        for t in range(reps):
            dst_ref[c, pl.ds(t, src.shape[0], stride=reps), :] = piece


def _read_cols(ref, rows=slice(None)):
    return jnp.concatenate([ref[c, rows, :] for c in range(ref.shape[0])], axis=1)


def _mix_prompt_kernel(x_ref, mod_ref, gattn_ref, win_ref, gq_ref, gk_ref, bdq_ref, bdk_ref,
                       rope_ref, sinks_ref, lng_ref, lnb_ref, ws_ref, bs_ref, wout_ref,
                       y_ref, kout_ref, vout_ref, kprep, vprep, mix_ref, *, layer):
    tt = x_ref.shape[1]
    t = pl.program_id(1)

    @pl.when(t == 0)
    def _():
        kprep[:, 0:WINDOW, :] = jnp.zeros((4, WINDOW, LANES), BF16)
        vprep[:, 0:WINDOW, :] = jnp.zeros((4, WINDOW, LANES), BF16)

    x = x_ref[0]
    h = _modulated_rmsnorm(x, gattn_ref[...], mod_ref[1:2], mod_ref[0:1])
    proj = _dot(h.astype(BF16), win_ref[...])

    cos_t, sin_dn, sin_up = rope_ref[0], rope_ref[1], rope_ref[2]
    q = _rope(_head_rmsnorm(proj[:, :ATTN_WIDTH], bdq_ref[...], gq_ref[...]), cos_t, sin_dn, sin_up)
    qb = (q * (HEAD_DIM ** -0.5)).astype(BF16)
    k = _rope(_head_rmsnorm(proj[:, ATTN_WIDTH:ATTN_WIDTH + KV_WIDTH], bdk_ref[...], gk_ref[...]),
              cos_t, sin_dn, sin_up)
    v = proj[:, ATTN_WIDTH + KV_WIDTH:ATTN_WIDTH + 2 * KV_WIDTH]
    kout_ref[0] = k[tt - WINDOW:, :]
    vout_ref[0] = v[tt - WINDOW:, :]
    for i, (kz, vz) in enumerate(zip(_split_heads(k), _split_heads(v))):
        kprep[i, WINDOW:, :] = kz.astype(BF16)
        vprep[i, WINDOW:, :] = vz.astype(BF16)

    row = lax.broadcasted_iota(jnp.int32, (WINDOW, 2 * WINDOW), 0)
    col = lax.broadcasted_iota(jnp.int32, (WINDOW, 2 * WINDOW), 1)
    band = (col - row >= 1) & (col - row <= WINDOW)
    band_first = band & (col + jnp.minimum(t, 1) * WINDOW >= WINDOW)

    for i in range(tt // WINDOW):
        mask = band_first if i == 0 else band
        rows = slice(i * WINDOW, (i + 1) * WINDOW)
        keys = slice(i * WINDOW, (i + 2) * WINDOW)
        for p in range(N_PAIRS):
            kv = p // 2
            kblk = jnp.concatenate([kprep[2 * kv, keys, :], kprep[2 * kv + 1, keys, :]], axis=0)
            vblk = jnp.concatenate([vprep[2 * kv, keys, :], vprep[2 * kv + 1, keys, :]], axis=0)
            s = _dot_nt(qb[rows, p * LANES:(p + 1) * LANES], kblk)
            es, invs = [], []
            for hh in range(2):
                sink = sinks_ref[layer, 2 * p + hh]
                sh_ = jnp.where(mask, s[:, hh * 2 * WINDOW:(hh + 1) * 2 * WINDOW], NEG)
                m = jnp.maximum(jnp.max(sh_, axis=-1, keepdims=True), sink)
                e = jnp.exp(sh_ - m)
                es.append(e)
                invs.append(1.0 / (jnp.sum(e, axis=-1, keepdims=True) + jnp.exp(sink - m)))
            o = _dot(jnp.concatenate(es, axis=1).astype(BF16), vblk)
            o = o * _lane_pair(invs[0], invs[1], o.shape)
            mix_ref[rows, p * LANES:(p + 1) * LANES] = o.astype(BF16)

    kprep[:, 0:WINDOW, :] = kprep[:, tt:tt + WINDOW, :]
    vprep[:, 0:WINDOW, :] = vprep[:, tt:tt + WINDOW, :]

    u = jax.nn.gelu(proj[:, ATTN_WIDTH + 2 * KV_WIDTH:ATTN_WIDTH + 2 * KV_WIDTH + GMLP_WIDTH])
    vg = _gelu_layernorm(proj[:, IN_WIDTH - GMLP_WIDTH:], lng_ref[...], lnb_ref[...])
    _spatial_gate_into(mix_ref, u, vg, ws_ref, bs_ref)

    y_ref[0] = x + mod_ref[2:3] * _dot(mix_ref[...], wout_ref[...])


def _mix_weight_specs(layer, gate_kind):
    return [
        _layer_spec(layer, (1, D_MODEL)),
        _layer_spec(layer, (D_MODEL, IN_WIDTH)),
        _layer_spec(layer, (1, ATTN_WIDTH)),
        _layer_spec(layer, (1, KV_WIDTH)),
        _const_spec((ATTN_WIDTH, ATTN_WIDTH)),
        _const_spec((KV_WIDTH, KV_WIDTH)),
    ], [
        pl.BlockSpec(memory_space=pltpu.SMEM),
        _layer_spec(layer, (1, GMLP_WIDTH)),
        _layer_spec(layer, (1, GMLP_WIDTH)),
        _layer_spec(layer, (GMLP_GROUPS, CHUNK, CHUNK)),
        _layer_spec(layer, (CHUNK, GMLP_GROUPS)),
        _layer_spec(layer, (D_MODEL, D_MODEL)),
    ]


def _mix_weight_args(pp, kind):
    return ([pp["g_attn"], pp["w_in"], pp["g_q"], pp["g_k"], pp["bd_q"], pp["bd_k"]],
            [pp["sinks"], pp["ln_g"], pp["ln_b"], pp["ws_" + kind], pp["bs_" + kind], pp["w_out"]])


def _mix_prompt_call(layer, x, mod4, n_skip, pp, rope, prev):
    b, t, _ = x.shape
    tt = PROMPT_TILE
    w_specs_a, w_specs_b = _mix_weight_specs(layer, "prompt")
    w_args_a, w_args_b = _mix_weight_args(pp, "prompt")
    kv_spec = pl.BlockSpec((None, 1, WINDOW, KV_WIDTH), lambda i, j: (layer, i, 0, 0))
    kv_shape = jax.ShapeDtypeStruct((DEPTH, b, WINDOW, KV_WIDTH), F32)
    return _layer_call(
        functools.partial(_mix_prompt_kernel, layer=layer), layer, prev,
        grid=(b, t // tt),
        in_specs=[pl.BlockSpec((1, tt, D_MODEL), lambda i, j: (i, j, 0)),
                  pl.BlockSpec((None, None, N_MOD, D_MODEL), lambda i, j: (layer, n_skip + i, 0, 0))]
        + w_specs_a + [pl.BlockSpec((3, tt, LANES), lambda i, j: (0, j, 0))] + w_specs_b,
        args=[x, mod4] + w_args_a + [rope] + w_args_b,
        out_specs=[pl.BlockSpec((1, tt, D_MODEL), lambda i, j: (i, j, 0)), kv_spec, kv_spec],
        out_shapes=[jax.ShapeDtypeStruct((b, t, D_MODEL), F32), kv_shape, kv_shape],
        scratch=[
            pltpu.VMEM((4, tt + WINDOW, LANES), BF16),
            pltpu.VMEM((4, tt + WINDOW, LANES), BF16),
            pltpu.VMEM((tt, D_MODEL), BF16),
        ],
        name="mix_prompt")


def _mix_sample_kernel(x_ref, sh_ref, sc_ref, ga_ref, ck_ref, cv_ref, gattn_ref, win_ref, gq_ref,
                       gk_ref, bdq_ref, bdk_ref, rope_ref, sinks_ref, lng_ref, lnb_ref, ws_ref,
                       bs_ref, wout_ref, y_ref, kout_ref, vout_ref, vgout_ref,
                       mod_scr, q_scr, s_scr, e_scr, o_scr, mix_ref, *, layer):
    n = x_ref.shape[0]
    sb = sh_ref.shape[0]
    ts = n // sb
    _expand_rows(mod_scr.at[0], sh_ref[...], ts)
    _expand_rows(mod_scr.at[1], sc_ref[...], ts)
    x = x_ref[...]
    h = _modulated_rmsnorm(x, gattn_ref[...], _read_cols(mod_scr.at[1]), _read_cols(mod_scr.at[0]))
    proj = _dot(h.astype(BF16), win_ref[...])
    _expand_rows(mod_scr.at[0], ga_ref[...], ts)

    cos_t, sin_dn, sin_up = rope_ref[0], rope_ref[1], rope_ref[2]
    q = _rope(_head_rmsnorm(proj[:, :ATTN_WIDTH], bdq_ref[...], gq_ref[...]), cos_t, sin_dn, sin_up)
    q = q * (HEAD_DIM ** -0.5)
    q_scr[...] = q
    qb = q.astype(BF16)
    k = _rope(_head_rmsnorm(proj[:, ATTN_WIDTH:ATTN_WIDTH + KV_WIDTH], bdk_ref[...], gk_ref[...]),
              cos_t, sin_dn, sin_up)
    v = proj[:, ATTN_WIDTH + KV_WIDTH:ATTN_WIDTH + 2 * KV_WIDTH]
    kout_ref[...] = k
    vout_ref[...] = v

    def cache_scores(b, carry):
        rows = pl.ds(pl.multiple_of(b * ts, ts), ts)
        kparts = _split_heads(ck_ref[b])
        for kv in range(N_KV_HEADS):
            kblk = jnp.concatenate([kparts[2 * kv], kparts[2 * kv + 1]], axis=0).astype(BF16)
            q16 = jnp.concatenate(
                [q_scr[rows, (2 * kv) * LANES:(2 * kv + 1) * LANES],
                 q_scr[rows, (2 * kv + 1) * LANES:(2 * kv + 2) * LANES]], axis=0).astype(BF16)
            s = _dot_nt(q16, kblk)
            s_scr[2 * kv, rows, :] = s[:ts]
            s_scr[2 * kv + 1, rows, :] = s[ts:]
        return carry

    lax.fori_loop(0, sb, cache_scores, 0)

    knew = [z.astype(BF16) for z in _split_heads(k)]
    vnew = [z.astype(BF16) for z in _split_heads(v)]
    grp = CHUNK
    rown = lax.broadcasted_iota(jnp.int32, (grp, grp), 0)
    coln = lax.broadcasted_iota(jnp.int32, (grp, grp), 1)
    seq_bits = ts.bit_length() - 1
    mask_new = ((rown >> seq_bits) == (coln >> seq_bits)) & (coln <= rown)
    mask_cache = (lax.broadcasted_iota(jnp.int32, (grp, WINDOW), 1)
                  > (lax.broadcasted_iota(jnp.int32, (grp, WINDOW), 0) & (ts - 1)))

    for p in range(N_PAIRS):
        kv = p // 2
        for g0 in range(0, n, grp):
            rows = slice(g0, g0 + grp)
            kblk = jnp.concatenate([knew[2 * kv][rows], knew[2 * kv + 1][rows]], axis=0)
            vblk = jnp.concatenate([vnew[2 * kv][rows], vnew[2 * kv + 1][rows]], axis=0)
            s_new = _dot_nt(qb[rows, p * LANES:(p + 1) * LANES], kblk)
            e_new, invs = [], []
            for hh in range(2):
                sink = sinks_ref[layer, 2 * p + hh]
                sn = jnp.where(mask_new, s_new[:, hh * grp:(hh + 1) * grp], NEG)
                sc_ = jnp.where(mask_cache, s_scr[p, rows, hh * WINDOW:(hh + 1) * WINDOW], NEG)
                m = jnp.maximum(jnp.maximum(jnp.max(sn, axis=-1, keepdims=True),
                                            jnp.max(sc_, axis=-1, keepdims=True)), sink)
                en = jnp.exp(sn - m)
                ec = jnp.exp(sc_ - m)
                den = (jnp.sum(en, axis=-1, keepdims=True) + jnp.sum(ec, axis=-1, keepdims=True)
                       + jnp.exp(sink - m))
                e_new.append(en)
                invs.append(1.0 / den)
                e_scr[p, rows, hh * WINDOW:(hh + 1) * WINDOW] = ec
            o_scr[p, rows, :] = _dot(jnp.concatenate(e_new, axis=1).astype(BF16), vblk)
            o_scr[N_PAIRS + p, rows, :] = _lane_pair(invs[0], invs[1], (grp, LANES))

    def cache_pv(b, carry):
        rows = pl.ds(pl.multiple_of(b * ts, ts), ts)
        vparts = _split_heads(cv_ref[b])
        for kv in range(N_KV_HEADS):
            vblk = jnp.concatenate([vparts[2 * kv], vparts[2 * kv + 1]], axis=0).astype(BF16)
            e16 = jnp.concatenate([e_scr[2 * kv, rows, :], e_scr[2 * kv + 1, rows, :]],
                                  axis=0).astype(BF16)
            o = _dot(e16, vblk)
            o_scr[2 * kv, rows, :] += o[:ts]
            o_scr[2 * kv + 1, rows, :] += o[ts:]
        return carry

    lax.fori_loop(0, sb, cache_pv, 0)

    for p in range(N_PAIRS):
        mix_ref[:, p * LANES:(p + 1) * LANES] = (o_scr[p] * o_scr[N_PAIRS + p]).astype(BF16)

    u = jax.nn.gelu(proj[:, ATTN_WIDTH + 2 * KV_WIDTH:ATTN_WIDTH + 2 * KV_WIDTH + GMLP_WIDTH])
    vg = _gelu_layernorm(proj[:, IN_WIDTH - GMLP_WIDTH:], lng_ref[...], lnb_ref[...])
    vgout_ref[...] = vg
    _spatial_gate_into(mix_ref, u, vg, ws_ref, bs_ref)

    y_ref[...] = x + _read_cols(mod_scr.at[0]) * _dot(mix_ref[...], wout_ref[...])


def _sample_mod_specs(layer, sb, first):
    return [pl.BlockSpec((None, sb, D_MODEL), functools.partial(lambda i, w: (layer, i, w), w=first + m))
            for m in range(3)]


def _mix_sample_call(layer, x, mod3, cache_k, cache_v, pp, rope, ts, prev):
    nt = x.shape[0]
    sb = SAMPLE_SEQS
    n = sb * ts
    w_specs_a, w_specs_b = _mix_weight_specs(layer, "sample")
    w_args_a, w_args_b = _mix_weight_args(pp, "sample")
    tok_spec = lambda w: pl.BlockSpec((None, n, w), lambda i: (layer, i, 0))
    tok_shape = lambda w: jax.ShapeDtypeStruct((DEPTH, nt, w), F32)
    cache_spec = pl.BlockSpec((None, sb, WINDOW, KV_WIDTH), lambda i: (layer, i, 0, 0))
    return _layer_call(
        functools.partial(_mix_sample_kernel, layer=layer), layer, prev,
        grid=(nt // n,),
        in_specs=[pl.BlockSpec((n, D_MODEL), lambda i: (i, 0))] + _sample_mod_specs(layer, sb, 0)
        + [cache_spec, cache_spec] + w_specs_a + [_const_spec((3, n, LANES))] + w_specs_b,
        args=[x, mod3, mod3, mod3, cache_k, cache_v] + w_args_a + [rope] + w_args_b,
        out_specs=[pl.BlockSpec((n, D_MODEL), lambda i: (i, 0)),
                   tok_spec(KV_WIDTH), tok_spec(KV_WIDTH), tok_spec(GMLP_WIDTH)],
        out_shapes=[jax.ShapeDtypeStruct((nt, D_MODEL), F32),
                    tok_shape(KV_WIDTH), tok_shape(KV_WIDTH), tok_shape(GMLP_WIDTH)],
        scratch=[
            pltpu.VMEM((2, D_MODEL // LANES, n, LANES), F32),
            pltpu.VMEM((n, ATTN_WIDTH), F32),
            pltpu.VMEM((N_PAIRS, n, 2 * WINDOW), F32),
            pltpu.VMEM((N_PAIRS, n, 2 * WINDOW), F32),
            pltpu.VMEM((2 * N_PAIRS, n, LANES), F32),
            pltpu.VMEM((n, D_MODEL), BF16),
        ],
        name="mix_sample")


def _shifted_rows(hu, before):
    rows, c = hu.shape
    hu3 = hu.reshape(rows // SUBLANES, SUBLANES, c)
    above = jnp.concatenate([before[None], hu3[:-1]], axis=0)
    sub = lax.broadcasted_iota(jnp.int32, (1, SUBLANES, c), 1)
    p1 = pltpu.roll(jnp.where(sub >= SUBLANES - 1, above, hu3), 1, axis=1)
    p2 = pltpu.roll(jnp.where(sub >= SUBLANES - 2, above, hu3), 2, axis=1)
    return p1.reshape(rows, c), p2.reshape(rows, c)


def _conv(hu, prev1, prev2, cw, cb):
    return cb + cw[0:1] * prev2 + cw[1:2] * prev1 + cw[2:3] * hu


def _ffn_chunks():
    return [(slice(j, j + FFN_CHUNK), slice(D_FF + j, D_FF + j + FFN_CHUNK))
            for j in range(0, D_FF, FFN_CHUNK)]


def _down_plan(n_chunks):
    bounds = list(range(0, n_chunks, DOWN_GROUP)) + [n_chunks]
    groups = [(a * FFN_CHUNK, b * FFN_CHUNK) for a, b in zip(bounds[:-1], bounds[1:])]
    return {b // FFN_CHUNK: (a, b) for a, b in groups[:-1]}, groups[-1]


def _down_into(acc_ref, act_scr, wdn_ref, a, b):
    contrib = _dot(act_scr[:, a:b], wdn_ref[a:b, :])
    if a == 0:
        acc_ref[...] = contrib
    else:
        acc_ref[...] += contrib


def _ffn_prompt_kernel(x_ref, mod_ref, gffn_ref, wup_ref, cw_ref, cb_ref, wdn_ref,
                       y_ref, cout_ref, carry, act_scr, acc_ref):
    tt = x_ref.shape[1]
    t = pl.program_id(1)

    @pl.when(t == 0)
    def _():
        carry[...] = jnp.zeros(carry.shape, F32)

    x = x_ref[0]
    hb = _modulated_rmsnorm(x, gffn_ref[...], mod_ref[4:5], mod_ref[3:4]).astype(BF16)
    chunks = _ffn_chunks()
    after, tail = _down_plan(len(chunks))
    for j, halves in enumerate(chunks):
        hus = [_dot(hb, wup_ref[:, cols]) for cols in halves]
        if j in after:
            _down_into(acc_ref, act_scr, wdn_ref, *after[j])
        convs = []
        for hu, cols in zip(hus, halves):
            prev1, prev2 = _shifted_rows(hu, carry[:, cols])
            carry[:, cols] = hu[tt - SUBLANES:, :]
            cout_ref[0, :, cols] = hu[tt - (CONV_W - 1):, :]
            convs.append(_conv(hu, prev1, prev2, cw_ref[:, cols], cb_ref[:, cols]))
        act_scr[:, j * FFN_CHUNK:(j + 1) * FFN_CHUNK] = (
            jax.nn.silu(convs[0]) * convs[1]).astype(BF16)
    _down_into(acc_ref, act_scr, wdn_ref, *tail)
    y_ref[0] = x + mod_ref[5:6] * acc_ref[...]


def _ffn_weight_specs(layer):
    return [
        _layer_spec(layer, (1, D_MODEL)),
        _layer_spec(layer, (D_MODEL, 2 * D_FF), single_buffer=True),
        _layer_spec(layer, (CONV_W, 2 * D_FF)),
        _layer_spec(layer, (1, 2 * D_FF)),
        _layer_spec(layer, (D_FF, D_MODEL), single_buffer=True),
    ]


def _ffn_weight_args(pp):
    return [pp["g_ffn"], pp["w_up"], pp["conv_w"], pp["conv_b"], pp["w_down"]]


def _ffn_prompt_call(layer, x, mod4, n_skip, pp, prev):
    b, t, _ = x.shape
    tt = FFN_TILE
    return _layer_call(
        _ffn_prompt_kernel, layer, prev,
        grid=(b, t // tt),
        in_specs=[pl.BlockSpec((1, tt, D_MODEL), lambda i, j: (i, j, 0)),
                  pl.BlockSpec((None, None, N_MOD, D_MODEL), lambda i, j: (layer, n_skip + i, 0, 0))]
        + _ffn_weight_specs(layer),
        args=[x, mod4] + _ffn_weight_args(pp),
        out_specs=[pl.BlockSpec((1, tt, D_MODEL), lambda i, j: (i, j, 0)),
                   pl.BlockSpec((None, 1, CONV_W - 1, 2 * D_FF), lambda i, j: (layer, i, 0, 0))],
        out_shapes=[jax.ShapeDtypeStruct((b, t, D_MODEL), F32),
                    jax.ShapeDtypeStruct((DEPTH, b, CONV_W - 1, 2 * D_FF), F32)],
        scratch=[
            pltpu.VMEM((SUBLANES, 2 * D_FF), F32),
            pltpu.VMEM((tt, D_FF), BF16),
            pltpu.VMEM((tt, D_MODEL), F32),
        ],
        name="ffn_prompt")


def _ffn_sample_kernel(x_ref, sh_ref, sc_ref, gf_ref, c0_ref, c1_ref, gffn_ref, wup_ref, cw_ref,
                       cb_ref, wdn_ref, y_ref, cout_ref, mod_scr, h1, h2, act_scr, acc_ref):
    n = x_ref.shape[0]
    sb = sh_ref.shape[0]
    ts = n // sb
    pad = SUBLANES
    _expand_rows(mod_scr.at[0], sh_ref[...], ts)
    _expand_rows(mod_scr.at[1], sc_ref[...], ts)
    x = x_ref[...]
    hb = _modulated_rmsnorm(x, gffn_ref[...], _read_cols(mod_scr.at[1]),
                            _read_cols(mod_scr.at[0])).astype(BF16)
    _expand_rows(mod_scr.at[0], gf_ref[...], ts)
    chunks = _ffn_chunks()
    after, tail = _down_plan(len(chunks))
    for j, halves in enumerate(chunks):
        hus = [_dot(hb, wup_ref[:, cols]) for cols in halves]
        if j in after:
            _down_into(acc_ref, act_scr, wdn_ref, *after[j])
        convs = []
        for hu, cols in zip(hus, halves):
            for c in range(FFN_CHUNK // LANES):
                off = cols.start + c * LANES
                piece = hu[:, c * LANES:(c + 1) * LANES]
                h1[c, pad:, :] = piece
                h2[c, pad:, :] = piece
                cout_ref[:, off:off + LANES] = h2[c, pl.ds(pad + ts - 2, sb, stride=ts), :]
                cout_ref[:, 2 * D_FF + off:2 * D_FF + off + LANES] = (
                    h2[c, pl.ds(pad + ts - 1, sb, stride=ts), :])
                c0 = c0_ref[:, off:off + LANES]
                c1 = c1_ref[:, off:off + LANES]
                h1[c, pl.ds(pad - 1, sb, stride=ts), :] = c1
                h2[c, pl.ds(pad - 2, sb, stride=ts), :] = c0
                h2[c, pl.ds(pad - 1, sb, stride=ts), :] = c1
            convs.append(_conv(hu, _read_cols(h1, slice(pad - 1, pad - 1 + n)),
                               _read_cols(h2, slice(pad - 2, pad - 2 + n)),
                               cw_ref[:, cols], cb_ref[:, cols]))
        act_scr[:, j * FFN_CHUNK:(j + 1) * FFN_CHUNK] = (
            jax.nn.silu(convs[0]) * convs[1]).astype(BF16)
    _down_into(acc_ref, act_scr, wdn_ref, *tail)
    y_ref[...] = x + _read_cols(mod_scr.at[0]) * acc_ref[...]


def _ffn_sample_call(layer, x, mod3, cconv, pp, ts, prev):
    nt = x.shape[0]
    sb = SAMPLE_SEQS
    n = sb * ts
    tok_spec = pl.BlockSpec((n, D_MODEL), lambda i: (i, 0))
    hist_spec = lambda r: pl.BlockSpec((None, sb, 2 * D_FF), lambda i: (layer, i, r))
    return _layer_call(
        _ffn_sample_kernel, layer, prev,
        grid=(nt // n,),
        in_specs=[tok_spec] + _sample_mod_specs(layer, sb, 3) + [hist_spec(0), hist_spec(1)]
        + _ffn_weight_specs(layer),
        args=[x, mod3, mod3, mod3, cconv, cconv] + _ffn_weight_args(pp),
        out_specs=[tok_spec,
                   pl.BlockSpec((None, sb, (CONV_W - 1) * 2 * D_FF), lambda i: (layer, i, 0))],
        out_shapes=[jax.ShapeDtypeStruct((nt, D_MODEL), F32),
                    jax.ShapeDtypeStruct(cconv.shape, F32)],
        scratch=[
            pltpu.VMEM((2, D_MODEL // LANES, n, LANES), F32),
            pltpu.VMEM((FFN_CHUNK // LANES, n + SUBLANES, LANES), F32),
            pltpu.VMEM((FFN_CHUNK // LANES, n + SUBLANES, LANES), F32),
            pltpu.VMEM((n, D_FF), BF16),
            pltpu.VMEM((n, D_MODEL), F32),
        ],
        name="ffn_sample")


def _rope_tables(pos):
    half = ROPE_DIMS // 2
    inv = ROPE_THETA ** (-jnp.arange(0, ROPE_DIMS, 2, dtype=F32) / ROPE_DIMS)
    ang = pos.astype(F32)[:, None] * inv[None, :]
    cos, sin = jnp.cos(ang), jnp.sin(ang)
    n = pos.shape[0]
    rest = jnp.zeros((n, HEAD_DIM - ROPE_DIMS), F32)
    zeros = jnp.zeros((n, half), F32)
    cos_t = jnp.concatenate([cos, cos, rest + 1.0], axis=1)
    sin_dn = jnp.concatenate([-sin, zeros, rest], axis=1)
    sin_up = jnp.concatenate([zeros, sin, rest], axis=1)
    return jnp.stack([jnp.tile(z, (1, LANES // HEAD_DIM)) for z in (cos_t, sin_dn, sin_up)])


def _block_diag_ones(width):
    idx = jnp.arange(width) // HEAD_DIM
    return (idx[:, None] == idx[None, :]).astype(BF16)


def _prepare_params(dec_seq, g_attn, w_in, g_q, g_k, sinks, ln_g, ln_b, w_s, b_s, w_out,
                    g_ffn, w_ffn_in, conv_w, conv_b, w_ffn_out):
    causal = jnp.tril(jnp.ones((CHUNK, CHUNK), dtype=bool))
    ws = jnp.where(causal, w_s, 0.0)
    seqs_per_chunk = CHUNK // dec_seq
    eye = jnp.eye(seqs_per_chunk, dtype=F32)
    ws_sample = jnp.einsum("ab,lgts->lgatbs", eye, ws[:, :, :dec_seq, :dec_seq]).reshape(w_s.shape)
    return {
        "g_attn": g_attn[:, None, :], "g_ffn": g_ffn[:, None, :],
        "w_in": w_in.astype(BF16), "w_out": w_out.astype(BF16),
        "g_q": jnp.tile(g_q, (1, N_HEADS))[:, None, :],
        "g_k": jnp.tile(g_k, (1, N_KV_HEADS))[:, None, :],
        "bd_q": _block_diag_ones(ATTN_WIDTH), "bd_k": _block_diag_ones(KV_WIDTH),
        "sinks": sinks,
        "ln_g": ln_g.reshape(DEPTH, 1, GMLP_WIDTH), "ln_b": ln_b.reshape(DEPTH, 1, GMLP_WIDTH),
        "ws_prompt": ws.astype(BF16), "bs_prompt": jnp.swapaxes(b_s, 1, 2),
        "ws_sample": ws_sample.astype(BF16),
        "bs_sample": jnp.swapaxes(jnp.tile(b_s[:, :, :dec_seq], (1, 1, seqs_per_chunk)), 1, 2),
        "w_up": w_ffn_in.astype(BF16), "w_down": w_ffn_out.astype(BF16),
        "conv_w": conv_w, "conv_b": conv_b[:, None, :],
    }


def kernel(x_prompt, x_sample, cache_k, cache_v, cache_conv, c_prompt, c_sample, w_ada, b_ada,
           g_attn, w_in, g_q, g_k, sinks, ln_g, ln_b, w_s, b_s, w_out, g_ffn, w_ffn_in, conv_w,
           conv_b, w_ffn_out):
    nbp, seq, _ = x_prompt.shape
    nbs, dec_seq, _ = x_sample.shape
    assert seq % PROMPT_TILE == 0 and PROMPT_TILE % CHUNK == 0 and nbs % SAMPLE_SEQS == 0
    assert seq % FFN_TILE == 0 and FFN_TILE % SUBLANES == 0 and D_FF % FFN_CHUNK == 0
    assert dec_seq == SUBLANES and CHUNK % dec_seq == 0 and cache_k.shape[2] == WINDOW
    assert nbs % SUBLANES == 0

    n_c = nbs + nbp
    c_all = jnp.concatenate([c_sample, c_prompt, jnp.zeros((-n_c % SUBLANES, D_MODEL), F32)])
    mod3 = _ada_call(c_all, w_ada, b_ada)
    mod4 = mod3.reshape(DEPTH, c_all.shape[0], N_MOD, D_MODEL)

    pp = _prepare_params(dec_seq, g_attn, w_in, g_q, g_k, sinks, ln_g, ln_b, w_s, b_s, w_out,
                         g_ffn, w_ffn_in, conv_w, conv_b, w_ffn_out)
    rope_p = _rope_tables(jnp.arange(seq, dtype=jnp.int32))
    rope_s = jnp.tile(_rope_tables(PAST_LEN + jnp.arange(dec_seq, dtype=jnp.int32)),
                      (1, SAMPLE_SEQS, 1))
    ck = cache_k.reshape(DEPTH, nbs, WINDOW, KV_WIDTH)
    cv = cache_v.reshape(DEPTH, nbs, WINDOW, KV_WIDTH)
    cconv = cache_conv.reshape(DEPTH, nbs, (CONV_W - 1) * 2 * D_FF)

    xp = x_prompt
    xs = x_sample.reshape(nbs * dec_seq, D_MODEL)
    mix_p = ffn_p = mix_s = ffn_s = None
    for l in range(DEPTH):
        xp, *mix_p = _mix_prompt_call(l, xp, mod4, nbs, pp, rope_p, mix_p)
        xp, *ffn_p = _ffn_prompt_call(l, xp, mod4, nbs, pp, ffn_p)
        xs, *mix_s = _mix_sample_call(l, xs, mod3, ck, cv, pp, rope_s, dec_seq, mix_s)
        xs, *ffn_s = _ffn_sample_call(l, xs, mod3, cconv, pp, dec_seq, ffn_s)

    kv_p = (DEPTH, nbp, WINDOW, N_KV_HEADS, HEAD_DIM)
    kv_s = (DEPTH, nbs, dec_seq, N_KV_HEADS, HEAD_DIM)
    return (xp, xs.reshape(nbs, dec_seq, D_MODEL),
            mix_p[0].reshape(kv_p), mix_p[1].reshape(kv_p), ffn_p[0],
            mix_s[0].reshape(kv_s), mix_s[1].reshape(kv_s),
            mix_s[2].reshape(DEPTH, nbs, dec_seq, GMLP_WIDTH),
            ffn_s[0].reshape(DEPTH, nbs, CONV_W - 1, 2 * D_FF))
```

```python
import functools

import jax
import jax.numpy as jnp
from jax import lax
from jax.experimental import pallas as pl
from jax.experimental.pallas import tpu as pltpu

D_MODEL = 1024
DEPTH = 2
HEAD_DIM = 64
N_HEADS = 8
N_KV_HEADS = 2
KV_WIDTH = N_KV_HEADS * HEAD_DIM
WINDOW = 128
ROPE_THETA = 500000.0
ROPE_DIMS = HEAD_DIM // 4
ATTN_WIDTH = N_HEADS * HEAD_DIM
GMLP_WIDTH = D_MODEL - ATTN_WIDTH
GMLP_GROUPS = 4
GMLP_GW = GMLP_WIDTH // GMLP_GROUPS
CHUNK = 128
IN_WIDTH = ATTN_WIDTH + 2 * KV_WIDTH + 2 * GMLP_WIDTH
D_FF = 2816
CONV_W = 3
N_MOD = 6
EPS = 1e-6
NEG = -1e30
PAST_LEN = 16384

LANES = 128
SUBLANES = 8
N_PAIRS = N_HEADS // 2
VMEM_LIMIT = 56 * 1024 * 1024

PROMPT_TILE = 512
OUT_COLS = 256
MIX_INTERLEAVE = 10
FFN_TILE = 1024
SAMPLE_SEQS = 64
FFN_CHUNK = 256
DOWN_GROUP = 2
ADA_TILE = 1536

F32 = jnp.float32
BF16 = jnp.bfloat16


def _dot(a, b):
    return jnp.dot(a, b, preferred_element_type=F32)


def _dot_nt(a, b):
    return lax.dot_general(a, b, (((1,), (1,)), ((), ())), preferred_element_type=F32)


def _const_spec(shape):
    nd = len(shape)
    return pl.BlockSpec(shape, lambda *_: (0,) * nd)


def _layer_spec(layer, tail, single_buffer=False):
    nd = len(tail)
    kw = {"pipeline_mode": pl.Buffered(1)} if single_buffer else {}
    return pl.BlockSpec((None,) + tuple(tail), lambda *_: (layer,) + (0,) * nd, **kw)


def _skip_aliased(body, n_in, n_alias):
    if n_alias == 0:
        return body
    return lambda *refs: body(*refs[:n_in], *refs[n_in + n_alias:])


def _layer_call(body, layer, prev, *, grid, in_specs, args, out_specs, out_shapes, scratch, name):
    n_in = len(args)
    aliased = [] if prev is None else list(prev)
    return pl.pallas_call(
        _skip_aliased(body, n_in, len(aliased)),
        grid=grid,
        in_specs=list(in_specs) + [pl.BlockSpec(memory_space=pl.ANY)] * len(aliased),
        out_specs=out_specs,
        out_shape=out_shapes,
        input_output_aliases={n_in + i: 1 + i for i in range(len(aliased))},
        scratch_shapes=scratch,
        compiler_params=pltpu.CompilerParams(
            dimension_semantics=("arbitrary",) * len(grid), vmem_limit_bytes=VMEM_LIMIT),
        name=name,
    )(*args, *aliased)


def _ada_kernel(c_ref, w_ref, b_ref, o_ref):
    c = c_ref[...]
    a = (c * jax.nn.sigmoid(c)).astype(BF16)
    o_ref[0] = _dot(a, w_ref[0].astype(BF16)) + b_ref[0]


def _ada_call(c_all, w_ada, b_ada):
    n = c_all.shape[0]
    return pl.pallas_call(
        _ada_kernel,
        grid=(DEPTH, N_MOD * D_MODEL // ADA_TILE),
        in_specs=[
            pl.BlockSpec((n, D_MODEL), lambda l, j: (0, 0)),
            pl.BlockSpec((1, D_MODEL, ADA_TILE), lambda l, j: (l, 0, j)),
            pl.BlockSpec((1, 1, ADA_TILE), lambda l, j: (l, 0, j)),
        ],
        out_specs=pl.BlockSpec((1, n, ADA_TILE), lambda l, j: (l, 0, j)),
        out_shape=jax.ShapeDtypeStruct((DEPTH, n, N_MOD * D_MODEL), F32),
        compiler_params=pltpu.CompilerParams(
            dimension_semantics=("arbitrary", "arbitrary"), vmem_limit_bytes=VMEM_LIMIT),
        name="ada_mod",
    )(c_all, w_ada, b_ada.reshape(DEPTH, 1, N_MOD * D_MODEL))


def _modulated_rmsnorm(x, gain, scale, shift):
    r = lax.rsqrt(jnp.mean(x * x, axis=-1, keepdims=True) + EPS)
    return (x * r) * (gain * (1.0 + scale)) + shift


def _head_rmsnorm(z, ones_bd, gain):
    ssq = _dot((z * z).astype(BF16), ones_bd)
    return z * lax.rsqrt(ssq * (1.0 / HEAD_DIM) + EPS) * gain


def _rope(z, cos_t, sin_dn, sin_up):
    half = ROPE_DIMS // 2
    cols = []
    for p in range(z.shape[-1] // LANES):
        zp = z[:, p * LANES:(p + 1) * LANES]
        cols.append(zp * cos_t
                    + pltpu.roll(zp, LANES - half, axis=1) * sin_dn
                    + pltpu.roll(zp, half, axis=1) * sin_up)
    return cols[0] if len(cols) == 1 else jnp.concatenate(cols, axis=1)


def _split_heads(z):
    lane = lax.broadcasted_iota(jnp.int32, z.shape, 1)
    lo = lane < HEAD_DIM
    zs = pltpu.roll(z, HEAD_DIM, axis=1)
    zero = jnp.zeros_like(z)
    return (jnp.where(lo, z, zero), jnp.where(lo, zero, zs),
            jnp.where(lo, zs, zero), jnp.where(lo, zero, z))


def _gelu_layernorm(zv, ln_g, ln_b):
    gv = jax.nn.gelu(zv)
    cols = []
    for g in range(GMLP_GROUPS):
        xg = gv[:, g * GMLP_GW:(g + 1) * GMLP_GW]
        mu = jnp.mean(xg, axis=-1, keepdims=True)
        xc = xg - mu
        var = jnp.mean(xc * xc, axis=-1, keepdims=True)
        cols.append(xc * lax.rsqrt(var + EPS))
    return jnp.concatenate(cols, axis=1) * ln_g + ln_b


def _spatial_gate_into(mix_ref, u, vg, ws_ref, bs_ref):
    vgb = vg.astype(BF16)
    for g in range(GMLP_GROUPS):
        w = ws_ref[g]
        bias = bs_ref[:, g:g + 1]
        cs = slice(g * GMLP_GW, (g + 1) * GMLP_GW)
        for c in range(u.shape[0] // CHUNK):
            rs = slice(c * CHUNK, (c + 1) * CHUNK)
            z = _dot(w, vgb[rs, cs]) + bias
            mix_ref[rs, ATTN_WIDTH + g * GMLP_GW:ATTN_WIDTH + (g + 1) * GMLP_GW] = (
                u[rs, cs] * z).astype(BF16)


def _lane_pair(a, b, shape):
    lane = lax.broadcasted_iota(jnp.int32, shape, 1)
    return jnp.where(lane < HEAD_DIM, a, b)


def _expand_rows(dst_ref, src, reps):
    for c in range(dst_ref.shape[0]):
        piece = src[:, c * LANES:(c + 1) * LANES]
        for t in range(reps):
            dst_ref[c, pl.ds(t, src.shape[0], stride=reps), :] = piece


def _read_cols(ref, rows=slice(None)):
    return jnp.concatenate([ref[c, rows, :] for c in range(ref.shape[0])], axis=1)


def _interleave(tasks, width):
    pending = iter(tasks)
    active = []
    while True:
        while len(active) < width:
            task = next(pending, None)
            if task is None:
                break
            active.append(task)
        if not active:
            return
        for task in list(active):
            if next(task, "done") == "done":
                active.remove(task)


def _mix_prompt_kernel(xa_ref, xc_ref, moda_ref, modc_ref, gattn_ref, win_ref, gq_ref, gk_ref,
                       bdq_ref, bdk_ref, rope_ref, sinks_ref, lng_ref, lnb_ref, ws_ref, bs_ref,
                       wout_ref, y_ref, kout_ref, vout_ref,
                       q_ring, k_ring, v_ring, u_ring, vg_ring, mix_ring, *, layer, tiles_per_seq):
    tt = xa_ref.shape[1]
    g = pl.program_id(0)

    @pl.when(g == 0)
    def _():
        for ring in (q_ring, k_ring, v_ring, u_ring, vg_ring, mix_ring):
            ring[...] = jnp.zeros(ring.shape, ring.dtype)

    cur, prv = g % 2, (g + 1) % 2
    kv_a, kv_b, kv_p = g % 3, (g + 2) % 3, (g + 1) % 3

    cos_t, sin_dn, sin_up = rope_ref[0], rope_ref[1], rope_ref[2]
    env = {}

    def norm_task():
        env["hb"] = _modulated_rmsnorm(
            xa_ref[0], gattn_ref[...], moda_ref[1:2], moda_ref[0:1]).astype(BF16)
        yield

    def q_task():
        q = _dot(env["hb"], win_ref[:, :ATTN_WIDTH])
        yield
        ssq = _dot((q * q).astype(BF16), bdq_ref[...])
        yield
        q = q * lax.rsqrt(ssq * (1.0 / HEAD_DIM) + EPS) * gq_ref[...]
        q_ring[cur] = (_rope(q, cos_t, sin_dn, sin_up) * (HEAD_DIM ** -0.5)).astype(BF16)

    def kv_task():
        kv = _dot(env["hb"], win_ref[:, ATTN_WIDTH:ATTN_WIDTH + 2 * KV_WIDTH])
        yield
        k, v = kv[:, :KV_WIDTH], kv[:, KV_WIDTH:]
        ssq = _dot((k * k).astype(BF16), bdk_ref[...])
        vout_ref[0] = v[tt - WINDOW:, :]
        for i, vz in enumerate(_split_heads(v)):
            v_ring[kv_a, i] = vz.astype(BF16)
        yield
        k = _rope(k * lax.rsqrt(ssq * (1.0 / HEAD_DIM) + EPS) * gk_ref[...], cos_t, sin_dn, sin_up)
        kout_ref[0] = k[tt - WINDOW:, :]
        for i, kz in enumerate(_split_heads(k)):
            k_ring[kv_a, i] = kz.astype(BF16)

    def u_task():
        zu = _dot(env["hb"], win_ref[:, ATTN_WIDTH + 2 * KV_WIDTH:IN_WIDTH - GMLP_WIDTH])
        yield
        u_ring[cur] = jax.nn.gelu(zu)

    def vg_task():
        zv = _dot(env["hb"], win_ref[:, IN_WIDTH - GMLP_WIDTH:])
        yield
        vg_ring[cur] = _gelu_layernorm(zv, lng_ref[...], lnb_ref[...]).astype(BF16)

    row = lax.broadcasted_iota(jnp.int32, (WINDOW, 2 * WINDOW), 0)
    col = lax.broadcasted_iota(jnp.int32, (WINDOW, 2 * WINDOW), 1)
    band = (col - row >= 1) & (col - row <= WINDOW)
    seq_start = (g + tiles_per_seq - 1) % tiles_per_seq == 0
    band_first = band & (col >= jnp.where(seq_start, WINDOW, 0))
    tail = slice(tt - WINDOW, tt)

    def band_rows(ring, idx, i):
        if i == 0:
            return [ring[kv_p, idx, tail, :], ring[kv_b, idx, 0:WINDOW, :]]
        return [ring[kv_b, idx, (i - 1) * WINDOW:(i + 1) * WINDOW, :]]

    def attn_task(i, p):
        mask = band_first if i == 0 else band
        rows = slice(i * WINDOW, (i + 1) * WINDOW)
        kv = p // 2
        kblk = jnp.concatenate(band_rows(k_ring, 2 * kv, i) + band_rows(k_ring, 2 * kv + 1, i), axis=0)
        s = _dot_nt(q_ring[prv, rows, p * LANES:(p + 1) * LANES], kblk)
        yield
        es, invs = [], []
        for hh in range(2):
            sink = sinks_ref[layer, 2 * p + hh]
            sh_ = jnp.where(mask, s[:, hh * 2 * WINDOW:(hh + 1) * 2 * WINDOW], NEG)
            m = jnp.maximum(jnp.max(sh_, axis=-1, keepdims=True), sink)
            e = jnp.exp(sh_ - m)
            es.append(e)
            invs.append(1.0 / (jnp.sum(e, axis=-1, keepdims=True) + jnp.exp(sink - m)))
        vblk = jnp.concatenate(band_rows(v_ring, 2 * kv, i) + band_rows(v_ring, 2 * kv + 1, i), axis=0)
        o = _dot(jnp.concatenate(es, axis=1).astype(BF16), vblk)
        yield
        o = o * _lane_pair(invs[0], invs[1], o.shape)
        mix_ring[cur, rows, p * LANES:(p + 1) * LANES] = o.astype(BF16)

    def gate_task(c, grp):
        rs = slice(c * CHUNK, (c + 1) * CHUNK)
        cs = slice(grp * GMLP_GW, (grp + 1) * GMLP_GW)
        z = _dot(ws_ref[grp], vg_ring[prv, rs, cs])
        yield
        z = z + bs_ref[:, grp:grp + 1]
        mix_ring[cur, rs, ATTN_WIDTH + grp * GMLP_GW:ATTN_WIDTH + (grp + 1) * GMLP_GW] = (
            u_ring[prv, rs, cs] * z).astype(BF16)

    def out_task(c):
        cols = slice(c * OUT_COLS, (c + 1) * OUT_COLS)
        r = _dot(mix_ring[prv], wout_ref[:, cols])
        yield
        y_ref[0, :, cols] = xc_ref[0, :, cols] + modc_ref[2:3, cols] * r

    n_blocks = tt // WINDOW
    attn = [attn_task(i, p) for i in range(n_blocks) for p in range(N_PAIRS)]
    gate = [gate_task(c, grp) for c in range(n_blocks) for grp in range(GMLP_GROUPS)]
    big = [out_task(0), norm_task(), vg_task(), out_task(1), u_task(), q_task(), out_task(2),
           kv_task(), out_task(3)]
    small = [t for pair in zip(attn, gate) for t in pair]
    per_big = -(-len(small) // len(big))
    order = []
    for b in big:
        order.append(b)
        order.extend(small[:per_big])
        small = small[per_big:]
    _interleave(order + small, MIX_INTERLEAVE)


def _mix_weight_specs(layer, gate_kind):
    return [
        _layer_spec(layer, (1, D_MODEL)),
        _layer_spec(layer, (D_MODEL, IN_WIDTH)),
        _layer_spec(layer, (1, ATTN_WIDTH)),
        _layer_spec(layer, (1, KV_WIDTH)),
        _const_spec((ATTN_WIDTH, ATTN_WIDTH)),
        _const_spec((KV_WIDTH, KV_WIDTH)),
    ], [
        pl.BlockSpec(memory_space=pltpu.SMEM),
        _layer_spec(layer, (1, GMLP_WIDTH)),
        _layer_spec(layer, (1, GMLP_WIDTH)),
        _layer_spec(layer, (GMLP_GROUPS, CHUNK, CHUNK)),
        _layer_spec(layer, (CHUNK, GMLP_GROUPS)),
        _layer_spec(layer, (D_MODEL, D_MODEL)),
    ]


def _mix_weight_args(pp, kind):
    return ([pp["g_attn"], pp["w_in"], pp["g_q"], pp["g_k"], pp["bd_q"], pp["bd_k"]],
            [pp["sinks"], pp["ln_g"], pp["ln_b"], pp["ws_" + kind], pp["bs_" + kind], pp["w_out"]])


def _mix_prompt_call(layer, x, mod4, n_skip, pp, rope, prev):
    b, t, _ = x.shape
    tt = PROMPT_TILE
    nt = t // tt
    last = b * nt - 1
    w_specs_a, w_specs_b = _mix_weight_specs(layer, "prompt")
    w_args_a, w_args_b = _mix_weight_args(pp, "prompt")

    def tile(g, lag):
        return jnp.clip(g - lag, 0, last)

    def x_spec(lag):
        return pl.BlockSpec((1, tt, D_MODEL), lambda g: (tile(g, lag) // nt, tile(g, lag) % nt, 0))

    def mod_spec(lag):
        return pl.BlockSpec((None, None, N_MOD, D_MODEL),
                            lambda g: (layer, n_skip + tile(g, lag) // nt, 0, 0))

    kv_spec = pl.BlockSpec((None, 1, WINDOW, KV_WIDTH), lambda g: (layer, tile(g, 0) // nt, 0, 0))
    kv_shape = jax.ShapeDtypeStruct((DEPTH, b, WINDOW, KV_WIDTH), F32)
    return _layer_call(
        functools.partial(_mix_prompt_kernel, layer=layer, tiles_per_seq=nt), layer, prev,
        grid=(b * nt + 2,),
        in_specs=[x_spec(0), x_spec(2), mod_spec(0), mod_spec(2)] + w_specs_a
        + [pl.BlockSpec((3, tt, LANES), lambda g: (0, tile(g, 0) % nt, 0))] + w_specs_b,
        args=[x, x, mod4, mod4] + w_args_a + [rope] + w_args_b,
        out_specs=[x_spec(2), kv_spec, kv_spec],
        out_shapes=[jax.ShapeDtypeStruct((b, t, D_MODEL), F32), kv_shape, kv_shape],
        scratch=[
            pltpu.VMEM((2, tt, ATTN_WIDTH), BF16),
            pltpu.VMEM((3, 4, tt, LANES), BF16),
            pltpu.VMEM((3, 4, tt, LANES), BF16),
            pltpu.VMEM((2, tt, GMLP_WIDTH), F32),
            pltpu.VMEM((2, tt, GMLP_WIDTH), BF16),
            pltpu.VMEM((2, tt, D_MODEL), BF16),
        ],
        name="mix_prompt")


def _mix_sample_kernel(x_ref, sh_ref, sc_ref, ga_ref, ck_ref, cv_ref, gattn_ref, win_ref, gq_ref,
                       gk_ref, bdq_ref, bdk_ref, rope_ref, sinks_ref, lng_ref, lnb_ref, ws_ref,
                       bs_ref, wout_ref, y_ref, kout_ref, vout_ref, vgout_ref,
                       mod_scr, q_scr, s_scr, e_scr, o_scr, mix_ref, *, layer):
    n = x_ref.shape[0]
    sb = sh_ref.shape[0]
    ts = n // sb
    _expand_rows(mod_scr.at[0], sh_ref[...], ts)
    _expand_rows(mod_scr.at[1], sc_ref[...], ts)
    x = x_ref[...]
    h = _modulated_rmsnorm(x, gattn_ref[...], _read_cols(mod_scr.at[1]), _read_cols(mod_scr.at[0]))
    proj = _dot(h.astype(BF16), win_ref[...])
    _expand_rows(mod_scr.at[0], ga_ref[...], ts)

    cos_t, sin_dn, sin_up = rope_ref[0], rope_ref[1], rope_ref[2]
    q = _rope(_head_rmsnorm(proj[:, :ATTN_WIDTH], bdq_ref[...], gq_ref[...]), cos_t, sin_dn, sin_up)
    q = q * (HEAD_DIM ** -0.5)
    q_scr[...] = q
    qb = q.astype(BF16)
    k = _rope(_head_rmsnorm(proj[:, ATTN_WIDTH:ATTN_WIDTH + KV_WIDTH], bdk_ref[...], gk_ref[...]),
              cos_t, sin_dn, sin_up)
    v = proj[:, ATTN_WIDTH + KV_WIDTH:ATTN_WIDTH + 2 * KV_WIDTH]
    kout_ref[...] = k
    vout_ref[...] = v

    def cache_scores(b, carry):
        rows = pl.ds(pl.multiple_of(b * ts, ts), ts)
        kparts = _split_heads(ck_ref[b])
        for kv in range(N_KV_HEADS):
            kblk = jnp.concatenate([kparts[2 * kv], kparts[2 * kv + 1]], axis=0).astype(BF16)
            q16 = jnp.concatenate(
                [q_scr[rows, (2 * kv) * LANES:(2 * kv + 1) * LANES],
                 q_scr[rows, (2 * kv + 1) * LANES:(2 * kv + 2) * LANES]], axis=0).astype(BF16)
            s = _dot_nt(q16, kblk)
            s_scr[2 * kv, rows, :] = s[:ts]
            s_scr[2 * kv + 1, rows, :] = s[ts:]
        return carry

    lax.fori_loop(0, sb, cache_scores, 0)

    knew = [z.astype(BF16) for z in _split_heads(k)]
    vnew = [z.astype(BF16) for z in _split_heads(v)]
    grp = CHUNK
    rown = lax.broadcasted_iota(jnp.int32, (grp, grp), 0)
    coln = lax.broadcasted_iota(jnp.int32, (grp, grp), 1)
    seq_bits = ts.bit_length() - 1
    mask_new = ((rown >> seq_bits) == (coln >> seq_bits)) & (coln <= rown)
    mask_cache = (lax.broadcasted_iota(jnp.int32, (grp, WINDOW), 1)
                  > (lax.broadcasted_iota(jnp.int32, (grp, WINDOW), 0) & (ts - 1)))

    for p in range(N_PAIRS):
        kv = p // 2
        for g0 in range(0, n, grp):
            rows = slice(g0, g0 + grp)
            kblk = jnp.concatenate([knew[2 * kv][rows], knew[2 * kv + 1][rows]], axis=0)
            vblk = jnp.concatenate([vnew[2 * kv][rows], vnew[2 * kv + 1][rows]], axis=0)
            s_new = _dot_nt(qb[rows, p * LANES:(p + 1) * LANES], kblk)
            e_new, invs = [], []
            for hh in range(2):
                sink = sinks_ref[layer, 2 * p + hh]
                sn = jnp.where(mask_new, s_new[:, hh * grp:(hh + 1) * grp], NEG)
                sc_ = jnp.where(mask_cache, s_scr[p, rows, hh * WINDOW:(hh + 1) * WINDOW], NEG)
                m = jnp.maximum(jnp.maximum(jnp.max(sn, axis=-1, keepdims=True),
                                            jnp.max(sc_, axis=-1, keepdims=True)), sink)
                en = jnp.exp(sn - m)
                ec = jnp.exp(sc_ - m)
                den = (jnp.sum(en, axis=-1, keepdims=True) + jnp.sum(ec, axis=-1, keepdims=True)
                       + jnp.exp(sink - m))
                e_new.append(en)
                invs.append(1.0 / den)
                e_scr[p, rows, hh * WINDOW:(hh + 1) * WINDOW] = ec
            o_scr[p, rows, :] = _dot(jnp.concatenate(e_new, axis=1).astype(BF16), vblk)
            o_scr[N_PAIRS + p, rows, :] = _lane_pair(invs[0], invs[1], (grp, LANES))

    def cache_pv(b, carry):
        rows = pl.ds(pl.multiple_of(b * ts, ts), ts)
        vparts = _split_heads(cv_ref[b])
        for kv in range(N_KV_HEADS):
            vblk = jnp.concatenate([vparts[2 * kv], vparts[2 * kv + 1]], axis=0).astype(BF16)
            e16 = jnp.concatenate([e_scr[2 * kv, rows, :], e_scr[2 * kv + 1, rows, :]],
                                  axis=0).astype(BF16)
            o = _dot(e16, vblk)
            o_scr[2 * kv, rows, :] += o[:ts]
            o_scr[2 * kv + 1, rows, :] += o[ts:]
        return carry

    lax.fori_loop(0, sb, cache_pv, 0)

    for p in range(N_PAIRS):
        mix_ref[:, p * LANES:(p + 1) * LANES] = (o_scr[p] * o_scr[N_PAIRS + p]).astype(BF16)

    u = jax.nn.gelu(proj[:, ATTN_WIDTH + 2 * KV_WIDTH:ATTN_WIDTH + 2 * KV_WIDTH + GMLP_WIDTH])
    vg = _gelu_layernorm(proj[:, IN_WIDTH - GMLP_WIDTH:], lng_ref[...], lnb_ref[...])
    vgout_ref[...] = vg
    _spatial_gate_into(mix_ref, u, vg, ws_ref, bs_ref)

    y_ref[...] = x + _read_cols(mod_scr.at[0]) * _dot(mix_ref[...], wout_ref[...])


def _sample_mod_specs(layer, sb, first):
    return [pl.BlockSpec((None, sb, D_MODEL), functools.partial(lambda i, w: (layer, i, w), w=first + m))
            for m in range(3)]


def _mix_sample_call(layer, x, mod3, cache_k, cache_v, pp, rope, ts, prev):
    nt = x.shape[0]
    sb = SAMPLE_SEQS
    n = sb * ts
    w_specs_a, w_specs_b = _mix_weight_specs(layer, "sample")
    w_args_a, w_args_b = _mix_weight_args(pp, "sample")
    tok_spec = lambda w: pl.BlockSpec((None, n, w), lambda i: (layer, i, 0))
    tok_shape = lambda w: jax.ShapeDtypeStruct((DEPTH, nt, w), F32)
    cache_spec = pl.BlockSpec((None, sb, WINDOW, KV_WIDTH), lambda i: (layer, i, 0, 0))
    return _layer_call(
        functools.partial(_mix_sample_kernel, layer=layer), layer, prev,
        grid=(nt // n,),
        in_specs=[pl.BlockSpec((n, D_MODEL), lambda i: (i, 0))] + _sample_mod_specs(layer, sb, 0)
        + [cache_spec, cache_spec] + w_specs_a + [_const_spec((3, n, LANES))] + w_specs_b,
        args=[x, mod3, mod3, mod3, cache_k, cache_v] + w_args_a + [rope] + w_args_b,
        out_specs=[pl.BlockSpec((n, D_MODEL), lambda i: (i, 0)),
                   tok_spec(KV_WIDTH), tok_spec(KV_WIDTH), tok_spec(GMLP_WIDTH)],
        out_shapes=[jax.ShapeDtypeStruct((nt, D_MODEL), F32),
                    tok_shape(KV_WIDTH), tok_shape(KV_WIDTH), tok_shape(GMLP_WIDTH)],
        scratch=[
            pltpu.VMEM((2, D_MODEL // LANES, n, LANES), F32),
            pltpu.VMEM((n, ATTN_WIDTH), F32),
            pltpu.VMEM((N_PAIRS, n, 2 * WINDOW), F32),
            pltpu.VMEM((N_PAIRS, n, 2 * WINDOW), F32),
            pltpu.VMEM((2 * N_PAIRS, n, LANES), F32),
            pltpu.VMEM((n, D_MODEL), BF16),
        ],
        name="mix_sample")


def _shifted_rows(hu, before):
    rows, c = hu.shape
    hu3 = hu.reshape(rows // SUBLANES, SUBLANES, c)
    above = jnp.concatenate([before[None], hu3[:-1]], axis=0)
    sub = lax.broadcasted_iota(jnp.int32, (1, SUBLANES, c), 1)
    p1 = pltpu.roll(jnp.where(sub >= SUBLANES - 1, above, hu3), 1, axis=1)
    p2 = pltpu.roll(jnp.where(sub >= SUBLANES - 2, above, hu3), 2, axis=1)
    return p1.reshape(rows, c), p2.reshape(rows, c)


def _conv(hu, prev1, prev2, cw, cb):
    return cb + cw[0:1] * prev2 + cw[1:2] * prev1 + cw[2:3] * hu


def _ffn_chunks():
    return [(slice(j, j + FFN_CHUNK), slice(D_FF + j, D_FF + j + FFN_CHUNK))
            for j in range(0, D_FF, FFN_CHUNK)]


def _down_plan(n_chunks):
    bounds = list(range(0, n_chunks, DOWN_GROUP)) + [n_chunks]
    groups = [(a * FFN_CHUNK, b * FFN_CHUNK) for a, b in zip(bounds[:-1], bounds[1:])]
    return {b // FFN_CHUNK: (a, b) for a, b in groups[:-1]}, groups[-1]


def _down_into(acc_ref, act_scr, wdn_ref, a, b):
    contrib = _dot(act_scr[:, a:b], wdn_ref[a:b, :])
    if a == 0:
        acc_ref[...] = contrib
    else:
        acc_ref[...] += contrib


def _ffn_prompt_kernel(x_ref, mod_ref, gffn_ref, wup_ref, cw_ref, cb_ref, wdn_ref,
                       y_ref, cout_ref, carry, act_scr, acc_ref):
    tt = x_ref.shape[1]
    t = pl.program_id(1)

    @pl.when(t == 0)
    def _():
        carry[...] = jnp.zeros(carry.shape, F32)

    x = x_ref[0]
    hb = _modulated_rmsnorm(x, gffn_ref[...], mod_ref[4:5], mod_ref[3:4]).astype(BF16)
    chunks = _ffn_chunks()
    after, tail = _down_plan(len(chunks))
    for j, halves in enumerate(chunks):
        hus = [_dot(hb, wup_ref[:, cols]) for cols in halves]
        if j in after:
            _down_into(acc_ref, act_scr, wdn_ref, *after[j])
        convs = []
        for hu, cols in zip(hus, halves):
            prev1, prev2 = _shifted_rows(hu, carry[:, cols])
            carry[:, cols] = hu[tt - SUBLANES:, :]
            cout_ref[0, :, cols] = hu[tt - (CONV_W - 1):, :]
            convs.append(_conv(hu, prev1, prev2, cw_ref[:, cols], cb_ref[:, cols]))
        act_scr[:, j * FFN_CHUNK:(j + 1) * FFN_CHUNK] = (
            jax.nn.silu(convs[0]) * convs[1]).astype(BF16)
    _down_into(acc_ref, act_scr, wdn_ref, *tail)
    y_ref[0] = x + mod_ref[5:6] * acc_ref[...]


def _ffn_weight_specs(layer):
    return [
        _layer_spec(layer, (1, D_MODEL)),
        _layer_spec(layer, (D_MODEL, 2 * D_FF), single_buffer=True),
        _layer_spec(layer, (CONV_W, 2 * D_FF)),
        _layer_spec(layer, (1, 2 * D_FF)),
        _layer_spec(layer, (D_FF, D_MODEL), single_buffer=True),
    ]


def _ffn_weight_args(pp):
    return [pp["g_ffn"], pp["w_up"], pp["conv_w"], pp["conv_b"], pp["w_down"]]


def _ffn_prompt_call(layer, x, mod4, n_skip, pp, prev):
    b, t, _ = x.shape
    tt = FFN_TILE
    return _layer_call(
        _ffn_prompt_kernel, layer, prev,
        grid=(b, t // tt),
        in_specs=[pl.BlockSpec((1, tt, D_MODEL), lambda i, j: (i, j, 0)),
                  pl.BlockSpec((None, None, N_MOD, D_MODEL), lambda i, j: (layer, n_skip + i, 0, 0))]
        + _ffn_weight_specs(layer),
        args=[x, mod4] + _ffn_weight_args(pp),
        out_specs=[pl.BlockSpec((1, tt, D_MODEL), lambda i, j: (i, j, 0)),
                   pl.BlockSpec((None, 1, CONV_W - 1, 2 * D_FF), lambda i, j: (layer, i, 0, 0))],
        out_shapes=[jax.ShapeDtypeStruct((b, t, D_MODEL), F32),
                    jax.ShapeDtypeStruct((DEPTH, b, CONV_W - 1, 2 * D_FF), F32)],
        scratch=[
            pltpu.VMEM((SUBLANES, 2 * D_FF), F32),
            pltpu.VMEM((tt, D_FF), BF16),
            pltpu.VMEM((tt, D_MODEL), F32),
        ],
        name="ffn_prompt")


def _ffn_sample_kernel(x_ref, sh_ref, sc_ref, gf_ref, c0_ref, c1_ref, gffn_ref, wup_ref, cw_ref,
                       cb_ref, wdn_ref, y_ref, cout_ref, mod_scr, h1, h2, act_scr, acc_ref):
    n = x_ref.shape[0]
    sb = sh_ref.shape[0]
    ts = n // sb
    pad = SUBLANES
    _expand_rows(mod_scr.at[0], sh_ref[...], ts)
    _expand_rows(mod_scr.at[1], sc_ref[...], ts)
    x = x_ref[...]
    hb = _modulated_rmsnorm(x, gffn_ref[...], _read_cols(mod_scr.at[1]),
                            _read_cols(mod_scr.at[0])).astype(BF16)
    _expand_rows(mod_scr.at[0], gf_ref[...], ts)
    chunks = _ffn_chunks()
    after, tail = _down_plan(len(chunks))
    for j, halves in enumerate(chunks):
        hus = [_dot(hb, wup_ref[:, cols]) for cols in halves]
        if j in after:
            _down_into(acc_ref, act_scr, wdn_ref, *after[j])
        convs = []
        for hu, cols in zip(hus, halves):
            for c in range(FFN_CHUNK // LANES):
                off = cols.start + c * LANES
                piece = hu[:, c * LANES:(c + 1) * LANES]
                h1[c, pad:, :] = piece
                h2[c, pad:, :] = piece
                cout_ref[:, off:off + LANES] = h2[c, pl.ds(pad + ts - 2, sb, stride=ts), :]
                cout_ref[:, 2 * D_FF + off:2 * D_FF + off + LANES] = (
                    h2[c, pl.ds(pad + ts - 1, sb, stride=ts), :])
                c0 = c0_ref[:, off:off + LANES]
                c1 = c1_ref[:, off:off + LANES]
                h1[c, pl.ds(pad - 1, sb, stride=ts), :] = c1
                h2[c, pl.ds(pad - 2, sb, stride=ts), :] = c0
                h2[c, pl.ds(pad - 1, sb, stride=ts), :] = c1
            convs.append(_conv(hu, _read_cols(h1, slice(pad - 1, pad - 1 + n)),
                               _read_cols(h2, slice(pad - 2, pad - 2 + n)),
                               cw_ref[:, cols], cb_ref[:, cols]))
        act_scr[:, j * FFN_CHUNK:(j + 1) * FFN_CHUNK] = (
            jax.nn.silu(convs[0]) * convs[1]).astype(BF16)
    _down_into(acc_ref, act_scr, wdn_ref, *tail)
    y_ref[...] = x + _read_cols(mod_scr.at[0]) * acc_ref[...]


def _ffn_sample_call(layer, x, mod3, cconv, pp, ts, prev):
    nt = x.shape[0]
    sb = SAMPLE_SEQS
    n = sb * ts
    tok_spec = pl.BlockSpec((n, D_MODEL), lambda i: (i, 0))
    hist_spec = lambda r: pl.BlockSpec((None, sb, 2 * D_FF), lambda i: (layer, i, r))
    return _layer_call(
        _ffn_sample_kernel, layer, prev,
        grid=(nt // n,),
        in_specs=[tok_spec] + _sample_mod_specs(layer, sb, 3) + [hist_spec(0), hist_spec(1)]
        + _ffn_weight_specs(layer),
        args=[x, mod3, mod3, mod3, cconv, cconv] + _ffn_weight_args(pp),
        out_specs=[tok_spec,
                   pl.BlockSpec((None, sb, (CONV_W - 1) * 2 * D_FF), lambda i: (layer, i, 0))],
        out_shapes=[jax.ShapeDtypeStruct((nt, D_MODEL), F32),
                    jax.ShapeDtypeStruct(cconv.shape, F32)],
        scratch=[
            pltpu.VMEM((2, D_MODEL // LANES, n, LANES), F32),
            pltpu.VMEM((FFN_CHUNK // LANES, n + SUBLANES, LANES), F32),
            pltpu.VMEM((FFN_CHUNK // LANES, n + SUBLANES, LANES), F32),
            pltpu.VMEM((n, D_FF), BF16),
            pltpu.VMEM((n, D_MODEL), F32),
        ],
        name="ffn_sample")


def _rope_tables(pos):
    half = ROPE_DIMS // 2
    inv = ROPE_THETA ** (-jnp.arange(0, ROPE_DIMS, 2, dtype=F32) / ROPE_DIMS)
    ang = pos.astype(F32)[:, None] * inv[None, :]
    cos, sin = jnp.cos(ang), jnp.sin(ang)
    n = pos.shape[0]
    rest = jnp.zeros((n, HEAD_DIM - ROPE_DIMS), F32)
    zeros = jnp.zeros((n, half), F32)
    cos_t = jnp.concatenate([cos, cos, rest + 1.0], axis=1)
    sin_dn = jnp.concatenate([-sin, zeros, rest], axis=1)
    sin_up = jnp.concatenate([zeros, sin, rest], axis=1)
    return jnp.stack([jnp.tile(z, (1, LANES // HEAD_DIM)) for z in (cos_t, sin_dn, sin_up)])


def _block_diag_ones(width):
    idx = jnp.arange(width) // HEAD_DIM
    return (idx[:, None] == idx[None, :]).astype(BF16)


def _prepare_params(dec_seq, g_attn, w_in, g_q, g_k, sinks, ln_g, ln_b, w_s, b_s, w_out,
                    g_ffn, w_ffn_in, conv_w, conv_b, w_ffn_out):
    causal = jnp.tril(jnp.ones((CHUNK, CHUNK), dtype=bool))
    ws = jnp.where(causal, w_s, 0.0)
    seqs_per_chunk = CHUNK // dec_seq
    eye = jnp.eye(seqs_per_chunk, dtype=F32)
    ws_sample = jnp.einsum("ab,lgts->lgatbs", eye, ws[:, :, :dec_seq, :dec_seq]).reshape(w_s.shape)
    return {
        "g_attn": g_attn[:, None, :], "g_ffn": g_ffn[:, None, :],
        "w_in": w_in.astype(BF16), "w_out": w_out.astype(BF16),
        "g_q": jnp.tile(g_q, (1, N_HEADS))[:, None, :],
        "g_k": jnp.tile(g_k, (1, N_KV_HEADS))[:, None, :],
        "bd_q": _block_diag_ones(ATTN_WIDTH), "bd_k": _block_diag_ones(KV_WIDTH),
        "sinks": sinks,
        "ln_g": ln_g.reshape(DEPTH, 1, GMLP_WIDTH), "ln_b": ln_b.reshape(DEPTH, 1, GMLP_WIDTH),
        "ws_prompt": ws.astype(BF16), "bs_prompt": jnp.swapaxes(b_s, 1, 2),
        "ws_sample": ws_sample.astype(BF16),
        "bs_sample": jnp.swapaxes(jnp.tile(b_s[:, :, :dec_seq], (1, 1, seqs_per_chunk)), 1, 2),
        "w_up": w_ffn_in.astype(BF16), "w_down": w_ffn_out.astype(BF16),
        "conv_w": conv_w, "conv_b": conv_b[:, None, :],
    }


def kernel(x_prompt, x_sample, cache_k, cache_v, cache_conv, c_prompt, c_sample, w_ada, b_ada,
           g_attn, w_in, g_q, g_k, sinks, ln_g, ln_b, w_s, b_s, w_out, g_ffn, w_ffn_in, conv_w,
           conv_b, w_ffn_out):
    nbp, seq, _ = x_prompt.shape
    nbs, dec_seq, _ = x_sample.shape
    assert seq % PROMPT_TILE == 0 and PROMPT_TILE % CHUNK == 0 and nbs % SAMPLE_SEQS == 0
    assert seq % FFN_TILE == 0 and FFN_TILE % SUBLANES == 0 and D_FF % FFN_CHUNK == 0
    assert dec_seq == SUBLANES and CHUNK % dec_seq == 0 and cache_k.shape[2] == WINDOW
    assert nbs % SUBLANES == 0

    n_c = nbs + nbp
    c_all = jnp.concatenate([c_sample, c_prompt, jnp.zeros((-n_c % SUBLANES, D_MODEL), F32)])
    mod3 = _ada_call(c_all, w_ada, b_ada)
    mod4 = mod3.reshape(DEPTH, c_all.shape[0], N_MOD, D_MODEL)

    pp = _prepare_params(dec_seq, g_attn, w_in, g_q, g_k, sinks, ln_g, ln_b, w_s, b_s, w_out,
                         g_ffn, w_ffn_in, conv_w, conv_b, w_ffn_out)
    rope_p = _rope_tables(jnp.arange(seq, dtype=jnp.int32))
    rope_s = jnp.tile(_rope_tables(PAST_LEN + jnp.arange(dec_seq, dtype=jnp.int32)),
                      (1, SAMPLE_SEQS, 1))
    ck = cache_k.reshape(DEPTH, nbs, WINDOW, KV_WIDTH)
    cv = cache_v.reshape(DEPTH, nbs, WINDOW, KV_WIDTH)
    cconv = cache_conv.reshape(DEPTH, nbs, (CONV_W - 1) * 2 * D_FF)

    xp = x_prompt
    xs = x_sample.reshape(nbs * dec_seq, D_MODEL)
    mix_p = ffn_p = mix_s = ffn_s = None
    for l in range(DEPTH):
        xp, *mix_p = _mix_prompt_call(l, xp, mod4, nbs, pp, rope_p, mix_p)
        xp, *ffn_p = _ffn_prompt_call(l, xp, mod4, nbs, pp, ffn_p)
        xs, *mix_s = _mix_sample_call(l, xs, mod3, ck, cv, pp, rope_s, dec_seq, mix_s)
        xs, *ffn_s = _ffn_sample_call(l, xs, mod3, cconv, pp, dec_seq, ffn_s)

    kv_p = (DEPTH, nbp, WINDOW, N_KV_HEADS, HEAD_DIM)
    kv_s = (DEPTH, nbs, dec_seq, N_KV_HEADS, HEAD_DIM)
    return (xp, xs.reshape(nbs, dec_seq, D_MODEL),
            mix_p[0].reshape(kv_p), mix_p[1].reshape(kv_p), ffn_p[0],
            mix_s[0].reshape(kv_s), mix_s[1].reshape(kv_s),
            mix_s[2].reshape(DEPTH, nbs, dec_seq, GMLP_WIDTH),
            ffn_s[0].reshape(DEPTH, nbs, CONV_W - 1, 2 * D_FF))
```

```python
import functools

import jax
import jax.numpy as jnp
from jax import lax
from jax.experimental import pallas as pl
from jax.experimental.pallas import tpu as pltpu

D_MODEL = 1024
DEPTH = 2
HEAD_DIM = 64
N_HEADS = 8
N_KV_HEADS = 2
KV_WIDTH = N_KV_HEADS * HEAD_DIM
WINDOW = 128
ROPE_THETA = 500000.0
ROPE_DIMS = HEAD_DIM // 4
ATTN_WIDTH = N_HEADS * HEAD_DIM
GMLP_WIDTH = D_MODEL - ATTN_WIDTH
GMLP_GROUPS = 4
GMLP_GW = GMLP_WIDTH // GMLP_GROUPS
CHUNK = 128
IN_WIDTH = ATTN_WIDTH + 2 * KV_WIDTH + 2 * GMLP_WIDTH
D_FF = 2816
CONV_W = 3
N_MOD = 6
EPS = 1e-6
NEG = -1e30
PAST_LEN = 16384

LANES = 128
SUBLANES = 8
N_PAIRS = N_HEADS // 2
VMEM_LIMIT = 56 * 1024 * 1024

PROMPT_TILE = 512
OUT_COLS = 256
MIX_INTERLEAVE = 10
FFN_TILE = 1024
SAMPLE_SEQS = 64
FFN_CHUNK = 256
DOWN_GROUP = 2
ADA_TILE = 1536

F32 = jnp.float32
BF16 = jnp.bfloat16


def _dot(a, b):
    return jnp.dot(a, b, preferred_element_type=F32)


def _dot_nt(a, b):
    return lax.dot_general(a, b, (((1,), (1,)), ((), ())), preferred_element_type=F32)


def _const_spec(shape):
    nd = len(shape)
    return pl.BlockSpec(shape, lambda *_: (0,) * nd)


def _layer_spec(layer, tail, single_buffer=False):
    nd = len(tail)
    kw = {"pipeline_mode": pl.Buffered(1)} if single_buffer else {}
    return pl.BlockSpec((None,) + tuple(tail), lambda *_: (layer,) + (0,) * nd, **kw)


def _skip_aliased(body, n_in, n_alias):
    if n_alias == 0:
        return body
    return lambda *refs: body(*refs[:n_in], *refs[n_in + n_alias:])


def _layer_call(body, layer, prev, *, grid, in_specs, args, out_specs, out_shapes, scratch, name):
    n_in = len(args)
    aliased = [] if prev is None else list(prev)
    return pl.pallas_call(
        _skip_aliased(body, n_in, len(aliased)),
        grid=grid,
        in_specs=list(in_specs) + [pl.BlockSpec(memory_space=pl.ANY)] * len(aliased),
        out_specs=out_specs,
        out_shape=out_shapes,
        input_output_aliases={n_in + i: 1 + i for i in range(len(aliased))},
        scratch_shapes=scratch,
        compiler_params=pltpu.CompilerParams(
            dimension_semantics=("arbitrary",) * len(grid), vmem_limit_bytes=VMEM_LIMIT),
        name=name,
    )(*args, *aliased)


def _ada_kernel(c_ref, w_ref, b_ref, o_ref):
    c = c_ref[...]
    a = (c * jax.nn.sigmoid(c)).astype(BF16)
    o_ref[0] = _dot(a, w_ref[0].astype(BF16)) + b_ref[0]


def _ada_call(c_all, w_ada, b_ada):
    n = c_all.shape[0]
    return pl.pallas_call(
        _ada_kernel,
        grid=(DEPTH, N_MOD * D_MODEL // ADA_TILE),
        in_specs=[
            pl.BlockSpec((n, D_MODEL), lambda l, j: (0, 0)),
            pl.BlockSpec((1, D_MODEL, ADA_TILE), lambda l, j: (l, 0, j)),
            pl.BlockSpec((1, 1, ADA_TILE), lambda l, j: (l, 0, j)),
        ],
        out_specs=pl.BlockSpec((1, n, ADA_TILE), lambda l, j: (l, 0, j)),
        out_shape=jax.ShapeDtypeStruct((DEPTH, n, N_MOD * D_MODEL), F32),
        compiler_params=pltpu.CompilerParams(
            dimension_semantics=("arbitrary", "arbitrary"), vmem_limit_bytes=VMEM_LIMIT),
        name="ada_mod",
    )(c_all, w_ada, b_ada.reshape(DEPTH, 1, N_MOD * D_MODEL))


def _modulated_rmsnorm(x, gain, scale, shift):
    r = lax.rsqrt(jnp.mean(x * x, axis=-1, keepdims=True) + EPS)
    return (x * r) * (gain * (1.0 + scale)) + shift


def _head_rmsnorm(z, ones_bd, gain):
    ssq = _dot((z * z).astype(BF16), ones_bd)
    return z * lax.rsqrt(ssq * (1.0 / HEAD_DIM) + EPS) * gain


def _rope(z, cos_t, sin_dn, sin_up):
    half = ROPE_DIMS // 2
    cols = []
    for p in range(z.shape[-1] // LANES):
        zp = z[:, p * LANES:(p + 1) * LANES]
        cols.append(zp * cos_t
                    + pltpu.roll(zp, LANES - half, axis=1) * sin_dn
                    + pltpu.roll(zp, half, axis=1) * sin_up)
    return cols[0] if len(cols) == 1 else jnp.concatenate(cols, axis=1)


def _split_heads(z):
    lane = lax.broadcasted_iota(jnp.int32, z.shape, 1)
    lo = lane < HEAD_DIM
    zs = pltpu.roll(z, HEAD_DIM, axis=1)
    zero = jnp.zeros_like(z)
    return (jnp.where(lo, z, zero), jnp.where(lo, zero, zs),
            jnp.where(lo, zs, zero), jnp.where(lo, zero, z))


def _gelu_layernorm(zv, ln_g, ln_b):
    gv = jax.nn.gelu(zv)
    cols = []
    for g in range(GMLP_GROUPS):
        xg = gv[:, g * GMLP_GW:(g + 1) * GMLP_GW]
        mu = jnp.mean(xg, axis=-1, keepdims=True)
        xc = xg - mu
        var = jnp.mean(xc * xc, axis=-1, keepdims=True)
        cols.append(xc * lax.rsqrt(var + EPS))
    return jnp.concatenate(cols, axis=1) * ln_g + ln_b


def _spatial_gate_into(mix_ref, u, vg, ws_ref, bs_ref):
    vgb = vg.astype(BF16)
    for g in range(GMLP_GROUPS):
        w = ws_ref[g]
        bias = bs_ref[:, g:g + 1]
        cs = slice(g * GMLP_GW, (g + 1) * GMLP_GW)
        for c in range(u.shape[0] // CHUNK):
            rs = slice(c * CHUNK, (c + 1) * CHUNK)
            z = _dot(w, vgb[rs, cs]) + bias
            mix_ref[rs, ATTN_WIDTH + g * GMLP_GW:ATTN_WIDTH + (g + 1) * GMLP_GW] = (
                u[rs, cs] * z).astype(BF16)


def _lane_pair(a, b, shape):
    lane = lax.broadcasted_iota(jnp.int32, shape, 1)
    return jnp.where(lane < HEAD_DIM, a, b)


def _expand_rows(dst_ref, src, reps):
    for c in range(dst_ref.shape[0]):
        piece = src[:, c * LANES:(c + 1) * LANES]
        for t in range(reps):
            dst_ref[c, pl.ds(t, src.shape[0], stride=reps), :] = piece


def _read_cols(ref, rows=slice(None)):
    return jnp.concatenate([ref[c, rows, :] for c in range(ref.shape[0])], axis=1)


def _interleave(tasks, width):
    pending = iter(tasks)
    active = []
    while True:
        while len(active) < width:
            task = next(pending, None)
            if task is None:
                break
            active.append(task)
        if not active:
            return
        for task in list(active):
            if next(task, "done") == "done":
                active.remove(task)


def _mix_prompt_kernel(xa_ref, xc_ref, moda_ref, modc_ref, gattn_ref, win_ref, gq_ref, gk_ref,
                       bdq_ref, bdk_ref, rope_ref, sinks_ref, lng_ref, lnb_ref, ws_ref, bs_ref,
                       wout_ref, y_ref, kout_ref, vout_ref,
                       q_ring, k_ring, v_ring, u_ring, vg_ring, mix_ring, *, layer, tiles_per_seq):
    tt = xa_ref.shape[1]
    g = pl.program_id(0)

    @pl.when(g == 0)
    def _():
        for ring in (q_ring, k_ring, v_ring, u_ring, vg_ring, mix_ring):
            ring[...] = jnp.zeros(ring.shape, ring.dtype)

    cur, prv = g % 2, (g + 1) % 2
    kv_a, kv_b, kv_p = g % 3, (g + 2) % 3, (g + 1) % 3

    cos_t, sin_dn, sin_up = rope_ref[0], rope_ref[1], rope_ref[2]
    env = {}

    def norm_task():
        env["hb"] = _modulated_rmsnorm(
            xa_ref[0], gattn_ref[...], moda_ref[1:2], moda_ref[0:1]).astype(BF16)
        yield

    def q_task():
        q = _dot(env["hb"], win_ref[:, :ATTN_WIDTH])
        yield
        ssq = _dot((q * q).astype(BF16), bdq_ref[...])
        yield
        q = q * lax.rsqrt(ssq * (1.0 / HEAD_DIM) + EPS) * gq_ref[...]
        q_ring[cur] = (_rope(q, cos_t, sin_dn, sin_up) * (HEAD_DIM ** -0.5)).astype(BF16)

    def kv_task():
        kv = _dot(env["hb"], win_ref[:, ATTN_WIDTH:ATTN_WIDTH + 2 * KV_WIDTH])
        yield
        k, v = kv[:, :KV_WIDTH], kv[:, KV_WIDTH:]
        ssq = _dot((k * k).astype(BF16), bdk_ref[...])
        vout_ref[0] = v[tt - WINDOW:, :]
        for i, vz in enumerate(_split_heads(v)):
            v_ring[kv_a, i] = vz.astype(BF16)
        yield
        k = _rope(k * lax.rsqrt(ssq * (1.0 / HEAD_DIM) + EPS) * gk_ref[...], cos_t, sin_dn, sin_up)
        kout_ref[0] = k[tt - WINDOW:, :]
        for i, kz in enumerate(_split_heads(k)):
            k_ring[kv_a, i] = kz.astype(BF16)

    def u_task():
        zu = _dot(env["hb"], win_ref[:, ATTN_WIDTH + 2 * KV_WIDTH:IN_WIDTH - GMLP_WIDTH])
        yield
        u_ring[cur] = jax.nn.gelu(zu)

    def vg_task():
        zv = _dot(env["hb"], win_ref[:, IN_WIDTH - GMLP_WIDTH:])
        yield
        vg_ring[cur] = _gelu_layernorm(zv, lng_ref[...], lnb_ref[...]).astype(BF16)

    row = lax.broadcasted_iota(jnp.int32, (WINDOW, 2 * WINDOW), 0)
    col = lax.broadcasted_iota(jnp.int32, (WINDOW, 2 * WINDOW), 1)
    band = (col - row >= 1) & (col - row <= WINDOW)
    seq_start = (g + tiles_per_seq - 1) % tiles_per_seq == 0
    band_first = band & (col >= jnp.where(seq_start, WINDOW, 0))
    tail = slice(tt - WINDOW, tt)

    def band_rows(ring, idx, i):
        if i == 0:
            return [ring[kv_p, idx, tail, :], ring[kv_b, idx, 0:WINDOW, :]]
        return [ring[kv_b, idx, (i - 1) * WINDOW:(i + 1) * WINDOW, :]]

    def attn_task(i, p):
        mask = band_first if i == 0 else band
        rows = slice(i * WINDOW, (i + 1) * WINDOW)
        kv = p // 2
        kblk = jnp.concatenate(band_rows(k_ring, 2 * kv, i) + band_rows(k_ring, 2 * kv + 1, i), axis=0)
        s = _dot_nt(q_ring[prv, rows, p * LANES:(p + 1) * LANES], kblk)
        yield
        es, invs = [], []
        for hh in range(2):
            sink = sinks_ref[layer, 2 * p + hh]
            sh_ = jnp.where(mask, s[:, hh * 2 * WINDOW:(hh + 1) * 2 * WINDOW], NEG)
            m = jnp.maximum(jnp.max(sh_, axis=-1, keepdims=True), sink)
            e = jnp.exp(sh_ - m)
            es.append(e)
            invs.append(1.0 / (jnp.sum(e, axis=-1, keepdims=True) + jnp.exp(sink - m)))
        vblk = jnp.concatenate(band_rows(v_ring, 2 * kv, i) + band_rows(v_ring, 2 * kv + 1, i), axis=0)
        o = _dot(jnp.concatenate(es, axis=1).astype(BF16), vblk)
        yield
        o = o * _lane_pair(invs[0], invs[1], o.shape)
        mix_ring[cur, rows, p * LANES:(p + 1) * LANES] = o.astype(BF16)

    def gate_task(c, grp):
        rs = slice(c * CHUNK, (c + 1) * CHUNK)
        cs = slice(grp * GMLP_GW, (grp + 1) * GMLP_GW)
        z = _dot(ws_ref[grp], vg_ring[prv, rs, cs])
        yield
        z = z + bs_ref[:, grp:grp + 1]
        mix_ring[cur, rs, ATTN_WIDTH + grp * GMLP_GW:ATTN_WIDTH + (grp + 1) * GMLP_GW] = (
            u_ring[prv, rs, cs] * z).astype(BF16)

    def out_task(c):
        cols = slice(c * OUT_COLS, (c + 1) * OUT_COLS)
        r = _dot(mix_ring[prv], wout_ref[:, cols])
        yield
        y_ref[0, :, cols] = xc_ref[0, :, cols] + modc_ref[2:3, cols] * r

    n_blocks = tt // WINDOW
    attn = [attn_task(i, p) for i in range(n_blocks) for p in range(N_PAIRS)]
    gate = [gate_task(c, grp) for c in range(n_blocks) for grp in range(GMLP_GROUPS)]
    big = [out_task(0), norm_task(), vg_task(), out_task(1), u_task(), q_task(), out_task(2),
           kv_task(), out_task(3)]
    small = [t for pair in zip(attn, gate) for t in pair]
    per_big = -(-len(small) // len(big))
    order = []
    for b in big:
        order.append(b)
        order.extend(small[:per_big])
        small = small[per_big:]
    _interleave(order + small, MIX_INTERLEAVE)


def _mix_weight_specs(layer, gate_kind):
    return [
        _layer_spec(layer, (1, D_MODEL)),
        _layer_spec(layer, (D_MODEL, IN_WIDTH)),
        _layer_spec(layer, (1, ATTN_WIDTH)),
        _layer_spec(layer, (1, KV_WIDTH)),
        _const_spec((ATTN_WIDTH, ATTN_WIDTH)),
        _const_spec((KV_WIDTH, KV_WIDTH)),
    ], [
        pl.BlockSpec(memory_space=pltpu.SMEM),
        _layer_spec(layer, (1, GMLP_WIDTH)),
        _layer_spec(layer, (1, GMLP_WIDTH)),
        _layer_spec(layer, (GMLP_GROUPS, CHUNK, CHUNK)),
        _layer_spec(layer, (CHUNK, GMLP_GROUPS)),
        _layer_spec(layer, (D_MODEL, D_MODEL)),
    ]


def _mix_weight_args(pp, kind):
    return ([pp["g_attn"], pp["w_in"], pp["g_q"], pp["g_k"], pp["bd_q"], pp["bd_k"]],
            [pp["sinks"], pp["ln_g"], pp["ln_b"], pp["ws_" + kind], pp["bs_" + kind], pp["w_out"]])


def _mix_prompt_call(layer, x, mod4, n_skip, pp, rope, prev):
    b, t, _ = x.shape
    tt = PROMPT_TILE
    nt = t // tt
    last = b * nt - 1
    w_specs_a, w_specs_b = _mix_weight_specs(layer, "prompt")
    w_args_a, w_args_b = _mix_weight_args(pp, "prompt")

    def tile(g, lag):
        return jnp.clip(g - lag, 0, last)

    def x_spec(lag):
        return pl.BlockSpec((1, tt, D_MODEL), lambda g: (tile(g, lag) // nt, tile(g, lag) % nt, 0))

    def mod_spec(lag):
        return pl.BlockSpec((None, None, N_MOD, D_MODEL),
                            lambda g: (layer, n_skip + tile(g, lag) // nt, 0, 0))

    kv_spec = pl.BlockSpec((None, 1, WINDOW, KV_WIDTH), lambda g: (layer, tile(g, 0) // nt, 0, 0))
    kv_shape = jax.ShapeDtypeStruct((DEPTH, b, WINDOW, KV_WIDTH), F32)
    return _layer_call(
        functools.partial(_mix_prompt_kernel, layer=layer, tiles_per_seq=nt), layer, prev,
        grid=(b * nt + 2,),
        in_specs=[x_spec(0), x_spec(2), mod_spec(0), mod_spec(2)] + w_specs_a
        + [pl.BlockSpec((3, tt, LANES), lambda g: (0, tile(g, 0) % nt, 0))] + w_specs_b,
        args=[x, x, mod4, mod4] + w_args_a + [rope] + w_args_b,
        out_specs=[x_spec(2), kv_spec, kv_spec],
        out_shapes=[jax.ShapeDtypeStruct((b, t, D_MODEL), F32), kv_shape, kv_shape],
        scratch=[
            pltpu.VMEM((2, tt, ATTN_WIDTH), BF16),
            pltpu.VMEM((3, 4, tt, LANES), BF16),
            pltpu.VMEM((3, 4, tt, LANES), BF16),
            pltpu.VMEM((2, tt, GMLP_WIDTH), F32),
            pltpu.VMEM((2, tt, GMLP_WIDTH), BF16),
            pltpu.VMEM((2, tt, D_MODEL), BF16),
        ],
        name="mix_prompt")


def _mix_sample_kernel(x_ref, sh_ref, sc_ref, ga_ref, ck_ref, cv_ref, gattn_ref, win_ref, gq_ref,
                       gk_ref, bdq_ref, bdk_ref, rope_ref, sinks_ref, lng_ref, lnb_ref, ws_ref,
                       bs_ref, wout_ref, y_ref, kout_ref, vout_ref, vgout_ref,
                       mod_scr, q_scr, s_scr, e_scr, o_scr, mix_ref, *, layer):
    n = x_ref.shape[0]
    sb = sh_ref.shape[0]
    ts = n // sb
    _expand_rows(mod_scr.at[0], sh_ref[...], ts)
    _expand_rows(mod_scr.at[1], sc_ref[...], ts)
    x = x_ref[...]
    h = _modulated_rmsnorm(x, gattn_ref[...], _read_cols(mod_scr.at[1]), _read_cols(mod_scr.at[0]))
    proj = _dot(h.astype(BF16), win_ref[...])
    _expand_rows(mod_scr.at[0], ga_ref[...], ts)

    cos_t, sin_dn, sin_up = rope_ref[0], rope_ref[1], rope_ref[2]
    q = _rope(_head_rmsnorm(proj[:, :ATTN_WIDTH], bdq_ref[...], gq_ref[...]), cos_t, sin_dn, sin_up)
    q = q * (HEAD_DIM ** -0.5)
    q_scr[...] = q
    qb = q.astype(BF16)
    k = _rope(_head_rmsnorm(proj[:, ATTN_WIDTH:ATTN_WIDTH + KV_WIDTH], bdk_ref[...], gk_ref[...]),
              cos_t, sin_dn, sin_up)
    v = proj[:, ATTN_WIDTH + KV_WIDTH:ATTN_WIDTH + 2 * KV_WIDTH]
    kout_ref[...] = k
    vout_ref[...] = v

    def cache_scores(b, carry):
        rows = pl.ds(pl.multiple_of(b * ts, ts), ts)
        kparts = _split_heads(ck_ref[b])
        for kv in range(N_KV_HEADS):
            kblk = jnp.concatenate([kparts[2 * kv], kparts[2 * kv + 1]], axis=0).astype(BF16)
            q16 = jnp.concatenate(
                [q_scr[rows, (2 * kv) * LANES:(2 * kv + 1) * LANES],
                 q_scr[rows, (2 * kv + 1) * LANES:(2 * kv + 2) * LANES]], axis=0).astype(BF16)
            s = _dot_nt(q16, kblk)
            s_scr[2 * kv, rows, :] = s[:ts]
            s_scr[2 * kv + 1, rows, :] = s[ts:]
        return carry

    lax.fori_loop(0, sb, cache_scores, 0)

    knew = [z.astype(BF16) for z in _split_heads(k)]
    vnew = [z.astype(BF16) for z in _split_heads(v)]
    grp = CHUNK
    rown = lax.broadcasted_iota(jnp.int32, (grp, grp), 0)
    coln = lax.broadcasted_iota(jnp.int32, (grp, grp), 1)
    seq_bits = ts.bit_length() - 1
    mask_new = ((rown >> seq_bits) == (coln >> seq_bits)) & (coln <= rown)
    mask_cache = (lax.broadcasted_iota(jnp.int32, (grp, WINDOW), 1)
                  > (lax.broadcasted_iota(jnp.int32, (grp, WINDOW), 0) & (ts - 1)))

    for p in range(N_PAIRS):
        kv = p // 2
        for g0 in range(0, n, grp):
            rows = slice(g0, g0 + grp)
            kblk = jnp.concatenate([knew[2 * kv][rows], knew[2 * kv + 1][rows]], axis=0)
            vblk = jnp.concatenate([vnew[2 * kv][rows], vnew[2 * kv + 1][rows]], axis=0)
            s_new = _dot_nt(qb[rows, p * LANES:(p + 1) * LANES], kblk)
            e_new, invs = [], []
            for hh in range(2):
                sink = sinks_ref[layer, 2 * p + hh]
                sn = jnp.where(mask_new, s_new[:, hh * grp:(hh + 1) * grp], NEG)
                sc_ = jnp.where(mask_cache, s_scr[p, rows, hh * WINDOW:(hh + 1) * WINDOW], NEG)
                m = jnp.maximum(jnp.maximum(jnp.max(sn, axis=-1, keepdims=True),
                                            jnp.max(sc_, axis=-1, keepdims=True)), sink)
                en = jnp.exp(sn - m)
                ec = jnp.exp(sc_ - m)
                den = (jnp.sum(en, axis=-1, keepdims=True) + jnp.sum(ec, axis=-1, keepdims=True)
                       + jnp.exp(sink - m))
                e_new.append(en)
                invs.append(1.0 / den)
                e_scr[p, rows, hh * WINDOW:(hh + 1) * WINDOW] = ec
            o_scr[p, rows, :] = _dot(jnp.concatenate(e_new, axis=1).astype(BF16), vblk)
            o_scr[N_PAIRS + p, rows, :] = _lane_pair(invs[0], invs[1], (grp, LANES))

    def cache_pv(b, carry):
        rows = pl.ds(pl.multiple_of(b * ts, ts), ts)
        vparts = _split_heads(cv_ref[b])
        for kv in range(N_KV_HEADS):
            vblk = jnp.concatenate([vparts[2 * kv], vparts[2 * kv + 1]], axis=0).astype(BF16)
            e16 = jnp.concatenate([e_scr[2 * kv, rows, :], e_scr[2 * kv + 1, rows, :]],
                                  axis=0).astype(BF16)
            o = _dot(e16, vblk)
            o_scr[2 * kv, rows, :] += o[:ts]
            o_scr[2 * kv + 1, rows, :] += o[ts:]
        return carry

    lax.fori_loop(0, sb, cache_pv, 0)

    for p in range(N_PAIRS):
        mix_ref[:, p * LANES:(p + 1) * LANES] = (o_scr[p] * o_scr[N_PAIRS + p]).astype(BF16)

    u = jax.nn.gelu(proj[:, ATTN_WIDTH + 2 * KV_WIDTH:ATTN_WIDTH + 2 * KV_WIDTH + GMLP_WIDTH])
    vg = _gelu_layernorm(proj[:, IN_WIDTH - GMLP_WIDTH:], lng_ref[...], lnb_ref[...])
    vgout_ref[...] = vg
    _spatial_gate_into(mix_ref, u, vg, ws_ref, bs_ref)

    y_ref[...] = x + _read_cols(mod_scr.at[0]) * _dot(mix_ref[...], wout_ref[...])


def _sample_mod_specs(layer, sb, first):
    return [pl.BlockSpec((None, sb, D_MODEL), functools.partial(lambda i, w: (layer, i, w), w=first + m))
            for m in range(3)]


def _mix_sample_call(layer, x, mod3, cache_k, cache_v, pp, rope, ts, prev):
    nt = x.shape[0]
    sb = SAMPLE_SEQS
    n = sb * ts
    w_specs_a, w_specs_b = _mix_weight_specs(layer, "sample")
    w_args_a, w_args_b = _mix_weight_args(pp, "sample")
    tok_spec = lambda w: pl.BlockSpec((None, n, w), lambda i: (layer, i, 0))
    tok_shape = lambda w: jax.ShapeDtypeStruct((DEPTH, nt, w), F32)
    cache_spec = pl.BlockSpec((None, sb, WINDOW, KV_WIDTH), lambda i: (layer, i, 0, 0))
    return _layer_call(
        functools.partial(_mix_sample_kernel, layer=layer), layer, prev,
        grid=(nt // n,),
        in_specs=[pl.BlockSpec((n, D_MODEL), lambda i: (i, 0))] + _sample_mod_specs(layer, sb, 0)
        + [cache_spec, cache_spec] + w_specs_a + [_const_spec((3, n, LANES))] + w_specs_b,
        args=[x, mod3, mod3, mod3, cache_k, cache_v] + w_args_a + [rope] + w_args_b,
        out_specs=[pl.BlockSpec((n, D_MODEL), lambda i: (i, 0)),
                   tok_spec(KV_WIDTH), tok_spec(KV_WIDTH), tok_spec(GMLP_WIDTH)],
        out_shapes=[jax.ShapeDtypeStruct((nt, D_MODEL), F32),
                    tok_shape(KV_WIDTH), tok_shape(KV_WIDTH), tok_shape(GMLP_WIDTH)],
        scratch=[
            pltpu.VMEM((2, D_MODEL // LANES, n, LANES), F32),
            pltpu.VMEM((n, ATTN_WIDTH), F32),
            pltpu.VMEM((N_PAIRS, n, 2 * WINDOW), F32),
            pltpu.VMEM((N_PAIRS, n, 2 * WINDOW), F32),
            pltpu.VMEM((2 * N_PAIRS, n, LANES), F32),
            pltpu.VMEM((n, D_MODEL), BF16),
        ],
        name="mix_sample")


def _shifted_rows(hu, before):
    rows, c = hu.shape
    hu3 = hu.reshape(rows // SUBLANES, SUBLANES, c)
    above = jnp.concatenate([before[None], hu3[:-1]], axis=0)
    sub = lax.broadcasted_iota(jnp.int32, (1, SUBLANES, c), 1)
    p1 = pltpu.roll(jnp.where(sub >= SUBLANES - 1, above, hu3), 1, axis=1)
    p2 = pltpu.roll(jnp.where(sub >= SUBLANES - 2, above, hu3), 2, axis=1)
    return p1.reshape(rows, c), p2.reshape(rows, c)


def _conv(hu, prev1, prev2, cw, cb):
    return cb + cw[0:1] * prev2 + cw[1:2] * prev1 + cw[2:3] * hu


def _ffn_chunks():
    return [(slice(j, j + FFN_CHUNK), slice(D_FF + j, D_FF + j + FFN_CHUNK))
            for j in range(0, D_FF, FFN_CHUNK)]


def _down_plan(n_chunks):
    bounds = list(range(0, n_chunks, DOWN_GROUP)) + [n_chunks]
    groups = [(a * FFN_CHUNK, b * FFN_CHUNK) for a, b in zip(bounds[:-1], bounds[1:])]
    return {b // FFN_CHUNK: (a, b) for a, b in groups[:-1]}, groups[-1]


def _down_into(acc_ref, act_scr, wdn_ref, a, b):
    contrib = _dot(act_scr[:, a:b], wdn_ref[a:b, :])
    if a == 0:
        acc_ref[...] = contrib
    else:
        acc_ref[...] += contrib


def _ffn_prompt_kernel(x_ref, mod_ref, gffn_ref, wup_ref, cw_ref, cb_ref, wdn_ref,
                       y_ref, cout_ref, carry, act_scr, acc_ref):
    tt = x_ref.shape[1]
    t = pl.program_id(1)

    @pl.when(t == 0)
    def _():
        carry[...] = jnp.zeros(carry.shape, F32)

    x = x_ref[0]
    hb = _modulated_rmsnorm(x, gffn_ref[...], mod_ref[4:5], mod_ref[3:4]).astype(BF16)
    chunks = _ffn_chunks()
    after, tail = _down_plan(len(chunks))
    for j, halves in enumerate(chunks):
        hus = [_dot(hb, wup_ref[:, cols]) for cols in halves]
        if j in after:
            _down_into(acc_ref, act_scr, wdn_ref, *after[j])
        convs = []
        for hu, cols in zip(hus, halves):
            prev1, prev2 = _shifted_rows(hu, carry[:, cols])
            carry[:, cols] = hu[tt - SUBLANES:, :]
            cout_ref[0, :, cols] = hu[tt - (CONV_W - 1):, :]
            convs.append(_conv(hu, prev1, prev2, cw_ref[:, cols], cb_ref[:, cols]))
        act_scr[:, j * FFN_CHUNK:(j + 1) * FFN_CHUNK] = (
            jax.nn.silu(convs[0]) * convs[1]).astype(BF16)
    _down_into(acc_ref, act_scr, wdn_ref, *tail)
    y_ref[0] = x + mod_ref[5:6] * acc_ref[...]


def _ffn_weight_specs(layer):
    return [
        _layer_spec(layer, (1, D_MODEL)),
        _layer_spec(layer, (D_MODEL, 2 * D_FF), single_buffer=True),
        _layer_spec(layer, (CONV_W, 2 * D_FF)),
        _layer_spec(layer, (1, 2 * D_FF)),
        _layer_spec(layer, (D_FF, D_MODEL), single_buffer=True),
    ]


def _ffn_weight_args(pp):
    return [pp["g_ffn"], pp["w_up"], pp["conv_w"], pp["conv_b"], pp["w_down"]]


def _ffn_prompt_call(layer, x, mod4, n_skip, pp, prev):
    b, t, _ = x.shape
    tt = FFN_TILE
    return _layer_call(
        _ffn_prompt_kernel, layer, prev,
        grid=(b, t // tt),
        in_specs=[pl.BlockSpec((1, tt, D_MODEL), lambda i, j: (i, j, 0)),
                  pl.BlockSpec((None, None, N_MOD, D_MODEL), lambda i, j: (layer, n_skip + i, 0, 0))]
        + _ffn_weight_specs(layer),
        args=[x, mod4] + _ffn_weight_args(pp),
        out_specs=[pl.BlockSpec((1, tt, D_MODEL), lambda i, j: (i, j, 0)),
                   pl.BlockSpec((None, 1, CONV_W - 1, 2 * D_FF), lambda i, j: (layer, i, 0, 0))],
        out_shapes=[jax.ShapeDtypeStruct((b, t, D_MODEL), F32),
                    jax.ShapeDtypeStruct((DEPTH, b, CONV_W - 1, 2 * D_FF), F32)],
        scratch=[
            pltpu.VMEM((SUBLANES, 2 * D_FF), F32),
            pltpu.VMEM((tt, D_FF), BF16),
            pltpu.VMEM((tt, D_MODEL), F32),
        ],
        name="ffn_prompt")


def _ffn_sample_kernel(x_ref, sh_ref, sc_ref, gf_ref, hist_ref, gffn_ref, wup_ref, cw_ref,
                       cb_ref, wdn_ref, y_ref, cout_ref, mod_scr, h1, h2, act_scr, acc_ref):
    n = x_ref.shape[0]
    sb = sh_ref.shape[0]
    ts = n // sb
    pad = SUBLANES
    _expand_rows(mod_scr.at[0], sh_ref[...], ts)
    _expand_rows(mod_scr.at[1], sc_ref[...], ts)
    x = x_ref[...]
    hb = _modulated_rmsnorm(x, gffn_ref[...], _read_cols(mod_scr.at[1]),
                            _read_cols(mod_scr.at[0])).astype(BF16)
    _expand_rows(mod_scr.at[0], gf_ref[...], ts)
    chunks = _ffn_chunks()
    after, tail = _down_plan(len(chunks))
    for j, halves in enumerate(chunks):
        hus = [_dot(hb, wup_ref[:, cols]) for cols in halves]
        if j in after:
            _down_into(acc_ref, act_scr, wdn_ref, *after[j])
        convs = []
        for hu, cols in zip(hus, halves):
            for c in range(FFN_CHUNK // LANES):
                off = cols.start + c * LANES
                piece = hu[:, c * LANES:(c + 1) * LANES]
                h1[c, pad:, :] = piece
                h2[c, pad:, :] = piece
                cout_ref[:, 0, off:off + LANES] = h2[c, pl.ds(pad + ts - 2, sb, stride=ts), :]
                cout_ref[:, 1, off:off + LANES] = h2[c, pl.ds(pad + ts - 1, sb, stride=ts), :]
                c0 = hist_ref[:, 0, off:off + LANES]
                c1 = hist_ref[:, 1, off:off + LANES]
                h1[c, pl.ds(pad - 1, sb, stride=ts), :] = c1
                h2[c, pl.ds(pad - 2, sb, stride=ts), :] = c0
                h2[c, pl.ds(pad - 1, sb, stride=ts), :] = c1
            convs.append(_conv(hu, _read_cols(h1, slice(pad - 1, pad - 1 + n)),
                               _read_cols(h2, slice(pad - 2, pad - 2 + n)),
                               cw_ref[:, cols], cb_ref[:, cols]))
        act_scr[:, j * FFN_CHUNK:(j + 1) * FFN_CHUNK] = (
            jax.nn.silu(convs[0]) * convs[1]).astype(BF16)
    _down_into(acc_ref, act_scr, wdn_ref, *tail)
    y_ref[...] = x + _read_cols(mod_scr.at[0]) * acc_ref[...]


def _ffn_sample_call(layer, x, mod3, cconv, pp, ts, prev):
    nt = x.shape[0]
    sb = SAMPLE_SEQS
    n = sb * ts
    tok_spec = pl.BlockSpec((n, D_MODEL), lambda i: (i, 0))
    hist_spec = pl.BlockSpec((None, sb, CONV_W - 1, 2 * D_FF), lambda i: (layer, i, 0, 0))
    return _layer_call(
        _ffn_sample_kernel, layer, prev,
        grid=(nt // n,),
        in_specs=[tok_spec] + _sample_mod_specs(layer, sb, 3) + [hist_spec]
        + _ffn_weight_specs(layer),
        args=[x, mod3, mod3, mod3, cconv] + _ffn_weight_args(pp),
        out_specs=[tok_spec, hist_spec],
        out_shapes=[jax.ShapeDtypeStruct((nt, D_MODEL), F32),
                    jax.ShapeDtypeStruct(cconv.shape, F32)],
        scratch=[
            pltpu.VMEM((2, D_MODEL // LANES, n, LANES), F32),
            pltpu.VMEM((FFN_CHUNK // LANES, n + SUBLANES, LANES), F32),
            pltpu.VMEM((FFN_CHUNK // LANES, n + SUBLANES, LANES), F32),
            pltpu.VMEM((n, D_FF), BF16),
            pltpu.VMEM((n, D_MODEL), F32),
        ],
        name="ffn_sample")


def _rope_tables(pos):
    half = ROPE_DIMS // 2
    inv = ROPE_THETA ** (-jnp.arange(0, ROPE_DIMS, 2, dtype=F32) / ROPE_DIMS)
    ang = pos.astype(F32)[:, None] * inv[None, :]
    cos, sin = jnp.cos(ang), jnp.sin(ang)
    n = pos.shape[0]
    rest = jnp.zeros((n, HEAD_DIM - ROPE_DIMS), F32)
    zeros = jnp.zeros((n, half), F32)
    cos_t = jnp.concatenate([cos, cos, rest + 1.0], axis=1)
    sin_dn = jnp.concatenate([-sin, zeros, rest], axis=1)
    sin_up = jnp.concatenate([zeros, sin, rest], axis=1)
    return jnp.stack([jnp.tile(z, (1, LANES // HEAD_DIM)) for z in (cos_t, sin_dn, sin_up)])


def _block_diag_ones(width):
    idx = jnp.arange(width) // HEAD_DIM
    return (idx[:, None] == idx[None, :]).astype(BF16)


def _prepare_params(dec_seq, g_attn, w_in, g_q, g_k, sinks, ln_g, ln_b, w_s, b_s, w_out,
                    g_ffn, w_ffn_in, conv_w, conv_b, w_ffn_out):
    causal = jnp.tril(jnp.ones((CHUNK, CHUNK), dtype=bool))
    ws = jnp.where(causal, w_s, 0.0)
    seqs_per_chunk = CHUNK // dec_seq
    eye = jnp.eye(seqs_per_chunk, dtype=F32)
    ws_sample = jnp.einsum("ab,lgts->lgatbs", eye, ws[:, :, :dec_seq, :dec_seq]).reshape(w_s.shape)
    return {
        "g_attn": g_attn[:, None, :], "g_ffn": g_ffn[:, None, :],
        "w_in": w_in.astype(BF16), "w_out": w_out.astype(BF16),
        "g_q": jnp.tile(g_q, (1, N_HEADS))[:, None, :],
        "g_k": jnp.tile(g_k, (1, N_KV_HEADS))[:, None, :],
        "bd_q": _block_diag_ones(ATTN_WIDTH), "bd_k": _block_diag_ones(KV_WIDTH),
        "sinks": sinks,
        "ln_g": ln_g.reshape(DEPTH, 1, GMLP_WIDTH), "ln_b": ln_b.reshape(DEPTH, 1, GMLP_WIDTH),
        "ws_prompt": ws.astype(BF16), "bs_prompt": jnp.swapaxes(b_s, 1, 2),
        "ws_sample": ws_sample.astype(BF16),
        "bs_sample": jnp.swapaxes(jnp.tile(b_s[:, :, :dec_seq], (1, 1, seqs_per_chunk)), 1, 2),
        "w_up": w_ffn_in.astype(BF16), "w_down": w_ffn_out.astype(BF16),
        "conv_w": conv_w, "conv_b": conv_b[:, None, :],
    }


def kernel(x_prompt, x_sample, cache_k, cache_v, cache_conv, c_prompt, c_sample, w_ada, b_ada,
           g_attn, w_in, g_q, g_k, sinks, ln_g, ln_b, w_s, b_s, w_out, g_ffn, w_ffn_in, conv_w,
           conv_b, w_ffn_out):
    nbp, seq, _ = x_prompt.shape
    nbs, dec_seq, _ = x_sample.shape
    assert seq % PROMPT_TILE == 0 and PROMPT_TILE % CHUNK == 0 and nbs % SAMPLE_SEQS == 0
    assert seq % FFN_TILE == 0 and FFN_TILE % SUBLANES == 0 and D_FF % FFN_CHUNK == 0
    assert dec_seq == SUBLANES and CHUNK % dec_seq == 0 and cache_k.shape[2] == WINDOW
    assert nbs % SUBLANES == 0

    n_c = nbs + nbp
    c_all = jnp.concatenate([c_sample, c_prompt, jnp.zeros((-n_c % SUBLANES, D_MODEL), F32)])
    mod3 = _ada_call(c_all, w_ada, b_ada)
    mod4 = mod3.reshape(DEPTH, c_all.shape[0], N_MOD, D_MODEL)

    pp = _prepare_params(dec_seq, g_attn, w_in, g_q, g_k, sinks, ln_g, ln_b, w_s, b_s, w_out,
                         g_ffn, w_ffn_in, conv_w, conv_b, w_ffn_out)
    rope_p = _rope_tables(jnp.arange(seq, dtype=jnp.int32))
    rope_s = jnp.tile(_rope_tables(PAST_LEN + jnp.arange(dec_seq, dtype=jnp.int32)),
                      (1, SAMPLE_SEQS, 1))
    ck = cache_k.reshape(DEPTH, nbs, WINDOW, KV_WIDTH)
    cv = cache_v.reshape(DEPTH, nbs, WINDOW, KV_WIDTH)

    xp = x_prompt
    xs = x_sample.reshape(nbs * dec_seq, D_MODEL)
    mix_p = ffn_p = mix_s = ffn_s = None
    for l in range(DEPTH):
        xp, *mix_p = _mix_prompt_call(l, xp, mod4, nbs, pp, rope_p, mix_p)
        xp, *ffn_p = _ffn_prompt_call(l, xp, mod4, nbs, pp, ffn_p)
        xs, *mix_s = _mix_sample_call(l, xs, mod3, ck, cv, pp, rope_s, dec_seq, mix_s)
        xs, *ffn_s = _ffn_sample_call(l, xs, mod3, cache_conv, pp, dec_seq, ffn_s)

    kv_p = (DEPTH, nbp, WINDOW, N_KV_HEADS, HEAD_DIM)
    kv_s = (DEPTH, nbs, dec_seq, N_KV_HEADS, HEAD_DIM)
    return (xp, xs.reshape(nbs, dec_seq, D_MODEL),
            mix_p[0].reshape(kv_p), mix_p[1].reshape(kv_p), ffn_p[0],
            mix_s[0].reshape(kv_s), mix_s[1].reshape(kv_s),
            mix_s[2].reshape(DEPTH, nbs, dec_seq, GMLP_WIDTH),
            ffn_s[0])
```

```python
import functools

import jax
import jax.numpy as jnp
from jax import lax
from jax.experimental import pallas as pl
from jax.experimental.pallas import tpu as pltpu

D_MODEL = 1024
DEPTH = 2
HEAD_DIM = 64
N_HEADS = 8
N_KV_HEADS = 2
KV_WIDTH = N_KV_HEADS * HEAD_DIM
WINDOW = 128
ROPE_THETA = 500000.0
ROPE_DIMS = HEAD_DIM // 4
ATTN_WIDTH = N_HEADS * HEAD_DIM
GMLP_WIDTH = D_MODEL - ATTN_WIDTH
GMLP_GROUPS = 4
GMLP_GW = GMLP_WIDTH // GMLP_GROUPS
CHUNK = 128
IN_WIDTH = ATTN_WIDTH + 2 * KV_WIDTH + 2 * GMLP_WIDTH
D_FF = 2816
CONV_W = 3
N_MOD = 6
EPS = 1e-6
NEG = -1e30
PAST_LEN = 16384

LANES = 128
SUBLANES = 8
N_PAIRS = N_HEADS // 2
VMEM_LIMIT = 56 * 1024 * 1024

PROMPT_TILE = 512
OUT_COLS = 256
MIX_INTERLEAVE = 10
FFN_TILE = 1024
SAMPLE_SEQS = 64
SEQ_UNROLL = 8
FFN_CHUNK = 256
DOWN_GROUP = 2
ADA_TILE = 1536

F32 = jnp.float32
BF16 = jnp.bfloat16


def _dot(a, b):
    return jnp.dot(a, b, preferred_element_type=F32)


def _dot_nt(a, b):
    return lax.dot_general(a, b, (((1,), (1,)), ((), ())), preferred_element_type=F32)


def _const_spec(shape):
    nd = len(shape)
    return pl.BlockSpec(shape, lambda *_: (0,) * nd)


def _layer_spec(layer, tail, single_buffer=False):
    nd = len(tail)
    kw = {"pipeline_mode": pl.Buffered(1)} if single_buffer else {}
    return pl.BlockSpec((None,) + tuple(tail), lambda *_: (layer,) + (0,) * nd, **kw)


def _skip_aliased(body, n_in, n_alias):
    if n_alias == 0:
        return body
    return lambda *refs: body(*refs[:n_in], *refs[n_in + n_alias:])


def _layer_call(body, layer, prev, *, grid, in_specs, args, out_specs, out_shapes, scratch, name):
    n_in = len(args)
    aliased = [] if prev is None else list(prev)
    return pl.pallas_call(
        _skip_aliased(body, n_in, len(aliased)),
        grid=grid,
        in_specs=list(in_specs) + [pl.BlockSpec(memory_space=pl.ANY)] * len(aliased),
        out_specs=out_specs,
        out_shape=out_shapes,
        input_output_aliases={n_in + i: 1 + i for i in range(len(aliased))},
        scratch_shapes=scratch,
        compiler_params=pltpu.CompilerParams(
            dimension_semantics=("arbitrary",) * len(grid), vmem_limit_bytes=VMEM_LIMIT),
        name=name,
    )(*args, *aliased)


def _ada_kernel(c_ref, w_ref, b_ref, o_ref):
    c = c_ref[...]
    a = (c * jax.nn.sigmoid(c)).astype(BF16)
    o_ref[0] = _dot(a, w_ref[0].astype(BF16)) + b_ref[0]


def _ada_call(c_all, w_ada, b_ada):
    n = c_all.shape[0]
    return pl.pallas_call(
        _ada_kernel,
        grid=(DEPTH, N_MOD * D_MODEL // ADA_TILE),
        in_specs=[
            pl.BlockSpec((n, D_MODEL), lambda l, j: (0, 0)),
            pl.BlockSpec((1, D_MODEL, ADA_TILE), lambda l, j: (l, 0, j)),
            pl.BlockSpec((1, 1, ADA_TILE), lambda l, j: (l, 0, j)),
        ],
        out_specs=pl.BlockSpec((1, n, ADA_TILE), lambda l, j: (l, 0, j)),
        out_shape=jax.ShapeDtypeStruct((DEPTH, n, N_MOD * D_MODEL), F32),
        compiler_params=pltpu.CompilerParams(
            dimension_semantics=("arbitrary", "arbitrary"), vmem_limit_bytes=VMEM_LIMIT),
        name="ada_mod",
    )(c_all, w_ada, b_ada.reshape(DEPTH, 1, N_MOD * D_MODEL))


def _modulated_rmsnorm(x, gain, scale, shift):
    r = lax.rsqrt(jnp.mean(x * x, axis=-1, keepdims=True) + EPS)
    return (x * r) * (gain * (1.0 + scale)) + shift


def _head_rmsnorm(z, ones_bd, gain):
    ssq = _dot((z * z).astype(BF16), ones_bd)
    return z * lax.rsqrt(ssq * (1.0 / HEAD_DIM) + EPS) * gain


def _rope(z, cos_t, sin_dn, sin_up):
    half = ROPE_DIMS // 2
    cols = []
    for p in range(z.shape[-1] // LANES):
        zp = z[:, p * LANES:(p + 1) * LANES]
        cols.append(zp * cos_t
                    + pltpu.roll(zp, LANES - half, axis=1) * sin_dn
                    + pltpu.roll(zp, half, axis=1) * sin_up)
    return cols[0] if len(cols) == 1 else jnp.concatenate(cols, axis=1)


def _split_heads(z):
    lane = lax.broadcasted_iota(jnp.int32, z.shape, 1)
    lo = lane < HEAD_DIM
    zs = pltpu.roll(z, HEAD_DIM, axis=1)
    zero = jnp.zeros_like(z)
    return (jnp.where(lo, z, zero), jnp.where(lo, zero, zs),
            jnp.where(lo, zs, zero), jnp.where(lo, zero, z))


def _gelu_layernorm(zv, ln_g, ln_b):
    gv = jax.nn.gelu(zv)
    cols = []
    for g in range(GMLP_GROUPS):
        xg = gv[:, g * GMLP_GW:(g + 1) * GMLP_GW]
        mu = jnp.mean(xg, axis=-1, keepdims=True)
        xc = xg - mu
        var = jnp.mean(xc * xc, axis=-1, keepdims=True)
        cols.append(xc * lax.rsqrt(var + EPS))
    return jnp.concatenate(cols, axis=1) * ln_g + ln_b


def _spatial_gate_into(mix_ref, u, vg, ws_ref, bs_ref):
    vgb = vg.astype(BF16)
    for g in range(GMLP_GROUPS):
        w = ws_ref[g]
        bias = bs_ref[:, g:g + 1]
        cs = slice(g * GMLP_GW, (g + 1) * GMLP_GW)
        for c in range(u.shape[0] // CHUNK):
            rs = slice(c * CHUNK, (c + 1) * CHUNK)
            z = _dot(w, vgb[rs, cs]) + bias
            mix_ref[rs, ATTN_WIDTH + g * GMLP_GW:ATTN_WIDTH + (g + 1) * GMLP_GW] = (
                u[rs, cs] * z).astype(BF16)


def _lane_pair(a, b, shape):
    lane = lax.broadcasted_iota(jnp.int32, shape, 1)
    return jnp.where(lane < HEAD_DIM, a, b)


def _expand_rows(dst_ref, src, reps):
    for c in range(dst_ref.shape[0]):
        piece = src[:, c * LANES:(c + 1) * LANES]
        for t in range(reps):
            dst_ref[c, pl.ds(t, src.shape[0], stride=reps), :] = piece


def _read_cols(ref, rows=slice(None)):
    return jnp.concatenate([ref[c, rows, :] for c in range(ref.shape[0])], axis=1)


def _interleave(tasks, width):
    pending = iter(tasks)
    active = []
    while True:
        while len(active) < width:
            task = next(pending, None)
            if task is None:
                break
            active.append(task)
        if not active:
            return
        for task in list(active):
            if next(task, "done") == "done":
                active.remove(task)


def _mix_prompt_kernel(xa_ref, xc_ref, moda_ref, modc_ref, gattn_ref, win_ref, gq_ref, gk_ref,
                       bdq_ref, bdk_ref, rope_ref, sinks_ref, lng_ref, lnb_ref, ws_ref, bs_ref,
                       wout_ref, y_ref, kout_ref, vout_ref,
                       q_ring, k_ring, v_ring, u_ring, vg_ring, mix_ring, *, layer, tiles_per_seq):
    tt = xa_ref.shape[1]
    g = pl.program_id(0)

    @pl.when(g == 0)
    def _():
        for ring in (q_ring, k_ring, v_ring, u_ring, vg_ring, mix_ring):
            ring[...] = jnp.zeros(ring.shape, ring.dtype)

    cur, prv = g % 2, (g + 1) % 2
    kv_a, kv_b, kv_p = g % 3, (g + 2) % 3, (g + 1) % 3

    cos_t, sin_dn, sin_up = rope_ref[0], rope_ref[1], rope_ref[2]
    env = {}

    def norm_task():
        env["hb"] = _modulated_rmsnorm(
            xa_ref[0], gattn_ref[...], moda_ref[1:2], moda_ref[0:1]).astype(BF16)
        yield

    def q_task():
        q = _dot(env["hb"], win_ref[:, :ATTN_WIDTH])
        yield
        ssq = _dot((q * q).astype(BF16), bdq_ref[...])
        yield
        q = q * lax.rsqrt(ssq * (1.0 / HEAD_DIM) + EPS) * gq_ref[...]
        q_ring[cur] = (_rope(q, cos_t, sin_dn, sin_up) * (HEAD_DIM ** -0.5)).astype(BF16)

    def kv_task():
        kv = _dot(env["hb"], win_ref[:, ATTN_WIDTH:ATTN_WIDTH + 2 * KV_WIDTH])
        yield
        k, v = kv[:, :KV_WIDTH], kv[:, KV_WIDTH:]
        ssq = _dot((k * k).astype(BF16), bdk_ref[...])
        vout_ref[0] = v[tt - WINDOW:, :]
        for i, vz in enumerate(_split_heads(v)):
            v_ring[kv_a, i] = vz.astype(BF16)
        yield
        k = _rope(k * lax.rsqrt(ssq * (1.0 / HEAD_DIM) + EPS) * gk_ref[...], cos_t, sin_dn, sin_up)
        kout_ref[0] = k[tt - WINDOW:, :]
        for i, kz in enumerate(_split_heads(k)):
            k_ring[kv_a, i] = kz.astype(BF16)

    def u_task():
        zu = _dot(env["hb"], win_ref[:, ATTN_WIDTH + 2 * KV_WIDTH:IN_WIDTH - GMLP_WIDTH])
        yield
        u_ring[cur] = jax.nn.gelu(zu)

    def vg_task():
        zv = _dot(env["hb"], win_ref[:, IN_WIDTH - GMLP_WIDTH:])
        yield
        vg_ring[cur] = _gelu_layernorm(zv, lng_ref[...], lnb_ref[...]).astype(BF16)

    row = lax.broadcasted_iota(jnp.int32, (WINDOW, 2 * WINDOW), 0)
    col = lax.broadcasted_iota(jnp.int32, (WINDOW, 2 * WINDOW), 1)
    band = (col - row >= 1) & (col - row <= WINDOW)
    seq_start = (g + tiles_per_seq - 1) % tiles_per_seq == 0
    band_first = band & (col >= jnp.where(seq_start, WINDOW, 0))
    tail = slice(tt - WINDOW, tt)

    def band_rows(ring, idx, i):
        if i == 0:
            return [ring[kv_p, idx, tail, :], ring[kv_b, idx, 0:WINDOW, :]]
        return [ring[kv_b, idx, (i - 1) * WINDOW:(i + 1) * WINDOW, :]]

    def attn_task(i, p):
        mask = band_first if i == 0 else band
        rows = slice(i * WINDOW, (i + 1) * WINDOW)
        kv = p // 2
        kblk = jnp.concatenate(band_rows(k_ring, 2 * kv, i) + band_rows(k_ring, 2 * kv + 1, i), axis=0)
        s = _dot_nt(q_ring[prv, rows, p * LANES:(p + 1) * LANES], kblk)
        yield
        es, invs = [], []
        for hh in range(2):
            sink = sinks_ref[layer, 2 * p + hh]
            sh_ = jnp.where(mask, s[:, hh * 2 * WINDOW:(hh + 1) * 2 * WINDOW], NEG)
            m = jnp.maximum(jnp.max(sh_, axis=-1, keepdims=True), sink)
            e = jnp.exp(sh_ - m)
            es.append(e)
            invs.append(1.0 / (jnp.sum(e, axis=-1, keepdims=True) + jnp.exp(sink - m)))
        vblk = jnp.concatenate(band_rows(v_ring, 2 * kv, i) + band_rows(v_ring, 2 * kv + 1, i), axis=0)
        o = _dot(jnp.concatenate(es, axis=1).astype(BF16), vblk)
        yield
        o = o * _lane_pair(invs[0], invs[1], o.shape)
        mix_ring[cur, rows, p * LANES:(p + 1) * LANES] = o.astype(BF16)

    def gate_task(c, grp):
        rs = slice(c * CHUNK, (c + 1) * CHUNK)
        cs = slice(grp * GMLP_GW, (grp + 1) * GMLP_GW)
        z = _dot(ws_ref[grp], vg_ring[prv, rs, cs])
        yield
        z = z + bs_ref[:, grp:grp + 1]
        mix_ring[cur, rs, ATTN_WIDTH + grp * GMLP_GW:ATTN_WIDTH + (grp + 1) * GMLP_GW] = (
            u_ring[prv, rs, cs] * z).astype(BF16)

    def out_task(c):
        cols = slice(c * OUT_COLS, (c + 1) * OUT_COLS)
        r = _dot(mix_ring[prv], wout_ref[:, cols])
        yield
        y_ref[0, :, cols] = xc_ref[0, :, cols] + modc_ref[2:3, cols] * r

    n_blocks = tt // WINDOW
    attn = [attn_task(i, p) for i in range(n_blocks) for p in range(N_PAIRS)]
    gate = [gate_task(c, grp) for c in range(n_blocks) for grp in range(GMLP_GROUPS)]
    big = [out_task(0), norm_task(), vg_task(), out_task(1), u_task(), q_task(), out_task(2),
           kv_task(), out_task(3)]
    small = [t for pair in zip(attn, gate) for t in pair]
    per_big = -(-len(small) // len(big))
    order = []
    for b in big:
        order.append(b)
        order.extend(small[:per_big])
        small = small[per_big:]
    _interleave(order + small, MIX_INTERLEAVE)


def _mix_weight_specs(layer, gate_kind):
    return [
        _layer_spec(layer, (1, D_MODEL)),
        _layer_spec(layer, (D_MODEL, IN_WIDTH), single_buffer=True),
        _layer_spec(layer, (1, ATTN_WIDTH)),
        _layer_spec(layer, (1, KV_WIDTH)),
        _const_spec((ATTN_WIDTH, ATTN_WIDTH)),
        _const_spec((KV_WIDTH, KV_WIDTH)),
    ], [
        pl.BlockSpec(memory_space=pltpu.SMEM),
        _layer_spec(layer, (1, GMLP_WIDTH)),
        _layer_spec(layer, (1, GMLP_WIDTH)),
        _layer_spec(layer, (GMLP_GROUPS, CHUNK, CHUNK)),
        _layer_spec(layer, (CHUNK, GMLP_GROUPS)),
        _layer_spec(layer, (D_MODEL, D_MODEL), single_buffer=True),
    ]


def _mix_weight_args(pp, kind):
    return ([pp["g_attn"], pp["w_in"], pp["g_q"], pp["g_k"], pp["bd_q"], pp["bd_k"]],
            [pp["sinks"], pp["ln_g"], pp["ln_b"], pp["ws_" + kind], pp["bs_" + kind], pp["w_out"]])


def _mix_prompt_call(layer, x, mod4, n_skip, pp, rope, prev):
    b, t, _ = x.shape
    tt = PROMPT_TILE
    nt = t // tt
    last = b * nt - 1
    w_specs_a, w_specs_b = _mix_weight_specs(layer, "prompt")
    w_args_a, w_args_b = _mix_weight_args(pp, "prompt")

    def tile(g, lag):
        return jnp.clip(g - lag, 0, last)

    def x_spec(lag):
        return pl.BlockSpec((1, tt, D_MODEL), lambda g: (tile(g, lag) // nt, tile(g, lag) % nt, 0))

    def mod_spec(lag):
        return pl.BlockSpec((None, None, N_MOD, D_MODEL),
                            lambda g: (layer, n_skip + tile(g, lag) // nt, 0, 0))

    kv_spec = pl.BlockSpec((None, 1, WINDOW, KV_WIDTH), lambda g: (layer, tile(g, 0) // nt, 0, 0))
    kv_shape = jax.ShapeDtypeStruct((DEPTH, b, WINDOW, KV_WIDTH), F32)
    return _layer_call(
        functools.partial(_mix_prompt_kernel, layer=layer, tiles_per_seq=nt), layer, prev,
        grid=(b * nt + 2,),
        in_specs=[x_spec(0), x_spec(2), mod_spec(0), mod_spec(2)] + w_specs_a
        + [pl.BlockSpec((3, tt, LANES), lambda g: (0, tile(g, 0) % nt, 0))] + w_specs_b,
        args=[x, x, mod4, mod4] + w_args_a + [rope] + w_args_b,
        out_specs=[x_spec(2), kv_spec, kv_spec],
        out_shapes=[jax.ShapeDtypeStruct((b, t, D_MODEL), F32), kv_shape, kv_shape],
        scratch=[
            pltpu.VMEM((2, tt, ATTN_WIDTH), BF16),
            pltpu.VMEM((3, 4, tt, LANES), BF16),
            pltpu.VMEM((3, 4, tt, LANES), BF16),
            pltpu.VMEM((2, tt, GMLP_WIDTH), F32),
            pltpu.VMEM((2, tt, GMLP_WIDTH), BF16),
            pltpu.VMEM((2, tt, D_MODEL), BF16),
        ],
        name="mix_prompt")


def _mix_sample_kernel(x_ref, sh_ref, sc_ref, ga_ref, ck_ref, cv_ref, gattn_ref, win_ref, gq_ref,
                       gk_ref, bdq_ref, bdk_ref, rope_ref, sinks_ref, lng_ref, lnb_ref, ws_ref,
                       bs_ref, wout_ref, y_ref, kout_ref, vout_ref, vgout_ref,
                       mod_scr, q_scr, s_scr, mix_ref, *, layer):
    n = x_ref.shape[0]
    sb = sh_ref.shape[0]
    ts = n // sb
    _expand_rows(mod_scr.at[0], sh_ref[...], ts)
    _expand_rows(mod_scr.at[1], sc_ref[...], ts)
    x = x_ref[...]
    h = _modulated_rmsnorm(x, gattn_ref[...], _read_cols(mod_scr.at[1]), _read_cols(mod_scr.at[0]))
    proj = _dot(h.astype(BF16), win_ref[...])
    _expand_rows(mod_scr.at[0], ga_ref[...], ts)

    cos_t, sin_dn, sin_up = rope_ref[0], rope_ref[1], rope_ref[2]
    q = _rope(_head_rmsnorm(proj[:, :ATTN_WIDTH], bdq_ref[...], gq_ref[...]), cos_t, sin_dn, sin_up)
    q = q * (HEAD_DIM ** -0.5)
    k = _rope(_head_rmsnorm(proj[:, ATTN_WIDTH:ATTN_WIDTH + KV_WIDTH], bdk_ref[...], gk_ref[...]),
              cos_t, sin_dn, sin_up)
    v = proj[:, ATTN_WIDTH + KV_WIDTH:ATTN_WIDTH + 2 * KV_WIDTH]
    kout_ref[...] = k
    vout_ref[...] = v

    lane = lax.broadcasted_iota(jnp.int32, (n, LANES), 1)
    group = N_HEADS // N_KV_HEADS
    qh = []
    for hd in range(N_HEADS):
        piece = q[:, (hd // 2) * LANES:(hd // 2 + 1) * LANES]
        if hd % 2 != hd // group:
            piece = pltpu.roll(piece, HEAD_DIM, axis=1)
        keep = (lane < HEAD_DIM) if hd // group == 0 else (lane >= HEAD_DIM)
        piece = jnp.where(keep, piece, 0.0)
        q_scr[:, hd * ts:(hd + 1) * ts, :] = piece.reshape(sb, ts, LANES)
        qh.append(piece.astype(BF16))

    def cache_scores(b, carry):
        s_scr[b] = _dot_nt(q_scr[b].astype(BF16), ck_ref[b].astype(BF16))
        return carry

    lax.fori_loop(0, sb, cache_scores, 0, unroll=SEQ_UNROLL)

    kb, vb = k.astype(BF16), v.astype(BF16)
    grp = CHUNK
    rown = lax.broadcasted_iota(jnp.int32, (grp, grp), 0)
    coln = lax.broadcasted_iota(jnp.int32, (grp, grp), 1)
    seq_bits = ts.bit_length() - 1
    mask_new = ((rown >> seq_bits) == (coln >> seq_bits)) & (coln <= rown)
    mask_cache = (lax.broadcasted_iota(jnp.int32, (grp, WINDOW), 1)
                  > (lax.broadcasted_iota(jnp.int32, (grp, WINDOW), 0) & (ts - 1)))

    for g0 in range(0, n, grp):
        rows = slice(g0, g0 + grp)
        seqs = slice(g0 // ts, (g0 + grp) // ts)
        s_new = _dot_nt(jnp.concatenate([z[rows] for z in qh], axis=0), kb[rows])
        e_new = []
        for hd in range(N_HEADS):
            sink = sinks_ref[layer, hd]
            hrows = slice(hd * ts, (hd + 1) * ts)
            sn = jnp.where(mask_new, s_new[hd * grp:(hd + 1) * grp], NEG)
            sc_ = jnp.where(mask_cache, s_scr[seqs, hrows, :].reshape(grp, WINDOW), NEG)
            m = jnp.maximum(jnp.maximum(jnp.max(sn, axis=-1, keepdims=True),
                                        jnp.max(sc_, axis=-1, keepdims=True)), sink)
            en = jnp.exp(sn - m)
            ec = jnp.exp(sc_ - m)
            inv = 1.0 / (jnp.sum(en, axis=-1, keepdims=True) + jnp.sum(ec, axis=-1, keepdims=True)
                         + jnp.exp(sink - m))
            e_new.append((en * inv).astype(BF16))
            s_scr[seqs, hrows, :] = (ec * inv).reshape(grp // ts, ts, WINDOW)
        o_new = _dot(jnp.concatenate(e_new, axis=0), vb[rows])
        for hd in range(N_HEADS):
            q_scr[seqs, hd * ts:(hd + 1) * ts, :] = o_new[hd * grp:(hd + 1) * grp].reshape(
                grp // ts, ts, LANES)

    def cache_pv(b, carry):
        q_scr[b] += _dot(s_scr[b].astype(BF16), cv_ref[b].astype(BF16))
        return carry

    lax.fori_loop(0, sb, cache_pv, 0, unroll=SEQ_UNROLL)

    for p in range(N_PAIRS):
        halves = []
        for hd in (2 * p, 2 * p + 1):
            o = q_scr[:, hd * ts:(hd + 1) * ts, :].reshape(n, LANES)
            if hd % 2 != hd // group:
                o = pltpu.roll(o, HEAD_DIM, axis=1)
            halves.append(o)
        mix_ref[:, p * LANES:(p + 1) * LANES] = jnp.where(
            lane < HEAD_DIM, halves[0], halves[1]).astype(BF16)

    u = jax.nn.gelu(proj[:, ATTN_WIDTH + 2 * KV_WIDTH:ATTN_WIDTH + 2 * KV_WIDTH + GMLP_WIDTH])
    vg = _gelu_layernorm(proj[:, IN_WIDTH - GMLP_WIDTH:], lng_ref[...], lnb_ref[...])
    vgout_ref[...] = vg
    _spatial_gate_into(mix_ref, u, vg, ws_ref, bs_ref)

    y_ref[...] = x + _read_cols(mod_scr.at[0]) * _dot(mix_ref[...], wout_ref[...])


def _sample_mod_specs(layer, sb, first):
    return [pl.BlockSpec((None, sb, D_MODEL), functools.partial(lambda i, w: (layer, i, w), w=first + m))
            for m in range(3)]


def _mix_sample_call(layer, x, mod3, cache_k, cache_v, pp, rope, ts, prev):
    nt = x.shape[0]
    sb = SAMPLE_SEQS
    n = sb * ts
    w_specs_a, w_specs_b = _mix_weight_specs(layer, "sample")
    w_args_a, w_args_b = _mix_weight_args(pp, "sample")
    tok_spec = lambda w: pl.BlockSpec((None, n, w), lambda i: (layer, i, 0))
    tok_shape = lambda w: jax.ShapeDtypeStruct((DEPTH, nt, w), F32)
    cache_spec = pl.BlockSpec((None, sb, WINDOW, KV_WIDTH), lambda i: (layer, i, 0, 0))
    return _layer_call(
        functools.partial(_mix_sample_kernel, layer=layer), layer, prev,
        grid=(nt // n,),
        in_specs=[pl.BlockSpec((n, D_MODEL), lambda i: (i, 0))] + _sample_mod_specs(layer, sb, 0)
        + [cache_spec, cache_spec] + w_specs_a + [_const_spec((3, n, LANES))] + w_specs_b,
        args=[x, mod3, mod3, mod3, cache_k, cache_v] + w_args_a + [rope] + w_args_b,
        out_specs=[pl.BlockSpec((n, D_MODEL), lambda i: (i, 0)),
                   tok_spec(KV_WIDTH), tok_spec(KV_WIDTH), tok_spec(GMLP_WIDTH)],
        out_shapes=[jax.ShapeDtypeStruct((nt, D_MODEL), F32),
                    tok_shape(KV_WIDTH), tok_shape(KV_WIDTH), tok_shape(GMLP_WIDTH)],
        scratch=[
            pltpu.VMEM((2, D_MODEL // LANES, n, LANES), F32),
            pltpu.VMEM((sb, N_HEADS * ts, LANES), F32),
            pltpu.VMEM((sb, N_HEADS * ts, WINDOW), F32),
            pltpu.VMEM((n, D_MODEL), BF16),
        ],
        name="mix_sample")


def _shifted_rows(hu, before):
    rows, c = hu.shape
    hu3 = hu.reshape(rows // SUBLANES, SUBLANES, c)
    above = jnp.concatenate([before[None], hu3[:-1]], axis=0)
    sub = lax.broadcasted_iota(jnp.int32, (1, SUBLANES, c), 1)
    p1 = pltpu.roll(jnp.where(sub >= SUBLANES - 1, above, hu3), 1, axis=1)
    p2 = pltpu.roll(jnp.where(sub >= SUBLANES - 2, above, hu3), 2, axis=1)
    return p1.reshape(rows, c), p2.reshape(rows, c)


def _conv(hu, prev1, prev2, cw, cb):
    return cb + cw[0:1] * prev2 + cw[1:2] * prev1 + cw[2:3] * hu


def _ffn_chunks():
    return [(slice(j, j + FFN_CHUNK), slice(D_FF + j, D_FF + j + FFN_CHUNK))
            for j in range(0, D_FF, FFN_CHUNK)]


def _down_plan(n_chunks):
    bounds = list(range(0, n_chunks, DOWN_GROUP)) + [n_chunks]
    groups = [(a * FFN_CHUNK, b * FFN_CHUNK) for a, b in zip(bounds[:-1], bounds[1:])]
    return {b // FFN_CHUNK: (a, b) for a, b in groups[:-1]}, groups[-1]


def _down_into(acc_ref, act_scr, wdn_ref, a, b):
    contrib = _dot(act_scr[:, a:b], wdn_ref[a:b, :])
    if a == 0:
        acc_ref[...] = contrib
    else:
        acc_ref[...] += contrib


def _ffn_prompt_kernel(x_ref, mod_ref, gffn_ref, wup_ref, cw_ref, cb_ref, wdn_ref,
                       y_ref, cout_ref, carry, act_scr, acc_ref):
    tt = x_ref.shape[1]
    t = pl.program_id(1)

    @pl.when(t == 0)
    def _():
        carry[...] = jnp.zeros(carry.shape, F32)

    x = x_ref[0]
    hb = _modulated_rmsnorm(x, gffn_ref[...], mod_ref[4:5], mod_ref[3:4]).astype(BF16)
    chunks = _ffn_chunks()
    after, tail = _down_plan(len(chunks))
    for j, halves in enumerate(chunks):
        hus = [_dot(hb, wup_ref[:, cols]) for cols in halves]
        if j in after:
            _down_into(acc_ref, act_scr, wdn_ref, *after[j])
        convs = []
        for hu, cols in zip(hus, halves):
            prev1, prev2 = _shifted_rows(hu, carry[:, cols])
            carry[:, cols] = hu[tt - SUBLANES:, :]
            cout_ref[0, :, cols] = hu[tt - (CONV_W - 1):, :]
            convs.append(_conv(hu, prev1, prev2, cw_ref[:, cols], cb_ref[:, cols]))
        act_scr[:, j * FFN_CHUNK:(j + 1) * FFN_CHUNK] = (
            jax.nn.silu(convs[0]) * convs[1]).astype(BF16)
    _down_into(acc_ref, act_scr, wdn_ref, *tail)
    y_ref[0] = x + mod_ref[5:6] * acc_ref[...]


def _ffn_weight_specs(layer):
    return [
        _layer_spec(layer, (1, D_MODEL)),
        _layer_spec(layer, (D_MODEL, 2 * D_FF), single_buffer=True),
        _layer_spec(layer, (CONV_W, 2 * D_FF)),
        _layer_spec(layer, (1, 2 * D_FF)),
        _layer_spec(layer, (D_FF, D_MODEL), single_buffer=True),
    ]


def _ffn_weight_args(pp):
    return [pp["g_ffn"], pp["w_up"], pp["conv_w"], pp["conv_b"], pp["w_down"]]


def _ffn_prompt_call(layer, x, mod4, n_skip, pp, prev):
    b, t, _ = x.shape
    tt = FFN_TILE
    return _layer_call(
        _ffn_prompt_kernel, layer, prev,
        grid=(b, t // tt),
        in_specs=[pl.BlockSpec((1, tt, D_MODEL), lambda i, j: (i, j, 0)),
                  pl.BlockSpec((None, None, N_MOD, D_MODEL), lambda i, j: (layer, n_skip + i, 0, 0))]
        + _ffn_weight_specs(layer),
        args=[x, mod4] + _ffn_weight_args(pp),
        out_specs=[pl.BlockSpec((1, tt, D_MODEL), lambda i, j: (i, j, 0)),
                   pl.BlockSpec((None, 1, CONV_W - 1, 2 * D_FF), lambda i, j: (layer, i, 0, 0))],
        out_shapes=[jax.ShapeDtypeStruct((b, t, D_MODEL), F32),
                    jax.ShapeDtypeStruct((DEPTH, b, CONV_W - 1, 2 * D_FF), F32)],
        scratch=[
            pltpu.VMEM((SUBLANES, 2 * D_FF), F32),
            pltpu.VMEM((tt, D_FF), BF16),
            pltpu.VMEM((tt, D_MODEL), F32),
        ],
        name="ffn_prompt")


def _ffn_sample_kernel(x_ref, sh_ref, sc_ref, gf_ref, hist_ref, gffn_ref, wup_ref, cw_ref,
                       cb_ref, wdn_ref, y_ref, cout_ref, mod_scr, h1, h2, act_scr, acc_ref):
    n = x_ref.shape[0]
    sb = sh_ref.shape[0]
    ts = n // sb
    pad = SUBLANES
    _expand_rows(mod_scr.at[0], sh_ref[...], ts)
    _expand_rows(mod_scr.at[1], sc_ref[...], ts)
    x = x_ref[...]
    hb = _modulated_rmsnorm(x, gffn_ref[...], _read_cols(mod_scr.at[1]),
                            _read_cols(mod_scr.at[0])).astype(BF16)
    _expand_rows(mod_scr.at[0], gf_ref[...], ts)
    chunks = _ffn_chunks()
    after, tail = _down_plan(len(chunks))
    for j, halves in enumerate(chunks):
        hus = [_dot(hb, wup_ref[:, cols]) for cols in halves]
        if j in after:
            _down_into(acc_ref, act_scr, wdn_ref, *after[j])
        convs = []
        for hu, cols in zip(hus, halves):
            for c in range(FFN_CHUNK // LANES):
                off = cols.start + c * LANES
                piece = hu[:, c * LANES:(c + 1) * LANES]
                h1[c, pad:, :] = piece
                h2[c, pad:, :] = piece
                cout_ref[:, 0, off:off + LANES] = h2[c, pl.ds(pad + ts - 2, sb, stride=ts), :]
                cout_ref[:, 1, off:off + LANES] = h2[c, pl.ds(pad + ts - 1, sb, stride=ts), :]
                c0 = hist_ref[:, 0, off:off + LANES]
                c1 = hist_ref[:, 1, off:off + LANES]
                h1[c, pl.ds(pad - 1, sb, stride=ts), :] = c1
                h2[c, pl.ds(pad - 2, sb, stride=ts), :] = c0
                h2[c, pl.ds(pad - 1, sb, stride=ts), :] = c1
            convs.append(_conv(hu, _read_cols(h1, slice(pad - 1, pad - 1 + n)),
                               _read_cols(h2, slice(pad - 2, pad - 2 + n)),
                               cw_ref[:, cols], cb_ref[:, cols]))
        act_scr[:, j * FFN_CHUNK:(j + 1) * FFN_CHUNK] = (
            jax.nn.silu(convs[0]) * convs[1]).astype(BF16)
    _down_into(acc_ref, act_scr, wdn_ref, *tail)
    y_ref[...] = x + _read_cols(mod_scr.at[0]) * acc_ref[...]


def _ffn_sample_call(layer, x, mod3, cconv, pp, ts, prev):
    nt = x.shape[0]
    sb = SAMPLE_SEQS
    n = sb * ts
    tok_spec = pl.BlockSpec((n, D_MODEL), lambda i: (i, 0))
    hist_spec = pl.BlockSpec((None, sb, CONV_W - 1, 2 * D_FF), lambda i: (layer, i, 0, 0))
    return _layer_call(
        _ffn_sample_kernel, layer, prev,
        grid=(nt // n,),
        in_specs=[tok_spec] + _sample_mod_specs(layer, sb, 3) + [hist_spec]
        + _ffn_weight_specs(layer),
        args=[x, mod3, mod3, mod3, cconv] + _ffn_weight_args(pp),
        out_specs=[tok_spec, hist_spec],
        out_shapes=[jax.ShapeDtypeStruct((nt, D_MODEL), F32),
                    jax.ShapeDtypeStruct(cconv.shape, F32)],
        scratch=[
            pltpu.VMEM((2, D_MODEL // LANES, n, LANES), F32),
            pltpu.VMEM((FFN_CHUNK // LANES, n + SUBLANES, LANES), F32),
            pltpu.VMEM((FFN_CHUNK // LANES, n + SUBLANES, LANES), F32),
            pltpu.VMEM((n, D_FF), BF16),
            pltpu.VMEM((n, D_MODEL), F32),
        ],
        name="ffn_sample")


def _rope_tables(pos):
    half = ROPE_DIMS // 2
    inv = ROPE_THETA ** (-jnp.arange(0, ROPE_DIMS, 2, dtype=F32) / ROPE_DIMS)
    ang = pos.astype(F32)[:, None] * inv[None, :]
    cos, sin = jnp.cos(ang), jnp.sin(ang)
    n = pos.shape[0]
    rest = jnp.zeros((n, HEAD_DIM - ROPE_DIMS), F32)
    zeros = jnp.zeros((n, half), F32)
    cos_t = jnp.concatenate([cos, cos, rest + 1.0], axis=1)
    sin_dn = jnp.concatenate([-sin, zeros, rest], axis=1)
    sin_up = jnp.concatenate([zeros, sin, rest], axis=1)
    return jnp.stack([jnp.tile(z, (1, LANES // HEAD_DIM)) for z in (cos_t, sin_dn, sin_up)])


def _block_diag_ones(width):
    idx = jnp.arange(width) // HEAD_DIM
    return (idx[:, None] == idx[None, :]).astype(BF16)


def _prepare_params(dec_seq, g_attn, w_in, g_q, g_k, sinks, ln_g, ln_b, w_s, b_s, w_out,
                    g_ffn, w_ffn_in, conv_w, conv_b, w_ffn_out):
    causal = jnp.tril(jnp.ones((CHUNK, CHUNK), dtype=bool))
    ws = jnp.where(causal, w_s, 0.0)
    seqs_per_chunk = CHUNK // dec_seq
    eye = jnp.eye(seqs_per_chunk, dtype=F32)
    ws_sample = jnp.einsum("ab,lgts->lgatbs", eye, ws[:, :, :dec_seq, :dec_seq]).reshape(w_s.shape)
    return {
        "g_attn": g_attn[:, None, :], "g_ffn": g_ffn[:, None, :],
        "w_in": w_in.astype(BF16), "w_out": w_out.astype(BF16),
        "g_q": jnp.tile(g_q, (1, N_HEADS))[:, None, :],
        "g_k": jnp.tile(g_k, (1, N_KV_HEADS))[:, None, :],
        "bd_q": _block_diag_ones(ATTN_WIDTH), "bd_k": _block_diag_ones(KV_WIDTH),
        "sinks": sinks,
        "ln_g": ln_g.reshape(DEPTH, 1, GMLP_WIDTH), "ln_b": ln_b.reshape(DEPTH, 1, GMLP_WIDTH),
        "ws_prompt": ws.astype(BF16), "bs_prompt": jnp.swapaxes(b_s, 1, 2),
        "ws_sample": ws_sample.astype(BF16),
        "bs_sample": jnp.swapaxes(jnp.tile(b_s[:, :, :dec_seq], (1, 1, seqs_per_chunk)), 1, 2),
        "w_up": w_ffn_in.astype(BF16), "w_down": w_ffn_out.astype(BF16),
        "conv_w": conv_w, "conv_b": conv_b[:, None, :],
    }


def kernel(x_prompt, x_sample, cache_k, cache_v, cache_conv, c_prompt, c_sample, w_ada, b_ada,
           g_attn, w_in, g_q, g_k, sinks, ln_g, ln_b, w_s, b_s, w_out, g_ffn, w_ffn_in, conv_w,
           conv_b, w_ffn_out):
    nbp, seq, _ = x_prompt.shape
    nbs, dec_seq, _ = x_sample.shape
    assert seq % PROMPT_TILE == 0 and PROMPT_TILE % CHUNK == 0 and nbs % SAMPLE_SEQS == 0
    assert seq % FFN_TILE == 0 and FFN_TILE % SUBLANES == 0 and D_FF % FFN_CHUNK == 0
    assert dec_seq == SUBLANES and CHUNK % dec_seq == 0 and cache_k.shape[2] == WINDOW
    assert nbs % SUBLANES == 0

    n_c = nbs + nbp
    c_all = jnp.concatenate([c_sample, c_prompt, jnp.zeros((-n_c % SUBLANES, D_MODEL), F32)])
    mod3 = _ada_call(c_all, w_ada, b_ada)
    mod4 = mod3.reshape(DEPTH, c_all.shape[0], N_MOD, D_MODEL)

    pp = _prepare_params(dec_seq, g_attn, w_in, g_q, g_k, sinks, ln_g, ln_b, w_s, b_s, w_out,
                         g_ffn, w_ffn_in, conv_w, conv_b, w_ffn_out)
    rope_p = _rope_tables(jnp.arange(seq, dtype=jnp.int32))
    rope_s = jnp.tile(_rope_tables(PAST_LEN + jnp.arange(dec_seq, dtype=jnp.int32)),
                      (1, SAMPLE_SEQS, 1))
    ck = cache_k.reshape(DEPTH, nbs, WINDOW, KV_WIDTH)
    cv = cache_v.reshape(DEPTH, nbs, WINDOW, KV_WIDTH)

    xp = x_prompt
    xs = x_sample.reshape(nbs * dec_seq, D_MODEL)
    mix_p = ffn_p = mix_s = ffn_s = None
    for l in range(DEPTH):
        xp, *mix_p = _mix_prompt_call(l, xp, mod4, nbs, pp, rope_p, mix_p)
        xp, *ffn_p = _ffn_prompt_call(l, xp, mod4, nbs, pp, ffn_p)
        xs, *mix_s = _mix_sample_call(l, xs, mod3, ck, cv, pp, rope_s, dec_seq, mix_s)
        xs, *ffn_s = _ffn_sample_call(l, xs, mod3, cache_conv, pp, dec_seq, ffn_s)

    kv_p = (DEPTH, nbp, WINDOW, N_KV_HEADS, HEAD_DIM)
    kv_s = (DEPTH, nbs, dec_seq, N_KV_HEADS, HEAD_DIM)
    return (xp, xs.reshape(nbs, dec_seq, D_MODEL),
            mix_p[0].reshape(kv_p), mix_p[1].reshape(kv_p), ffn_p[0],
            mix_s[0].reshape(kv_s), mix_s[1].reshape(kv_s),
            mix_s[2].reshape(DEPTH, nbs, dec_seq, GMLP_WIDTH),
            ffn_s[0])
```

```python
import functools

import jax
import jax.numpy as jnp
from jax import lax
from jax.experimental import pallas as pl
from jax.experimental.pallas import tpu as pltpu

D_MODEL = 1024
DEPTH = 2
HEAD_DIM = 64
N_HEADS = 8
N_KV_HEADS = 2
KV_WIDTH = N_KV_HEADS * HEAD_DIM
WINDOW = 128
ROPE_THETA = 500000.0
ROPE_DIMS = HEAD_DIM // 4
ATTN_WIDTH = N_HEADS * HEAD_DIM
GMLP_WIDTH = D_MODEL - ATTN_WIDTH
GMLP_GROUPS = 4
GMLP_GW = GMLP_WIDTH // GMLP_GROUPS
CHUNK = 128
IN_WIDTH = ATTN_WIDTH + 2 * KV_WIDTH + 2 * GMLP_WIDTH
D_FF = 2816
CONV_W = 3
N_MOD = 6
EPS = 1e-6
NEG = -1e30
PAST_LEN = 16384

LANES = 128
SUBLANES = 8
N_PAIRS = N_HEADS // 2
VMEM_LIMIT = 56 * 1024 * 1024

PROMPT_TILE = 512
OUT_COLS = 256
MIX_INTERLEAVE = 10
FFN_TILE = 1024
SAMPLE_SEQS = 64
SEQ_UNROLL = 8
FFN_CHUNK = 256
DOWN_GROUP = 2
ADA_TILE = 1536

F32 = jnp.float32
BF16 = jnp.bfloat16


def _dot(a, b):
    return jnp.dot(a, b, preferred_element_type=F32)


def _dot_nt(a, b):
    return lax.dot_general(a, b, (((1,), (1,)), ((), ())), preferred_element_type=F32)


def _const_spec(shape):
    nd = len(shape)
    return pl.BlockSpec(shape, lambda *_: (0,) * nd)


def _layer_spec(layer, tail, single_buffer=False):
    nd = len(tail)
    kw = {"pipeline_mode": pl.Buffered(1)} if single_buffer else {}
    return pl.BlockSpec((None,) + tuple(tail), lambda *_: (layer,) + (0,) * nd, **kw)


def _skip_aliased(body, n_in, n_alias):
    if n_alias == 0:
        return body
    return lambda *refs: body(*refs[:n_in], *refs[n_in + n_alias:])


def _layer_call(body, layer, prev, *, grid, in_specs, args, out_specs, out_shapes, scratch, name,
                n_fresh=1):
    n_in = len(args)
    aliased = [] if prev is None else list(prev)
    return pl.pallas_call(
        _skip_aliased(body, n_in, len(aliased)),
        grid=grid,
        in_specs=list(in_specs) + [pl.BlockSpec(memory_space=pl.ANY)] * len(aliased),
        out_specs=out_specs,
        out_shape=out_shapes,
        input_output_aliases={n_in + i: n_fresh + i for i in range(len(aliased))},
        scratch_shapes=scratch,
        compiler_params=pltpu.CompilerParams(
            dimension_semantics=("arbitrary",) * len(grid), vmem_limit_bytes=VMEM_LIMIT),
        name=name,
    )(*args, *aliased)


def _ada_kernel(c_ref, w_ref, b_ref, o_ref):
    c = c_ref[...]
    a = (c * jax.nn.sigmoid(c)).astype(BF16)
    o_ref[0] = _dot(a, w_ref[0].astype(BF16)) + b_ref[0]


def _ada_call(c_all, w_ada, b_ada):
    n = c_all.shape[0]
    return pl.pallas_call(
        _ada_kernel,
        grid=(DEPTH, N_MOD * D_MODEL // ADA_TILE),
        in_specs=[
            pl.BlockSpec((n, D_MODEL), lambda l, j: (0, 0)),
            pl.BlockSpec((1, D_MODEL, ADA_TILE), lambda l, j: (l, 0, j)),
            pl.BlockSpec((1, 1, ADA_TILE), lambda l, j: (l, 0, j)),
        ],
        out_specs=pl.BlockSpec((1, n, ADA_TILE), lambda l, j: (l, 0, j)),
        out_shape=jax.ShapeDtypeStruct((DEPTH, n, N_MOD * D_MODEL), F32),
        compiler_params=pltpu.CompilerParams(
            dimension_semantics=("arbitrary", "arbitrary"), vmem_limit_bytes=VMEM_LIMIT),
        name="ada_mod",
    )(c_all, w_ada, b_ada.reshape(DEPTH, 1, N_MOD * D_MODEL))


def _modulated_rmsnorm(x, gain, scale, shift):
    r = lax.rsqrt(jnp.mean(x * x, axis=-1, keepdims=True) + EPS)
    return (x * r) * (gain * (1.0 + scale)) + shift


def _head_rmsnorm(z, ones_bd, gain):
    ssq = _dot((z * z).astype(BF16), ones_bd)
    return z * lax.rsqrt(ssq * (1.0 / HEAD_DIM) + EPS) * gain


def _rope(z, cos_t, sin_dn, sin_up):
    half = ROPE_DIMS // 2
    cols = []
    for p in range(z.shape[-1] // LANES):
        zp = z[:, p * LANES:(p + 1) * LANES]
        cols.append(zp * cos_t
                    + pltpu.roll(zp, LANES - half, axis=1) * sin_dn
                    + pltpu.roll(zp, half, axis=1) * sin_up)
    return cols[0] if len(cols) == 1 else jnp.concatenate(cols, axis=1)


def _split_heads(z):
    lane = lax.broadcasted_iota(jnp.int32, z.shape, 1)
    lo = lane < HEAD_DIM
    zs = pltpu.roll(z, HEAD_DIM, axis=1)
    zero = jnp.zeros_like(z)
    return (jnp.where(lo, z, zero), jnp.where(lo, zero, zs),
            jnp.where(lo, zs, zero), jnp.where(lo, zero, z))


def _gelu_layernorm(zv, ln_g, ln_b):
    gv = jax.nn.gelu(zv)
    cols = []
    for g in range(GMLP_GROUPS):
        xg = gv[:, g * GMLP_GW:(g + 1) * GMLP_GW]
        mu = jnp.mean(xg, axis=-1, keepdims=True)
        xc = xg - mu
        var = jnp.mean(xc * xc, axis=-1, keepdims=True)
        cols.append(xc * lax.rsqrt(var + EPS))
    return jnp.concatenate(cols, axis=1) * ln_g + ln_b


def _spatial_gate_into(mix_ref, u, vg, ws_ref, bs_ref):
    vgb = vg.astype(BF16)
    for g in range(GMLP_GROUPS):
        w = ws_ref[g]
        bias = bs_ref[:, g:g + 1]
        cs = slice(g * GMLP_GW, (g + 1) * GMLP_GW)
        for c in range(u.shape[0] // CHUNK):
            rs = slice(c * CHUNK, (c + 1) * CHUNK)
            z = _dot(w, vgb[rs, cs]) + bias
            mix_ref[rs, ATTN_WIDTH + g * GMLP_GW:ATTN_WIDTH + (g + 1) * GMLP_GW] = (
                u[rs, cs] * z).astype(BF16)


def _lane_pair(a, b, shape):
    lane = lax.broadcasted_iota(jnp.int32, shape, 1)
    return jnp.where(lane < HEAD_DIM, a, b)


def _expand_rows(dst_ref, src, reps):
    for c in range(dst_ref.shape[0]):
        piece = src[:, c * LANES:(c + 1) * LANES]
        for t in range(reps):
            dst_ref[c, pl.ds(t, src.shape[0], stride=reps), :] = piece


def _read_cols(ref, rows=slice(None)):
    return jnp.concatenate([ref[c, rows, :] for c in range(ref.shape[0])], axis=1)


def _interleave(tasks, width):
    pending = iter(tasks)
    active = []
    while True:
        while len(active) < width:
            task = next(pending, None)
            if task is None:
                break
            active.append(task)
        if not active:
            return
        for task in list(active):
            if next(task, "done") == "done":
                active.remove(task)


def _mix_prompt_kernel(xa_ref, xc_ref, moda_ref, modc_ref, gattn_ref, win32_ref, gq_ref, gk_ref,
                       bdq_ref, bdk_ref, rope_ref, sinks_ref, lng_ref, lnb_ref, ws_ref, bs_ref,
                       wout32_ref, wup32_ref, wdn32_ref,
                       y_ref, win_ref, wout_ref, wup_ref, wdn_ref, kout_ref, vout_ref,
                       q_ring, k_ring, v_ring, u_ring, vg_ring, mix_ring, *, layer, tiles_per_seq):
    tt = xa_ref.shape[1]
    g = pl.program_id(0)

    @pl.when(g == 0)
    def _():
        win_ref[...] = win32_ref[...].astype(BF16)
        wout_ref[...] = wout32_ref[...].astype(BF16)
        for ring in (q_ring, k_ring, v_ring, u_ring, vg_ring, mix_ring):
            ring[...] = jnp.zeros(ring.shape, ring.dtype)

    wup_ref[...] = wup32_ref[...].astype(BF16)
    wdn_ref[...] = wdn32_ref[...].astype(BF16)

    cur, prv = g % 2, (g + 1) % 2
    kv_a, kv_b, kv_p = g % 3, (g + 2) % 3, (g + 1) % 3

    cos_t, sin_dn, sin_up = rope_ref[0], rope_ref[1], rope_ref[2]
    env = {}

    def norm_task():
        env["hb"] = _modulated_rmsnorm(
            xa_ref[0], gattn_ref[...], moda_ref[1:2], moda_ref[0:1]).astype(BF16)
        yield

    def q_task():
        q = _dot(env["hb"], win_ref[:, :ATTN_WIDTH])
        yield
        ssq = _dot((q * q).astype(BF16), bdq_ref[...])
        yield
        q = q * lax.rsqrt(ssq * (1.0 / HEAD_DIM) + EPS) * gq_ref[...]
        q_ring[cur] = (_rope(q, cos_t, sin_dn, sin_up) * (HEAD_DIM ** -0.5)).astype(BF16)

    def kv_task():
        kv = _dot(env["hb"], win_ref[:, ATTN_WIDTH:ATTN_WIDTH + 2 * KV_WIDTH])
        yield
        k, v = kv[:, :KV_WIDTH], kv[:, KV_WIDTH:]
        ssq = _dot((k * k).astype(BF16), bdk_ref[...])
        vout_ref[0] = v[tt - WINDOW:, :]
        for i, vz in enumerate(_split_heads(v)):
            v_ring[kv_a, i] = vz.astype(BF16)
        yield
        k = _rope(k * lax.rsqrt(ssq * (1.0 / HEAD_DIM) + EPS) * gk_ref[...], cos_t, sin_dn, sin_up)
        kout_ref[0] = k[tt - WINDOW:, :]
        for i, kz in enumerate(_split_heads(k)):
            k_ring[kv_a, i] = kz.astype(BF16)

    def u_task():
        zu = _dot(env["hb"], win_ref[:, ATTN_WIDTH + 2 * KV_WIDTH:IN_WIDTH - GMLP_WIDTH])
        yield
        u_ring[cur] = jax.nn.gelu(zu)

    def vg_task():
        zv = _dot(env["hb"], win_ref[:, IN_WIDTH - GMLP_WIDTH:])
        yield
        vg_ring[cur] = _gelu_layernorm(zv, lng_ref[...], lnb_ref[...]).astype(BF16)

    row = lax.broadcasted_iota(jnp.int32, (WINDOW, 2 * WINDOW), 0)
    col = lax.broadcasted_iota(jnp.int32, (WINDOW, 2 * WINDOW), 1)
    band = (col - row >= 1) & (col - row <= WINDOW)
    seq_start = (g + tiles_per_seq - 1) % tiles_per_seq == 0
    band_first = band & (col >= jnp.where(seq_start, WINDOW, 0))
    tail = slice(tt - WINDOW, tt)

    def band_rows(ring, idx, i):
        if i == 0:
            return [ring[kv_p, idx, tail, :], ring[kv_b, idx, 0:WINDOW, :]]
        return [ring[kv_b, idx, (i - 1) * WINDOW:(i + 1) * WINDOW, :]]

    def attn_task(i, p):
        mask = band_first if i == 0 else band
        rows = slice(i * WINDOW, (i + 1) * WINDOW)
        kv = p // 2
        kblk = jnp.concatenate(band_rows(k_ring, 2 * kv, i) + band_rows(k_ring, 2 * kv + 1, i), axis=0)
        s = _dot_nt(q_ring[prv, rows, p * LANES:(p + 1) * LANES], kblk)
        yield
        es, invs = [], []
        for hh in range(2):
            sink = sinks_ref[layer, 2 * p + hh]
            sh_ = jnp.where(mask, s[:, hh * 2 * WINDOW:(hh + 1) * 2 * WINDOW], NEG)
            m = jnp.maximum(jnp.max(sh_, axis=-1, keepdims=True), sink)
            e = jnp.exp(sh_ - m)
            es.append(e)
            invs.append(1.0 / (jnp.sum(e, axis=-1, keepdims=True) + jnp.exp(sink - m)))
        vblk = jnp.concatenate(band_rows(v_ring, 2 * kv, i) + band_rows(v_ring, 2 * kv + 1, i), axis=0)
        o = _dot(jnp.concatenate(es, axis=1).astype(BF16), vblk)
        yield
        o = o * _lane_pair(invs[0], invs[1], o.shape)
        mix_ring[cur, rows, p * LANES:(p + 1) * LANES] = o.astype(BF16)

    def gate_task(c, grp):
        rs = slice(c * CHUNK, (c + 1) * CHUNK)
        cs = slice(grp * GMLP_GW, (grp + 1) * GMLP_GW)
        z = _dot(ws_ref[grp], vg_ring[prv, rs, cs])
        yield
        z = z + bs_ref[:, grp:grp + 1]
        mix_ring[cur, rs, ATTN_WIDTH + grp * GMLP_GW:ATTN_WIDTH + (grp + 1) * GMLP_GW] = (
            u_ring[prv, rs, cs] * z).astype(BF16)

    def out_task(c):
        cols = slice(c * OUT_COLS, (c + 1) * OUT_COLS)
        r = _dot(mix_ring[prv], wout_ref[:, cols])
        yield
        y_ref[0, :, cols] = xc_ref[0, :, cols] + modc_ref[2:3, cols] * r

    n_blocks = tt // WINDOW
    attn = [attn_task(i, p) for i in range(n_blocks) for p in range(N_PAIRS)]
    gate = [gate_task(c, grp) for c in range(n_blocks) for grp in range(GMLP_GROUPS)]
    big = [out_task(0), norm_task(), vg_task(), out_task(1), u_task(), q_task(), out_task(2),
           kv_task(), out_task(3)]
    small = [t for pair in zip(attn, gate) for t in pair]
    per_big = -(-len(small) // len(big))
    order = []
    for b in big:
        order.append(b)
        order.extend(small[:per_big])
        small = small[per_big:]
    _interleave(order + small, MIX_INTERLEAVE)


def _mix_weight_specs(layer, w_in_spec, w_out_spec):
    return [
        _layer_spec(layer, (1, D_MODEL)),
        w_in_spec,
        _layer_spec(layer, (1, ATTN_WIDTH)),
        _layer_spec(layer, (1, KV_WIDTH)),
        _const_spec((ATTN_WIDTH, ATTN_WIDTH)),
        _const_spec((KV_WIDTH, KV_WIDTH)),
    ], [
        pl.BlockSpec(memory_space=pltpu.SMEM),
        _layer_spec(layer, (1, GMLP_WIDTH)),
        _layer_spec(layer, (1, GMLP_WIDTH)),
        _layer_spec(layer, (GMLP_GROUPS, CHUNK, CHUNK)),
        _layer_spec(layer, (CHUNK, GMLP_GROUPS)),
        w_out_spec,
    ]


def _mix_weight_args(pp, kind, w_in, w_out):
    return ([pp["g_attn"], w_in, pp["g_q"], pp["g_k"], pp["bd_q"], pp["bd_k"]],
            [pp["sinks"], pp["ln_g"], pp["ln_b"], pp["ws_" + kind], pp["bs_" + kind], w_out])


def _mix_prompt_call(layer, x, mod4, n_skip, pp, rope, prev):
    b, t, _ = x.shape
    tt = PROMPT_TILE
    nt = t // tt
    last = b * nt - 1
    w_specs_a, w_specs_b = _mix_weight_specs(
        layer, _layer_spec(layer, (D_MODEL, IN_WIDTH), single_buffer=True),
        _layer_spec(layer, (D_MODEL, D_MODEL), single_buffer=True))
    w_args_a, w_args_b = _mix_weight_args(pp, "prompt", pp["w_in"], pp["w_out"])
    up_rows, dn_rows = D_MODEL // (b * nt), D_FF // (b * nt // 2)
    assert up_rows % 16 == 0 and dn_rows % 16 == 0

    def tile(g, lag):
        return jnp.clip(g - lag, 0, last)

    def slab_specs(rows, n_slabs, width, src_layer):
        idx = lambda g: jnp.minimum(g, n_slabs - 1)
        return (pl.BlockSpec((None, rows, width), lambda g: (src_layer, idx(g), 0)),
                pl.BlockSpec((rows, width), lambda g: (idx(g), 0)))

    up_in, up_out = slab_specs(up_rows, D_MODEL // up_rows, 2 * D_FF, layer)
    dn_in, dn_out = slab_specs(dn_rows, D_FF // dn_rows, D_MODEL, layer)

    def x_spec(lag):
        return pl.BlockSpec((1, tt, D_MODEL), lambda g: (tile(g, lag) // nt, tile(g, lag) % nt, 0))

    def mod_spec(lag):
        return pl.BlockSpec((None, None, N_MOD, D_MODEL),
                            lambda g: (layer, n_skip + tile(g, lag) // nt, 0, 0))

    kv_spec = pl.BlockSpec((None, 1, WINDOW, KV_WIDTH), lambda g: (layer, tile(g, 0) // nt, 0, 0))
    kv_shape = jax.ShapeDtypeStruct((DEPTH, b, WINDOW, KV_WIDTH), F32)
    return _layer_call(
        functools.partial(_mix_prompt_kernel, layer=layer, tiles_per_seq=nt), layer, prev,
        grid=(b * nt + 2,),
        in_specs=[x_spec(0), x_spec(2), mod_spec(0), mod_spec(2)] + w_specs_a
        + [pl.BlockSpec((3, tt, LANES), lambda g: (0, tile(g, 0) % nt, 0))] + w_specs_b
        + [up_in, dn_in],
        args=[x, x, mod4, mod4] + w_args_a + [rope] + w_args_b + [pp["w_up"], pp["w_down"]],
        out_specs=[x_spec(2), _const_spec((D_MODEL, IN_WIDTH)), _const_spec((D_MODEL, D_MODEL)),
                   up_out, dn_out, kv_spec, kv_spec],
        out_shapes=[jax.ShapeDtypeStruct((b, t, D_MODEL), F32),
                    jax.ShapeDtypeStruct((D_MODEL, IN_WIDTH), BF16),
                    jax.ShapeDtypeStruct((D_MODEL, D_MODEL), BF16),
                    jax.ShapeDtypeStruct((D_MODEL, 2 * D_FF), BF16),
                    jax.ShapeDtypeStruct((D_FF, D_MODEL), BF16), kv_shape, kv_shape],
        n_fresh=5,
        scratch=[
            pltpu.VMEM((2, tt, ATTN_WIDTH), BF16),
            pltpu.VMEM((3, 4, tt, LANES), BF16),
            pltpu.VMEM((3, 4, tt, LANES), BF16),
            pltpu.VMEM((2, tt, GMLP_WIDTH), F32),
            pltpu.VMEM((2, tt, GMLP_WIDTH), BF16),
            pltpu.VMEM((2, tt, D_MODEL), BF16),
        ],
        name="mix_prompt")


def _mix_sample_kernel(x_ref, sh_ref, sc_ref, ga_ref, ck_ref, cv_ref, gattn_ref, win_ref, gq_ref,
                       gk_ref, bdq_ref, bdk_ref, rope_ref, sinks_ref, lng_ref, lnb_ref, ws_ref,
                       bs_ref, wout_ref, y_ref, kout_ref, vout_ref, vgout_ref,
                       mod_scr, q_scr, s_scr, mix_ref, *, layer):
    n = x_ref.shape[0]
    sb = sh_ref.shape[0]
    ts = n // sb
    _expand_rows(mod_scr.at[0], sh_ref[...], ts)
    _expand_rows(mod_scr.at[1], sc_ref[...], ts)
    x = x_ref[...]
    h = _modulated_rmsnorm(x, gattn_ref[...], _read_cols(mod_scr.at[1]), _read_cols(mod_scr.at[0]))
    proj = _dot(h.astype(BF16), win_ref[...])
    _expand_rows(mod_scr.at[0], ga_ref[...], ts)

    cos_t, sin_dn, sin_up = rope_ref[0], rope_ref[1], rope_ref[2]
    q = _rope(_head_rmsnorm(proj[:, :ATTN_WIDTH], bdq_ref[...], gq_ref[...]), cos_t, sin_dn, sin_up)
    q = q * (HEAD_DIM ** -0.5)
    k = _rope(_head_rmsnorm(proj[:, ATTN_WIDTH:ATTN_WIDTH + KV_WIDTH], bdk_ref[...], gk_ref[...]),
              cos_t, sin_dn, sin_up)
    v = proj[:, ATTN_WIDTH + KV_WIDTH:ATTN_WIDTH + 2 * KV_WIDTH]
    kout_ref[...] = k
    vout_ref[...] = v

    lane = lax.broadcasted_iota(jnp.int32, (n, LANES), 1)
    group = N_HEADS // N_KV_HEADS
    qh = []
    for hd in range(N_HEADS):
        piece = q[:, (hd // 2) * LANES:(hd // 2 + 1) * LANES]
        if hd % 2 != hd // group:
            piece = pltpu.roll(piece, HEAD_DIM, axis=1)
        keep = (lane < HEAD_DIM) if hd // group == 0 else (lane >= HEAD_DIM)
        piece = jnp.where(keep, piece, 0.0)
        q_scr[:, hd * ts:(hd + 1) * ts, :] = piece.reshape(sb, ts, LANES)
        qh.append(piece.astype(BF16))

    def cache_scores(b, carry):
        s_scr[b] = _dot_nt(q_scr[b].astype(BF16), ck_ref[b].astype(BF16))
        return carry

    lax.fori_loop(0, sb, cache_scores, 0, unroll=SEQ_UNROLL)

    kb, vb = k.astype(BF16), v.astype(BF16)
    grp = CHUNK
    rown = lax.broadcasted_iota(jnp.int32, (grp, grp), 0)
    coln = lax.broadcasted_iota(jnp.int32, (grp, grp), 1)
    seq_bits = ts.bit_length() - 1
    mask_new = ((rown >> seq_bits) == (coln >> seq_bits)) & (coln <= rown)
    mask_cache = (lax.broadcasted_iota(jnp.int32, (grp, WINDOW), 1)
                  > (lax.broadcasted_iota(jnp.int32, (grp, WINDOW), 0) & (ts - 1)))

    for g0 in range(0, n, grp):
        rows = slice(g0, g0 + grp)
        seqs = slice(g0 // ts, (g0 + grp) // ts)
        s_new = _dot_nt(jnp.concatenate([z[rows] for z in qh], axis=0), kb[rows])
        e_new = []
        for hd in range(N_HEADS):
            sink = sinks_ref[layer, hd]
            hrows = slice(hd * ts, (hd + 1) * ts)
            sn = jnp.where(mask_new, s_new[hd * grp:(hd + 1) * grp], NEG)
            sc_ = jnp.where(mask_cache, s_scr[seqs, hrows, :].reshape(grp, WINDOW), NEG)
            m = jnp.maximum(jnp.maximum(jnp.max(sn, axis=-1, keepdims=True),
                                        jnp.max(sc_, axis=-1, keepdims=True)), sink)
            en = jnp.exp(sn - m)
            ec = jnp.exp(sc_ - m)
            inv = 1.0 / (jnp.sum(en, axis=-1, keepdims=True) + jnp.sum(ec, axis=-1, keepdims=True)
                         + jnp.exp(sink - m))
            e_new.append((en * inv).astype(BF16))
            s_scr[seqs, hrows, :] = (ec * inv).reshape(grp // ts, ts, WINDOW)
        o_new = _dot(jnp.concatenate(e_new, axis=0), vb[rows])
        for hd in range(N_HEADS):
            q_scr[seqs, hd * ts:(hd + 1) * ts, :] = o_new[hd * grp:(hd + 1) * grp].reshape(
                grp // ts, ts, LANES)

    def cache_pv(b, carry):
        q_scr[b] += _dot(s_scr[b].astype(BF16), cv_ref[b].astype(BF16))
        return carry

    lax.fori_loop(0, sb, cache_pv, 0, unroll=SEQ_UNROLL)

    for p in range(N_PAIRS):
        halves = []
        for hd in (2 * p, 2 * p + 1):
            o = q_scr[:, hd * ts:(hd + 1) * ts, :].reshape(n, LANES)
            if hd % 2 != hd // group:
                o = pltpu.roll(o, HEAD_DIM, axis=1)
            halves.append(o)
        mix_ref[:, p * LANES:(p + 1) * LANES] = jnp.where(
            lane < HEAD_DIM, halves[0], halves[1]).astype(BF16)

    u = jax.nn.gelu(proj[:, ATTN_WIDTH + 2 * KV_WIDTH:ATTN_WIDTH + 2 * KV_WIDTH + GMLP_WIDTH])
    vg = _gelu_layernorm(proj[:, IN_WIDTH - GMLP_WIDTH:], lng_ref[...], lnb_ref[...])
    vgout_ref[...] = vg
    _spatial_gate_into(mix_ref, u, vg, ws_ref, bs_ref)

    y_ref[...] = x + _read_cols(mod_scr.at[0]) * _dot(mix_ref[...], wout_ref[...])


def _sample_mod_specs(layer, sb, first):
    return [pl.BlockSpec((None, sb, D_MODEL), functools.partial(lambda i, w: (layer, i, w), w=first + m))
            for m in range(3)]


def _single_spec(shape):
    nd = len(shape)
    return pl.BlockSpec(shape, lambda *_: (0,) * nd, pipeline_mode=pl.Buffered(1))


def _mix_sample_call(layer, x, mod3, cache_k, cache_v, pp, w_in, w_out, rope, ts, prev):
    nt = x.shape[0]
    sb = SAMPLE_SEQS
    n = sb * ts
    w_specs_a, w_specs_b = _mix_weight_specs(
        layer, _single_spec((D_MODEL, IN_WIDTH)), _single_spec((D_MODEL, D_MODEL)))
    w_args_a, w_args_b = _mix_weight_args(pp, "sample", w_in, w_out)
    tok_spec = lambda w: pl.BlockSpec((None, n, w), lambda i: (layer, i, 0))
    tok_shape = lambda w: jax.ShapeDtypeStruct((DEPTH, nt, w), F32)
    cache_spec = pl.BlockSpec((None, sb, WINDOW, KV_WIDTH), lambda i: (layer, i, 0, 0))
    return _layer_call(
        functools.partial(_mix_sample_kernel, layer=layer), layer, prev,
        grid=(nt // n,),
        in_specs=[pl.BlockSpec((n, D_MODEL), lambda i: (i, 0))] + _sample_mod_specs(layer, sb, 0)
        + [cache_spec, cache_spec] + w_specs_a + [_const_spec((3, n, LANES))] + w_specs_b,
        args=[x, mod3, mod3, mod3, cache_k, cache_v] + w_args_a + [rope] + w_args_b,
        out_specs=[pl.BlockSpec((n, D_MODEL), lambda i: (i, 0)),
                   tok_spec(KV_WIDTH), tok_spec(KV_WIDTH), tok_spec(GMLP_WIDTH)],
        out_shapes=[jax.ShapeDtypeStruct((nt, D_MODEL), F32),
                    tok_shape(KV_WIDTH), tok_shape(KV_WIDTH), tok_shape(GMLP_WIDTH)],
        scratch=[
            pltpu.VMEM((2, D_MODEL // LANES, n, LANES), F32),
            pltpu.VMEM((sb, N_HEADS * ts, LANES), F32),
            pltpu.VMEM((sb, N_HEADS * ts, WINDOW), F32),
            pltpu.VMEM((n, D_MODEL), BF16),
        ],
        name="mix_sample")


def _shifted_rows(hu, before):
    rows, c = hu.shape
    hu3 = hu.reshape(rows // SUBLANES, SUBLANES, c)
    above = jnp.concatenate([before[None], hu3[:-1]], axis=0)
    sub = lax.broadcasted_iota(jnp.int32, (1, SUBLANES, c), 1)
    p1 = pltpu.roll(jnp.where(sub >= SUBLANES - 1, above, hu3), 1, axis=1)
    p2 = pltpu.roll(jnp.where(sub >= SUBLANES - 2, above, hu3), 2, axis=1)
    return p1.reshape(rows, c), p2.reshape(rows, c)


def _conv(hu, prev1, prev2, cw, cb):
    return cb + cw[0:1] * prev2 + cw[1:2] * prev1 + cw[2:3] * hu


def _ffn_chunks():
    return [(slice(j, j + FFN_CHUNK), slice(D_FF + j, D_FF + j + FFN_CHUNK))
            for j in range(0, D_FF, FFN_CHUNK)]


def _down_plan(n_chunks):
    bounds = list(range(0, n_chunks, DOWN_GROUP)) + [n_chunks]
    groups = [(a * FFN_CHUNK, b * FFN_CHUNK) for a, b in zip(bounds[:-1], bounds[1:])]
    return {b // FFN_CHUNK: (a, b) for a, b in groups[:-1]}, groups[-1]


def _down_into(acc_ref, act_scr, wdn_ref, a, b):
    contrib = _dot(act_scr[:, a:b], wdn_ref[a:b, :])
    if a == 0:
        acc_ref[...] = contrib
    else:
        acc_ref[...] += contrib


def _ffn_prompt_kernel(x_ref, mod_ref, gffn_ref, wup_ref, cw_ref, cb_ref, wdn_ref,
                       y_ref, cout_ref, carry, act_scr, acc_ref):
    tt = x_ref.shape[1]
    t = pl.program_id(1)

    @pl.when(t == 0)
    def _():
        carry[...] = jnp.zeros(carry.shape, F32)

    x = x_ref[0]
    hb = _modulated_rmsnorm(x, gffn_ref[...], mod_ref[4:5], mod_ref[3:4]).astype(BF16)
    chunks = _ffn_chunks()
    after, tail = _down_plan(len(chunks))
    up = lambda halves: [_dot(hb, wup_ref[:, cols]) for cols in halves]
    ahead = up(chunks[0])
    for j, halves in enumerate(chunks):
        hus = ahead
        if j + 1 < len(chunks):
            ahead = up(chunks[j + 1])
        if j in after:
            _down_into(acc_ref, act_scr, wdn_ref, *after[j])
        convs = []
        for hu, cols in zip(hus, halves):
            prev1, prev2 = _shifted_rows(hu, carry[:, cols])
            carry[:, cols] = hu[tt - SUBLANES:, :]
            cout_ref[0, :, cols] = hu[tt - (CONV_W - 1):, :]
            convs.append(_conv(hu, prev1, prev2, cw_ref[:, cols], cb_ref[:, cols]))
        act_scr[:, j * FFN_CHUNK:(j + 1) * FFN_CHUNK] = (
            jax.nn.silu(convs[0]) * convs[1]).astype(BF16)
    _down_into(acc_ref, act_scr, wdn_ref, *tail)
    y_ref[0] = x + mod_ref[5:6] * acc_ref[...]


def _ffn_weight_specs(layer):
    return [
        _layer_spec(layer, (1, D_MODEL)),
        _single_spec((D_MODEL, 2 * D_FF)),
        _layer_spec(layer, (CONV_W, 2 * D_FF)),
        _layer_spec(layer, (1, 2 * D_FF)),
        _single_spec((D_FF, D_MODEL)),
    ]


def _ffn_weight_args(pp, w_up, w_down):
    return [pp["g_ffn"], w_up, pp["conv_w"], pp["conv_b"], w_down]


def _ffn_prompt_call(layer, x, mod4, n_skip, pp, w_up, w_down, prev):
    b, t, _ = x.shape
    tt = FFN_TILE
    return _layer_call(
        _ffn_prompt_kernel, layer, prev,
        grid=(b, t // tt),
        in_specs=[pl.BlockSpec((1, tt, D_MODEL), lambda i, j: (i, j, 0)),
                  pl.BlockSpec((None, None, N_MOD, D_MODEL), lambda i, j: (layer, n_skip + i, 0, 0))]
        + _ffn_weight_specs(layer),
        args=[x, mod4] + _ffn_weight_args(pp, w_up, w_down),
        out_specs=[pl.BlockSpec((1, tt, D_MODEL), lambda i, j: (i, j, 0)),
                   pl.BlockSpec((None, 1, CONV_W - 1, 2 * D_FF), lambda i, j: (layer, i, 0, 0))],
        out_shapes=[jax.ShapeDtypeStruct((b, t, D_MODEL), F32),
                    jax.ShapeDtypeStruct((DEPTH, b, CONV_W - 1, 2 * D_FF), F32)],
        scratch=[
            pltpu.VMEM((SUBLANES, 2 * D_FF), F32),
            pltpu.VMEM((tt, D_FF), BF16),
            pltpu.VMEM((tt, D_MODEL), F32),
        ],
        name="ffn_prompt")


def _ffn_sample_kernel(x_ref, sh_ref, sc_ref, gf_ref, hist_ref, gffn_ref, wup_ref, cw_ref,
                       cb_ref, wdn_ref, y_ref, cout_ref, mod_scr, h1, h2, act_scr, acc_ref):
    n = x_ref.shape[0]
    sb = sh_ref.shape[0]
    ts = n // sb
    pad = SUBLANES
    _expand_rows(mod_scr.at[0], sh_ref[...], ts)
    _expand_rows(mod_scr.at[1], sc_ref[...], ts)
    x = x_ref[...]
    hb = _modulated_rmsnorm(x, gffn_ref[...], _read_cols(mod_scr.at[1]),
                            _read_cols(mod_scr.at[0])).astype(BF16)
    _expand_rows(mod_scr.at[0], gf_ref[...], ts)
    chunks = _ffn_chunks()
    after, tail = _down_plan(len(chunks))
    up = lambda halves: [_dot(hb, wup_ref[:, cols]) for cols in halves]
    ahead = up(chunks[0])
    for j, halves in enumerate(chunks):
        hus = ahead
        if j + 1 < len(chunks):
            ahead = up(chunks[j + 1])
        if j in after:
            _down_into(acc_ref, act_scr, wdn_ref, *after[j])
        convs = []
        for hu, cols in zip(hus, halves):
            for c in range(FFN_CHUNK // LANES):
                off = cols.start + c * LANES
                piece = hu[:, c * LANES:(c + 1) * LANES]
                h1[c, pad:, :] = piece
                h2[c, pad:, :] = piece
                cout_ref[:, 0, off:off + LANES] = h2[c, pl.ds(pad + ts - 2, sb, stride=ts), :]
                cout_ref[:, 1, off:off + LANES] = h2[c, pl.ds(pad + ts - 1, sb, stride=ts), :]
                c0 = hist_ref[:, 0, off:off + LANES]
                c1 = hist_ref[:, 1, off:off + LANES]
                h1[c, pl.ds(pad - 1, sb, stride=ts), :] = c1
                h2[c, pl.ds(pad - 2, sb, stride=ts), :] = c0
                h2[c, pl.ds(pad - 1, sb, stride=ts), :] = c1
            convs.append(_conv(hu, _read_cols(h1, slice(pad - 1, pad - 1 + n)),
                               _read_cols(h2, slice(pad - 2, pad - 2 + n)),
                               cw_ref[:, cols], cb_ref[:, cols]))
        act_scr[:, j * FFN_CHUNK:(j + 1) * FFN_CHUNK] = (
            jax.nn.silu(convs[0]) * convs[1]).astype(BF16)
    _down_into(acc_ref, act_scr, wdn_ref, *tail)
    y_ref[...] = x + _read_cols(mod_scr.at[0]) * acc_ref[...]


def _ffn_sample_call(layer, x, mod3, cconv, pp, w_up, w_down, ts, prev):
    nt = x.shape[0]
    sb = SAMPLE_SEQS
    n = sb * ts
    tok_spec = pl.BlockSpec((n, D_MODEL), lambda i: (i, 0))
    hist_spec = pl.BlockSpec((None, sb, CONV_W - 1, 2 * D_FF), lambda i: (layer, i, 0, 0))
    return _layer_call(
        _ffn_sample_kernel, layer, prev,
        grid=(nt // n,),
        in_specs=[tok_spec] + _sample_mod_specs(layer, sb, 3) + [hist_spec]
        + _ffn_weight_specs(layer),
        args=[x, mod3, mod3, mod3, cconv] + _ffn_weight_args(pp, w_up, w_down),
        out_specs=[tok_spec, hist_spec],
        out_shapes=[jax.ShapeDtypeStruct((nt, D_MODEL), F32),
                    jax.ShapeDtypeStruct(cconv.shape, F32)],
        scratch=[
            pltpu.VMEM((2, D_MODEL // LANES, n, LANES), F32),
            pltpu.VMEM((FFN_CHUNK // LANES, n + SUBLANES, LANES), F32),
            pltpu.VMEM((FFN_CHUNK // LANES, n + SUBLANES, LANES), F32),
            pltpu.VMEM((n, D_FF), BF16),
            pltpu.VMEM((n, D_MODEL), F32),
        ],
        name="ffn_sample")


def _rope_tables(pos):
    half = ROPE_DIMS // 2
    inv = ROPE_THETA ** (-jnp.arange(0, ROPE_DIMS, 2, dtype=F32) / ROPE_DIMS)
    ang = pos.astype(F32)[:, None] * inv[None, :]
    cos, sin = jnp.cos(ang), jnp.sin(ang)
    n = pos.shape[0]
    rest = jnp.zeros((n, HEAD_DIM - ROPE_DIMS), F32)
    zeros = jnp.zeros((n, half), F32)
    cos_t = jnp.concatenate([cos, cos, rest + 1.0], axis=1)
    sin_dn = jnp.concatenate([-sin, zeros, rest], axis=1)
    sin_up = jnp.concatenate([zeros, sin, rest], axis=1)
    return jnp.stack([jnp.tile(z, (1, LANES // HEAD_DIM)) for z in (cos_t, sin_dn, sin_up)])


def _block_diag_ones(width):
    idx = jnp.arange(width) // HEAD_DIM
    return (idx[:, None] == idx[None, :]).astype(BF16)


def _prepare_params(dec_seq, g_attn, w_in, g_q, g_k, sinks, ln_g, ln_b, w_s, b_s, w_out,
                    g_ffn, w_ffn_in, conv_w, conv_b, w_ffn_out):
    causal = jnp.tril(jnp.ones((CHUNK, CHUNK), dtype=bool))
    ws = jnp.where(causal, w_s, 0.0)
    seqs_per_chunk = CHUNK // dec_seq
    eye = jnp.eye(seqs_per_chunk, dtype=F32)
    ws_sample = jnp.einsum("ab,lgts->lgatbs", eye, ws[:, :, :dec_seq, :dec_seq]).reshape(w_s.shape)
    return {
        "g_attn": g_attn[:, None, :], "g_ffn": g_ffn[:, None, :],
        "w_in": w_in, "w_out": w_out,
        "g_q": jnp.tile(g_q, (1, N_HEADS))[:, None, :],
        "g_k": jnp.tile(g_k, (1, N_KV_HEADS))[:, None, :],
        "bd_q": _block_diag_ones(ATTN_WIDTH), "bd_k": _block_diag_ones(KV_WIDTH),
        "sinks": sinks,
        "ln_g": ln_g.reshape(DEPTH, 1, GMLP_WIDTH), "ln_b": ln_b.reshape(DEPTH, 1, GMLP_WIDTH),
        "ws_prompt": ws.astype(BF16), "bs_prompt": jnp.swapaxes(b_s, 1, 2),
        "ws_sample": ws_sample.astype(BF16),
        "bs_sample": jnp.swapaxes(jnp.tile(b_s[:, :, :dec_seq], (1, 1, seqs_per_chunk)), 1, 2),
        "w_up": w_ffn_in, "w_down": w_ffn_out,
        "conv_w": conv_w, "conv_b": conv_b[:, None, :],
    }


def kernel(x_prompt, x_sample, cache_k, cache_v, cache_conv, c_prompt, c_sample, w_ada, b_ada,
           g_attn, w_in, g_q, g_k, sinks, ln_g, ln_b, w_s, b_s, w_out, g_ffn, w_ffn_in, conv_w,
           conv_b, w_ffn_out):
    nbp, seq, _ = x_prompt.shape
    nbs, dec_seq, _ = x_sample.shape
    assert seq % PROMPT_TILE == 0 and PROMPT_TILE % CHUNK == 0 and nbs % SAMPLE_SEQS == 0
    assert seq % FFN_TILE == 0 and FFN_TILE % SUBLANES == 0 and D_FF % FFN_CHUNK == 0
    assert dec_seq == SUBLANES and CHUNK % dec_seq == 0 and cache_k.shape[2] == WINDOW
    assert nbs % SUBLANES == 0

    n_c = nbs + nbp
    c_all = jnp.concatenate([c_sample, c_prompt, jnp.zeros((-n_c % SUBLANES, D_MODEL), F32)])
    mod3 = _ada_call(c_all, w_ada, b_ada)
    mod4 = mod3.reshape(DEPTH, c_all.shape[0], N_MOD, D_MODEL)

    pp = _prepare_params(dec_seq, g_attn, w_in, g_q, g_k, sinks, ln_g, ln_b, w_s, b_s, w_out,
                         g_ffn, w_ffn_in, conv_w, conv_b, w_ffn_out)
    rope_p = _rope_tables(jnp.arange(seq, dtype=jnp.int32))
    rope_s = jnp.tile(_rope_tables(PAST_LEN + jnp.arange(dec_seq, dtype=jnp.int32)),
                      (1, SAMPLE_SEQS, 1))
    ck = cache_k.reshape(DEPTH, nbs, WINDOW, KV_WIDTH)
    cv = cache_v.reshape(DEPTH, nbs, WINDOW, KV_WIDTH)

    xp = x_prompt
    xs = x_sample.reshape(nbs * dec_seq, D_MODEL)
    mix_p = ffn_p = mix_s = ffn_s = None
    for l in range(DEPTH):
        xp, w_in_b, w_out_b, w_up_b, w_dn_b, *mix_p = _mix_prompt_call(
            l, xp, mod4, nbs, pp, rope_p, mix_p)
        xp, *ffn_p = _ffn_prompt_call(l, xp, mod4, nbs, pp, w_up_b, w_dn_b, ffn_p)
        xs, *mix_s = _mix_sample_call(l, xs, mod3, ck, cv, pp, w_in_b, w_out_b, rope_s, dec_seq,
                                      mix_s)
        xs, *ffn_s = _ffn_sample_call(l, xs, mod3, cache_conv, pp, w_up_b, w_dn_b, dec_seq, ffn_s)

    kv_p = (DEPTH, nbp, WINDOW, N_KV_HEADS, HEAD_DIM)
    kv_s = (DEPTH, nbs, dec_seq, N_KV_HEADS, HEAD_DIM)
    return (xp, xs.reshape(nbs, dec_seq, D_MODEL),
            mix_p[0].reshape(kv_p), mix_p[1].reshape(kv_p), ffn_p[0],
            mix_s[0].reshape(kv_s), mix_s[1].reshape(kv_s),
            mix_s[2].reshape(DEPTH, nbs, dec_seq, GMLP_WIDTH),
            ffn_s[0])
```

```python
import functools

import jax
import jax.numpy as jnp
from jax import lax
from jax.experimental import pallas as pl
from jax.experimental.pallas import tpu as pltpu

D_MODEL = 1024
DEPTH = 2
HEAD_DIM = 64
N_HEADS = 8
N_KV_HEADS = 2
KV_WIDTH = N_KV_HEADS * HEAD_DIM
WINDOW = 128
ROPE_THETA = 500000.0
ROPE_DIMS = HEAD_DIM // 4
ATTN_WIDTH = N_HEADS * HEAD_DIM
GMLP_WIDTH = D_MODEL - ATTN_WIDTH
GMLP_GROUPS = 4
GMLP_GW = GMLP_WIDTH // GMLP_GROUPS
CHUNK = 128
IN_WIDTH = ATTN_WIDTH + 2 * KV_WIDTH + 2 * GMLP_WIDTH
D_FF = 2816
CONV_W = 3
N_MOD = 6
EPS = 1e-6
NEG = -1e30
PAST_LEN = 16384

LANES = 128
SUBLANES = 8
N_PAIRS = N_HEADS // 2
VMEM_LIMIT = 56 * 1024 * 1024

PROMPT_TILE = 512
OUT_COLS = 256
MIX_INTERLEAVE = 14
FFN_TILE = 1024
SAMPLE_SEQS = 64
SEQ_UNROLL = 8
FFN_CHUNK = 256
DOWN_GROUP = 2
ADA_TILE = 1536

F32 = jnp.float32
BF16 = jnp.bfloat16


def _dot(a, b):
    return jnp.dot(a, b, preferred_element_type=F32)


def _dot_nt(a, b):
    return lax.dot_general(a, b, (((1,), (1,)), ((), ())), preferred_element_type=F32)


def _const_spec(shape):
    nd = len(shape)
    return pl.BlockSpec(shape, lambda *_: (0,) * nd)


def _layer_spec(layer, tail, single_buffer=False):
    nd = len(tail)
    kw = {"pipeline_mode": pl.Buffered(1)} if single_buffer else {}
    return pl.BlockSpec((None,) + tuple(tail), lambda *_: (layer,) + (0,) * nd, **kw)


def _skip_aliased(body, n_in, n_alias):
    if n_alias == 0:
        return body
    return lambda *refs: body(*refs[:n_in], *refs[n_in + n_alias:])


def _layer_call(body, layer, prev, *, grid, in_specs, args, out_specs, out_shapes, scratch, name,
                n_fresh=1):
    n_in = len(args)
    aliased = [] if prev is None else list(prev)
    return pl.pallas_call(
        _skip_aliased(body, n_in, len(aliased)),
        grid=grid,
        in_specs=list(in_specs) + [pl.BlockSpec(memory_space=pl.ANY)] * len(aliased),
        out_specs=out_specs,
        out_shape=out_shapes,
        input_output_aliases={n_in + i: n_fresh + i for i in range(len(aliased))},
        scratch_shapes=scratch,
        compiler_params=pltpu.CompilerParams(
            dimension_semantics=("arbitrary",) * len(grid), vmem_limit_bytes=VMEM_LIMIT),
        name=name,
    )(*args, *aliased)


def _ada_kernel(c_ref, w_ref, b_ref, o_ref):
    c = c_ref[...]
    a = (c * jax.nn.sigmoid(c)).astype(BF16)
    o_ref[0] = _dot(a, w_ref[0].astype(BF16)) + b_ref[0]


def _ada_call(c_all, w_ada, b_ada):
    n = c_all.shape[0]
    return pl.pallas_call(
        _ada_kernel,
        grid=(DEPTH, N_MOD * D_MODEL // ADA_TILE),
        in_specs=[
            pl.BlockSpec((n, D_MODEL), lambda l, j: (0, 0)),
            pl.BlockSpec((1, D_MODEL, ADA_TILE), lambda l, j: (l, 0, j)),
            pl.BlockSpec((1, 1, ADA_TILE), lambda l, j: (l, 0, j)),
        ],
        out_specs=pl.BlockSpec((1, n, ADA_TILE), lambda l, j: (l, 0, j)),
        out_shape=jax.ShapeDtypeStruct((DEPTH, n, N_MOD * D_MODEL), F32),
        compiler_params=pltpu.CompilerParams(
            dimension_semantics=("arbitrary", "arbitrary"), vmem_limit_bytes=VMEM_LIMIT),
        name="ada_mod",
    )(c_all, w_ada, b_ada.reshape(DEPTH, 1, N_MOD * D_MODEL))


def _modulated_rmsnorm(x, gain, scale, shift):
    r = lax.rsqrt(jnp.mean(x * x, axis=-1, keepdims=True) + EPS)
    return (x * r) * (gain * (1.0 + scale)) + shift


def _head_rmsnorm(z, ones_bd, gain):
    ssq = _dot((z * z).astype(BF16), ones_bd)
    return z * lax.rsqrt(ssq * (1.0 / HEAD_DIM) + EPS) * gain


def _rope(z, cos_t, sin_dn, sin_up):
    half = ROPE_DIMS // 2
    cols = []
    for p in range(z.shape[-1] // LANES):
        zp = z[:, p * LANES:(p + 1) * LANES]
        cols.append(zp * cos_t
                    + pltpu.roll(zp, LANES - half, axis=1) * sin_dn
                    + pltpu.roll(zp, half, axis=1) * sin_up)
    return cols[0] if len(cols) == 1 else jnp.concatenate(cols, axis=1)


def _split_heads(z):
    lane = lax.broadcasted_iota(jnp.int32, z.shape, 1)
    lo = lane < HEAD_DIM
    zs = pltpu.roll(z, HEAD_DIM, axis=1)
    zero = jnp.zeros_like(z)
    return (jnp.where(lo, z, zero), jnp.where(lo, zero, zs),
            jnp.where(lo, zs, zero), jnp.where(lo, zero, z))


def _gelu_layernorm(zv, ln_g, ln_b):
    gv = jax.nn.gelu(zv)
    cols = []
    for g in range(GMLP_GROUPS):
        xg = gv[:, g * GMLP_GW:(g + 1) * GMLP_GW]
        mu = jnp.mean(xg, axis=-1, keepdims=True)
        xc = xg - mu
        var = jnp.mean(xc * xc, axis=-1, keepdims=True)
        cols.append(xc * lax.rsqrt(var + EPS))
    return jnp.concatenate(cols, axis=1) * ln_g + ln_b


def _spatial_gate_into(mix_ref, u, vg, ws_ref, bs_ref):
    vgb = vg.astype(BF16)
    for g in range(GMLP_GROUPS):
        w = ws_ref[g]
        bias = bs_ref[:, g:g + 1]
        cs = slice(g * GMLP_GW, (g + 1) * GMLP_GW)
        for c in range(u.shape[0] // CHUNK):
            rs = slice(c * CHUNK, (c + 1) * CHUNK)
            z = _dot(w, vgb[rs, cs]) + bias
            mix_ref[rs, ATTN_WIDTH + g * GMLP_GW:ATTN_WIDTH + (g + 1) * GMLP_GW] = (
                u[rs, cs] * z).astype(BF16)


def _lane_pair(a, b, shape):
    lane = lax.broadcasted_iota(jnp.int32, shape, 1)
    return jnp.where(lane < HEAD_DIM, a, b)


def _prompt_mod_spec(layer, n_skip):
    return pl.BlockSpec((None, SUBLANES, N_MOD * D_MODEL),
                        lambda *_: (layer, n_skip // SUBLANES, 0))


def _mod_row(mod_ref, seq, m):
    return mod_ref[pl.ds(seq, 1), m * D_MODEL:(m + 1) * D_MODEL]


def _expand_rows(dst_ref, src, reps):
    for c in range(dst_ref.shape[0]):
        piece = src[:, c * LANES:(c + 1) * LANES]
        for t in range(reps):
            dst_ref[c, pl.ds(t, src.shape[0], stride=reps), :] = piece


def _read_cols(ref, rows=slice(None)):
    return jnp.concatenate([ref[c, rows, :] for c in range(ref.shape[0])], axis=1)


def _interleave(tasks, width):
    pending = iter(tasks)
    active = []
    while True:
        while len(active) < width:
            task = next(pending, None)
            if task is None:
                break
            active.append(task)
        if not active:
            return
        for task in list(active):
            if next(task, "done") == "done":
                active.remove(task)


def _mix_prompt_kernel(xa_ref, xc_ref, mod_ref, gattn_ref, win32_ref, gq_ref, gk_ref,
                       bdq_ref, bdk_ref, rope_ref, sinks_ref, lng_ref, lnb_ref, ws_ref, bs_ref,
                       wout32_ref, wup32_ref, wdn32_ref,
                       y_ref, win_ref, wout_ref, wup_ref, wdn_ref, kout_ref, vout_ref,
                       q_ring, k_ring, v_ring, u_ring, vg_ring, mix_ring,
                       *, layer, tiles_per_seq, n_tiles):
    tt = xa_ref.shape[1]
    g = pl.program_id(0)

    @pl.when(g == 0)
    def _():
        win_ref[...] = win32_ref[...].astype(BF16)
        wout_ref[...] = wout32_ref[...].astype(BF16)
        for ring in (q_ring, k_ring, v_ring, u_ring, vg_ring, mix_ring):
            ring[...] = jnp.zeros(ring.shape, ring.dtype)

    wup_ref[...] = wup32_ref[...].astype(BF16)
    wdn_ref[...] = wdn32_ref[...].astype(BF16)

    cur, prv = g % 2, (g + 1) % 2
    seq_a = jnp.minimum(g, n_tiles - 1) // tiles_per_seq
    seq_c = jnp.clip(g - 2, 0, n_tiles - 1) // tiles_per_seq
    kv_a, kv_b, kv_p = g % 3, (g + 2) % 3, (g + 1) % 3

    cos_t, sin_dn, sin_up = rope_ref[0], rope_ref[1], rope_ref[2]
    env = {}

    def norm_task():
        env["hb"] = _modulated_rmsnorm(
            xa_ref[0], gattn_ref[...], _mod_row(mod_ref, seq_a, 1),
            _mod_row(mod_ref, seq_a, 0)).astype(BF16)
        yield

    def q_task():
        q = _dot(env["hb"], win_ref[:, :ATTN_WIDTH])
        yield
        ssq = _dot((q * q).astype(BF16), bdq_ref[...])
        yield
        q = q * lax.rsqrt(ssq * (1.0 / HEAD_DIM) + EPS) * gq_ref[...]
        q_ring[cur] = (_rope(q, cos_t, sin_dn, sin_up) * (HEAD_DIM ** -0.5)).astype(BF16)

    def kv_task():
        kv = _dot(env["hb"], win_ref[:, ATTN_WIDTH:ATTN_WIDTH + 2 * KV_WIDTH])
        yield
        k, v = kv[:, :KV_WIDTH], kv[:, KV_WIDTH:]
        ssq = _dot((k * k).astype(BF16), bdk_ref[...])
        vout_ref[0] = v[tt - WINDOW:, :]
        for i, vz in enumerate(_split_heads(v)):
            v_ring[kv_a, i] = vz.astype(BF16)
        yield
        k = _rope(k * lax.rsqrt(ssq * (1.0 / HEAD_DIM) + EPS) * gk_ref[...], cos_t, sin_dn, sin_up)
        kout_ref[0] = k[tt - WINDOW:, :]
        for i, kz in enumerate(_split_heads(k)):
            k_ring[kv_a, i] = kz.astype(BF16)

    def u_task():
        zu = _dot(env["hb"], win_ref[:, ATTN_WIDTH + 2 * KV_WIDTH:IN_WIDTH - GMLP_WIDTH])
        yield
        u_ring[cur] = jax.nn.gelu(zu)

    def vg_task():
        zv = _dot(env["hb"], win_ref[:, IN_WIDTH - GMLP_WIDTH:])
        yield
        vg_ring[cur] = _gelu_layernorm(zv, lng_ref[...], lnb_ref[...]).astype(BF16)

    row = lax.broadcasted_iota(jnp.int32, (WINDOW, 2 * WINDOW), 0)
    col = lax.broadcasted_iota(jnp.int32, (WINDOW, 2 * WINDOW), 1)
    band = (col - row >= 1) & (col - row <= WINDOW)
    seq_start = (g + tiles_per_seq - 1) % tiles_per_seq == 0
    band_first = band & (col >= jnp.where(seq_start, WINDOW, 0))
    tail = slice(tt - WINDOW, tt)

    def band_rows(ring, idx, i):
        if i == 0:
            return [ring[kv_p, idx, tail, :], ring[kv_b, idx, 0:WINDOW, :]]
        return [ring[kv_b, idx, (i - 1) * WINDOW:(i + 1) * WINDOW, :]]

    def attn_task(i, kv):
        mask = band_first if i == 0 else band
        rows = slice(i * WINDOW, (i + 1) * WINDOW)
        pairs = (2 * kv, 2 * kv + 1)
        kblk = jnp.concatenate(band_rows(k_ring, 2 * kv, i) + band_rows(k_ring, 2 * kv + 1, i), axis=0)
        q2 = jnp.concatenate([q_ring[prv, rows, p * LANES:(p + 1) * LANES] for p in pairs], axis=0)
        s = _dot_nt(q2, kblk)
        yield
        es, scales = [], []
        for j, p in enumerate(pairs):
            e_pair, invs = [], []
            for hh in range(2):
                sink = sinks_ref[layer, 2 * p + hh]
                sh_ = jnp.where(mask, s[j * WINDOW:(j + 1) * WINDOW,
                                        hh * 2 * WINDOW:(hh + 1) * 2 * WINDOW], NEG)
                m = jnp.maximum(jnp.max(sh_, axis=-1, keepdims=True), sink)
                e = jnp.exp(sh_ - m)
                e_pair.append(e.astype(BF16))
                invs.append(1.0 / (jnp.sum(e, axis=-1, keepdims=True) + jnp.exp(sink - m)))
            es.append(jnp.concatenate(e_pair, axis=1))
            scales.append(_lane_pair(invs[0], invs[1], (WINDOW, LANES)))
        vblk = jnp.concatenate(band_rows(v_ring, 2 * kv, i) + band_rows(v_ring, 2 * kv + 1, i), axis=0)
        o = _dot(jnp.concatenate(es, axis=0), vblk)
        yield
        for j, p in enumerate(pairs):
            mix_ring[cur, rows, p * LANES:(p + 1) * LANES] = (
                o[j * WINDOW:(j + 1) * WINDOW] * scales[j]).astype(BF16)

    def gate_task(c, grp):
        rs = slice(c * CHUNK, (c + 1) * CHUNK)
        cs = slice(grp * GMLP_GW, (grp + 1) * GMLP_GW)
        z = _dot(ws_ref[grp], vg_ring[prv, rs, cs])
        yield
        z = z + bs_ref[:, grp:grp + 1]
        mix_ring[cur, rs, ATTN_WIDTH + grp * GMLP_GW:ATTN_WIDTH + (grp + 1) * GMLP_GW] = (
            u_ring[prv, rs, cs] * z).astype(BF16)

    def out_task(c):
        cols = slice(c * OUT_COLS, (c + 1) * OUT_COLS)
        r = _dot(mix_ring[prv], wout_ref[:, cols])
        yield
        y_ref[0, :, cols] = xc_ref[0, :, cols] + _mod_row(mod_ref, seq_c, 2)[:, cols] * r

    n_blocks = tt // WINDOW
    attn = [attn_task(i, kv) for i in range(n_blocks) for kv in range(N_KV_HEADS)]
    gate = [gate_task(c, grp) for c in range(n_blocks) for grp in range(GMLP_GROUPS)]
    big = [out_task(0), norm_task(), vg_task(), out_task(1), u_task(), q_task(), out_task(2),
           kv_task(), out_task(3)]
    small = [t for trio in zip(attn, gate[0::2], gate[1::2]) for t in trio]
    per_big = -(-len(small) // len(big))
    order = []
    for b in big:
        order.append(b)
        order.extend(small[:per_big])
        small = small[per_big:]
    _interleave(order + small, MIX_INTERLEAVE)


def _mix_weight_specs(layer, w_in_spec, w_out_spec):
    return [
        _layer_spec(layer, (1, D_MODEL)),
        w_in_spec,
        _layer_spec(layer, (1, ATTN_WIDTH)),
        _layer_spec(layer, (1, KV_WIDTH)),
        _const_spec((ATTN_WIDTH, ATTN_WIDTH)),
        _const_spec((KV_WIDTH, KV_WIDTH)),
    ], [
        pl.BlockSpec(memory_space=pltpu.SMEM),
        _layer_spec(layer, (1, GMLP_WIDTH)),
        _layer_spec(layer, (1, GMLP_WIDTH)),
        _layer_spec(layer, (GMLP_GROUPS, CHUNK, CHUNK)),
        _layer_spec(layer, (CHUNK, GMLP_GROUPS)),
        w_out_spec,
    ]


def _mix_weight_args(pp, kind, w_in, w_out):
    return ([pp["g_attn"], w_in, pp["g_q"], pp["g_k"], pp["bd_q"], pp["bd_k"]],
            [pp["sinks"], pp["ln_g"], pp["ln_b"], pp["ws_" + kind], pp["bs_" + kind], w_out])


def _mix_prompt_call(layer, x, mod3, n_skip, pp, rope, prev):
    b, t, _ = x.shape
    tt = PROMPT_TILE
    nt = t // tt
    last = b * nt - 1
    w_specs_a, w_specs_b = _mix_weight_specs(
        layer, _layer_spec(layer, (D_MODEL, IN_WIDTH), single_buffer=True),
        _layer_spec(layer, (D_MODEL, D_MODEL), single_buffer=True))
    w_args_a, w_args_b = _mix_weight_args(pp, "prompt", pp["w_in"], pp["w_out"])
    up_rows, dn_rows = D_MODEL // (b * nt), D_FF // (b * nt // 2)
    assert up_rows % 16 == 0 and dn_rows % 16 == 0

    def tile(g, lag):
        return jnp.clip(g - lag, 0, last)

    def slab_specs(rows, n_slabs, width, src_layer):
        idx = lambda g: jnp.minimum(g, n_slabs - 1)
        return (pl.BlockSpec((None, rows, width), lambda g: (src_layer, idx(g), 0)),
                pl.BlockSpec((rows, width), lambda g: (idx(g), 0)))

    up_in, up_out = slab_specs(up_rows, D_MODEL // up_rows, 2 * D_FF, layer)
    dn_in, dn_out = slab_specs(dn_rows, D_FF // dn_rows, D_MODEL, layer)

    def x_spec(lag):
        return pl.BlockSpec((1, tt, D_MODEL), lambda g: (tile(g, lag) // nt, tile(g, lag) % nt, 0))

    kv_spec = pl.BlockSpec((None, 1, WINDOW, KV_WIDTH), lambda g: (layer, tile(g, 0) // nt, 0, 0))
    kv_shape = jax.ShapeDtypeStruct((DEPTH, b, WINDOW, KV_WIDTH), F32)
    return _layer_call(
        functools.partial(_mix_prompt_kernel, layer=layer, tiles_per_seq=nt, n_tiles=b * nt),
        layer, prev,
        grid=(b * nt + 2,),
        in_specs=[x_spec(0), x_spec(2), _prompt_mod_spec(layer, n_skip)] + w_specs_a
        + [pl.BlockSpec((3, tt, LANES), lambda g: (0, tile(g, 0) % nt, 0))] + w_specs_b
        + [up_in, dn_in],
        args=[x, x, mod3] + w_args_a + [rope] + w_args_b + [pp["w_up"], pp["w_down"]],
        out_specs=[x_spec(2), _const_spec((D_MODEL, IN_WIDTH)), _const_spec((D_MODEL, D_MODEL)),
                   up_out, dn_out, kv_spec, kv_spec],
        out_shapes=[jax.ShapeDtypeStruct((b, t, D_MODEL), F32),
                    jax.ShapeDtypeStruct((D_MODEL, IN_WIDTH), BF16),
                    jax.ShapeDtypeStruct((D_MODEL, D_MODEL), BF16),
                    jax.ShapeDtypeStruct((D_MODEL, 2 * D_FF), BF16),
                    jax.ShapeDtypeStruct((D_FF, D_MODEL), BF16), kv_shape, kv_shape],
        n_fresh=5,
        scratch=[
            pltpu.VMEM((2, tt, ATTN_WIDTH), BF16),
            pltpu.VMEM((3, 4, tt, LANES), BF16),
            pltpu.VMEM((3, 4, tt, LANES), BF16),
            pltpu.VMEM((2, tt, GMLP_WIDTH), F32),
            pltpu.VMEM((2, tt, GMLP_WIDTH), BF16),
            pltpu.VMEM((2, tt, D_MODEL), BF16),
        ],
        name="mix_prompt")


def _mix_sample_kernel(x_ref, sh_ref, sc_ref, ga_ref, ck_ref, cv_ref, gattn_ref, win_ref, gq_ref,
                       gk_ref, bdq_ref, bdk_ref, rope_ref, sinks_ref, lng_ref, lnb_ref, ws_ref,
                       bs_ref, wout_ref, y_ref, kout_ref, vout_ref, vgout_ref,
                       mod_scr, q_scr, s_scr, mix_ref, *, layer):
    n = x_ref.shape[0]
    sb = sh_ref.shape[0]
    ts = n // sb
    _expand_rows(mod_scr.at[0], sh_ref[...], ts)
    _expand_rows(mod_scr.at[1], sc_ref[...], ts)
    x = x_ref[...]
    h = _modulated_rmsnorm(x, gattn_ref[...], _read_cols(mod_scr.at[1]), _read_cols(mod_scr.at[0]))
    proj = _dot(h.astype(BF16), win_ref[...])
    _expand_rows(mod_scr.at[0], ga_ref[...], ts)

    cos_t, sin_dn, sin_up = rope_ref[0], rope_ref[1], rope_ref[2]
    q = _rope(_head_rmsnorm(proj[:, :ATTN_WIDTH], bdq_ref[...], gq_ref[...]), cos_t, sin_dn, sin_up)
    q = q * (HEAD_DIM ** -0.5)
    k = _rope(_head_rmsnorm(proj[:, ATTN_WIDTH:ATTN_WIDTH + KV_WIDTH], bdk_ref[...], gk_ref[...]),
              cos_t, sin_dn, sin_up)
    v = proj[:, ATTN_WIDTH + KV_WIDTH:ATTN_WIDTH + 2 * KV_WIDTH]
    kout_ref[...] = k
    vout_ref[...] = v

    lane = lax.broadcasted_iota(jnp.int32, (n, LANES), 1)
    group = N_HEADS // N_KV_HEADS
    qh = []
    for hd in range(N_HEADS):
        piece = q[:, (hd // 2) * LANES:(hd // 2 + 1) * LANES]
        if hd % 2 != hd // group:
            piece = pltpu.roll(piece, HEAD_DIM, axis=1)
        keep = (lane < HEAD_DIM) if hd // group == 0 else (lane >= HEAD_DIM)
        piece = jnp.where(keep, piece, 0.0)
        q_scr[:, hd * ts:(hd + 1) * ts, :] = piece.reshape(sb, ts, LANES)
        qh.append(piece.astype(BF16))

    def cache_scores(b, carry):
        s_scr[b] = _dot_nt(q_scr[b].astype(BF16), ck_ref[b].astype(BF16))
        return carry

    lax.fori_loop(0, sb, cache_scores, 0, unroll=SEQ_UNROLL)

    kb, vb = k.astype(BF16), v.astype(BF16)
    grp = CHUNK
    rown = lax.broadcasted_iota(jnp.int32, (grp, grp), 0)
    coln = lax.broadcasted_iota(jnp.int32, (grp, grp), 1)
    seq_bits = ts.bit_length() - 1
    mask_new = ((rown >> seq_bits) == (coln >> seq_bits)) & (coln <= rown)
    mask_cache = (lax.broadcasted_iota(jnp.int32, (grp, WINDOW), 1)
                  > (lax.broadcasted_iota(jnp.int32, (grp, WINDOW), 0) & (ts - 1)))

    for g0 in range(0, n, grp):
        rows = slice(g0, g0 + grp)
        seqs = slice(g0 // ts, (g0 + grp) // ts)
        s_new = _dot_nt(jnp.concatenate([z[rows] for z in qh], axis=0), kb[rows])
        e_new = []
        for hd in range(N_HEADS):
            sink = sinks_ref[layer, hd]
            hrows = slice(hd * ts, (hd + 1) * ts)
            sn = jnp.where(mask_new, s_new[hd * grp:(hd + 1) * grp], NEG)
            sc_ = jnp.where(mask_cache, s_scr[seqs, hrows, :].reshape(grp, WINDOW), NEG)
            m = jnp.maximum(jnp.maximum(jnp.max(sn, axis=-1, keepdims=True),
                                        jnp.max(sc_, axis=-1, keepdims=True)), sink)
            en = jnp.exp(sn - m)
            ec = jnp.exp(sc_ - m)
            inv = 1.0 / (jnp.sum(en, axis=-1, keepdims=True) + jnp.sum(ec, axis=-1, keepdims=True)
                         + jnp.exp(sink - m))
            e_new.append((en * inv).astype(BF16))
            s_scr[seqs, hrows, :] = (ec * inv).reshape(grp // ts, ts, WINDOW)
        o_new = _dot(jnp.concatenate(e_new, axis=0), vb[rows])
        for hd in range(N_HEADS):
            q_scr[seqs, hd * ts:(hd + 1) * ts, :] = o_new[hd * grp:(hd + 1) * grp].reshape(
                grp // ts, ts, LANES)

    def cache_pv(b, carry):
        q_scr[b] += _dot(s_scr[b].astype(BF16), cv_ref[b].astype(BF16))
        return carry

    lax.fori_loop(0, sb, cache_pv, 0, unroll=SEQ_UNROLL)

    for p in range(N_PAIRS):
        halves = []
        for hd in (2 * p, 2 * p + 1):
            o = q_scr[:, hd * ts:(hd + 1) * ts, :].reshape(n, LANES)
            if hd % 2 != hd // group:
                o = pltpu.roll(o, HEAD_DIM, axis=1)
            halves.append(o)
        mix_ref[:, p * LANES:(p + 1) * LANES] = jnp.where(
            lane < HEAD_DIM, halves[0], halves[1]).astype(BF16)

    u = jax.nn.gelu(proj[:, ATTN_WIDTH + 2 * KV_WIDTH:ATTN_WIDTH + 2 * KV_WIDTH + GMLP_WIDTH])
    vg = _gelu_layernorm(proj[:, IN_WIDTH - GMLP_WIDTH:], lng_ref[...], lnb_ref[...])
    vgout_ref[...] = vg
    _spatial_gate_into(mix_ref, u, vg, ws_ref, bs_ref)

    y_ref[...] = x + _read_cols(mod_scr.at[0]) * _dot(mix_ref[...], wout_ref[...])


def _sample_mod_specs(layer, sb, first):
    return [pl.BlockSpec((None, sb, D_MODEL), functools.partial(lambda i, w: (layer, i, w), w=first + m))
            for m in range(3)]


def _single_spec(shape):
    nd = len(shape)
    return pl.BlockSpec(shape, lambda *_: (0,) * nd, pipeline_mode=pl.Buffered(1))


def _mix_sample_call(layer, x, mod3, cache_k, cache_v, pp, w_in, w_out, rope, ts, prev):
    nt = x.shape[0]
    sb = SAMPLE_SEQS
    n = sb * ts
    w_specs_a, w_specs_b = _mix_weight_specs(
        layer, _single_spec((D_MODEL, IN_WIDTH)), _single_spec((D_MODEL, D_MODEL)))
    w_args_a, w_args_b = _mix_weight_args(pp, "sample", w_in, w_out)
    tok_spec = lambda w: pl.BlockSpec((None, n, w), lambda i: (layer, i, 0))
    tok_shape = lambda w: jax.ShapeDtypeStruct((DEPTH, nt, w), F32)
    cache_spec = pl.BlockSpec((None, sb, WINDOW, KV_WIDTH), lambda i: (layer, i, 0, 0))
    return _layer_call(
        functools.partial(_mix_sample_kernel, layer=layer), layer, prev,
        grid=(nt // n,),
        in_specs=[pl.BlockSpec((n, D_MODEL), lambda i: (i, 0))] + _sample_mod_specs(layer, sb, 0)
        + [cache_spec, cache_spec] + w_specs_a + [_const_spec((3, n, LANES))] + w_specs_b,
        args=[x, mod3, mod3, mod3, cache_k, cache_v] + w_args_a + [rope] + w_args_b,
        out_specs=[pl.BlockSpec((n, D_MODEL), lambda i: (i, 0)),
                   tok_spec(KV_WIDTH), tok_spec(KV_WIDTH), tok_spec(GMLP_WIDTH)],
        out_shapes=[jax.ShapeDtypeStruct((nt, D_MODEL), F32),
                    tok_shape(KV_WIDTH), tok_shape(KV_WIDTH), tok_shape(GMLP_WIDTH)],
        scratch=[
            pltpu.VMEM((2, D_MODEL // LANES, n, LANES), F32),
            pltpu.VMEM((sb, N_HEADS * ts, LANES), F32),
            pltpu.VMEM((sb, N_HEADS * ts, WINDOW), F32),
            pltpu.VMEM((n, D_MODEL), BF16),
        ],
        name="mix_sample")


def _shifted_rows(hu, before):
    rows, c = hu.shape
    hu3 = hu.reshape(rows // SUBLANES, SUBLANES, c)
    above = jnp.concatenate([before[None], hu3[:-1]], axis=0)
    sub = lax.broadcasted_iota(jnp.int32, (1, SUBLANES, c), 1)
    p1 = pltpu.roll(jnp.where(sub >= SUBLANES - 1, above, hu3), 1, axis=1)
    p2 = pltpu.roll(jnp.where(sub >= SUBLANES - 2, above, hu3), 2, axis=1)
    return p1.reshape(rows, c), p2.reshape(rows, c)


def _conv(hu, prev1, prev2, cw, cb):
    return cb + cw[0:1] * prev2 + cw[1:2] * prev1 + cw[2:3] * hu


def _ffn_chunks():
    return [(slice(j, j + FFN_CHUNK), slice(D_FF + j, D_FF + j + FFN_CHUNK))
            for j in range(0, D_FF, FFN_CHUNK)]


def _down_plan(n_chunks):
    bounds = list(range(0, n_chunks, DOWN_GROUP)) + [n_chunks]
    groups = [(a * FFN_CHUNK, b * FFN_CHUNK) for a, b in zip(bounds[:-1], bounds[1:])]
    return {b // FFN_CHUNK: (a, b) for a, b in groups[:-1]}, groups[-1]


def _down_into(acc_ref, act_scr, wdn_ref, a, b):
    contrib = _dot(act_scr[:, a:b], wdn_ref[a:b, :])
    if a == 0:
        acc_ref[...] = contrib
    else:
        acc_ref[...] += contrib


def _ffn_prompt_kernel(x_ref, mod_ref, gffn_ref, wup_ref, cw_ref, cb_ref, wdn_ref,
                       y_ref, cout_ref, carry, act_scr, acc_ref):
    tt = x_ref.shape[1]
    t = pl.program_id(1)

    @pl.when(t == 0)
    def _():
        carry[...] = jnp.zeros(carry.shape, F32)

    x = x_ref[0]
    seq = pl.program_id(0)
    hb = _modulated_rmsnorm(x, gffn_ref[...], _mod_row(mod_ref, seq, 4),
                            _mod_row(mod_ref, seq, 3)).astype(BF16)
    chunks = _ffn_chunks()
    after, tail = _down_plan(len(chunks))
    up = lambda halves: [_dot(hb, wup_ref[:, cols]) for cols in halves]
    ahead = up(chunks[0])
    for j, halves in enumerate(chunks):
        hus = ahead
        if j + 1 < len(chunks):
            ahead = up(chunks[j + 1])
        if j in after:
            _down_into(acc_ref, act_scr, wdn_ref, *after[j])
        convs = []
        for hu, cols in zip(hus, halves):
            prev1, prev2 = _shifted_rows(hu, carry[:, cols])
            carry[:, cols] = hu[tt - SUBLANES:, :]
            cout_ref[0, :, cols] = hu[tt - (CONV_W - 1):, :]
            convs.append(_conv(hu, prev1, prev2, cw_ref[:, cols], cb_ref[:, cols]))
        act_scr[:, j * FFN_CHUNK:(j + 1) * FFN_CHUNK] = (
            jax.nn.silu(convs[0]) * convs[1]).astype(BF16)
    _down_into(acc_ref, act_scr, wdn_ref, *tail)
    y_ref[0] = x + _mod_row(mod_ref, seq, 5) * acc_ref[...]


def _ffn_weight_specs(layer):
    return [
        _layer_spec(layer, (1, D_MODEL)),
        _single_spec((D_MODEL, 2 * D_FF)),
        _layer_spec(layer, (CONV_W, 2 * D_FF)),
        _layer_spec(layer, (1, 2 * D_FF)),
        _single_spec((D_FF, D_MODEL)),
    ]


def _ffn_weight_args(pp, w_up, w_down):
    return [pp["g_ffn"], w_up, pp["conv_w"], pp["conv_b"], w_down]


def _ffn_prompt_call(layer, x, mod3, n_skip, pp, w_up, w_down, prev):
    b, t, _ = x.shape
    tt = FFN_TILE
    return _layer_call(
        _ffn_prompt_kernel, layer, prev,
        grid=(b, t // tt),
        in_specs=[pl.BlockSpec((1, tt, D_MODEL), lambda i, j: (i, j, 0)),
                  _prompt_mod_spec(layer, n_skip)]
        + _ffn_weight_specs(layer),
        args=[x, mod3] + _ffn_weight_args(pp, w_up, w_down),
        out_specs=[pl.BlockSpec((1, tt, D_MODEL), lambda i, j: (i, j, 0)),
                   pl.BlockSpec((None, 1, CONV_W - 1, 2 * D_FF), lambda i, j: (layer, i, 0, 0))],
        out_shapes=[jax.ShapeDtypeStruct((b, t, D_MODEL), F32),
                    jax.ShapeDtypeStruct((DEPTH, b, CONV_W - 1, 2 * D_FF), F32)],
        scratch=[
            pltpu.VMEM((SUBLANES, 2 * D_FF), F32),
            pltpu.VMEM((tt, D_FF), BF16),
            pltpu.VMEM((tt, D_MODEL), F32),
        ],
        name="ffn_prompt")


def _ffn_sample_kernel(x_ref, sh_ref, sc_ref, gf_ref, hist_ref, gffn_ref, wup_ref, cw_ref,
                       cb_ref, wdn_ref, y_ref, cout_ref, mod_scr, h1, h2, act_scr, acc_ref):
    n = x_ref.shape[0]
    sb = sh_ref.shape[0]
    ts = n // sb
    pad = SUBLANES
    _expand_rows(mod_scr.at[0], sh_ref[...], ts)
    _expand_rows(mod_scr.at[1], sc_ref[...], ts)
    x = x_ref[...]
    hb = _modulated_rmsnorm(x, gffn_ref[...], _read_cols(mod_scr.at[1]),
                            _read_cols(mod_scr.at[0])).astype(BF16)
    _expand_rows(mod_scr.at[0], gf_ref[...], ts)
    chunks = _ffn_chunks()
    after, tail = _down_plan(len(chunks))
    up = lambda halves: [_dot(hb, wup_ref[:, cols]) for cols in halves]
    ahead = up(chunks[0])
    for j, halves in enumerate(chunks):
        hus = ahead
        if j + 1 < len(chunks):
            ahead = up(chunks[j + 1])
        if j in after:
            _down_into(acc_ref, act_scr, wdn_ref, *after[j])
        convs = []
        for hu, cols in zip(hus, halves):
            for c in range(FFN_CHUNK // LANES):
                off = cols.start + c * LANES
                piece = hu[:, c * LANES:(c + 1) * LANES]
                h1[c, pad:, :] = piece
                h2[c, pad:, :] = piece
                cout_ref[:, 0, off:off + LANES] = h2[c, pl.ds(pad + ts - 2, sb, stride=ts), :]
                cout_ref[:, 1, off:off + LANES] = h2[c, pl.ds(pad + ts - 1, sb, stride=ts), :]
                c0 = hist_ref[:, 0, off:off + LANES]
                c1 = hist_ref[:, 1, off:off + LANES]
                h1[c, pl.ds(pad - 1, sb, stride=ts), :] = c1
                h2[c, pl.ds(pad - 2, sb, stride=ts), :] = c0
                h2[c, pl.ds(pad - 1, sb, stride=ts), :] = c1
            convs.append(_conv(hu, _read_cols(h1, slice(pad - 1, pad - 1 + n)),
                               _read_cols(h2, slice(pad - 2, pad - 2 + n)),
                               cw_ref[:, cols], cb_ref[:, cols]))
        act_scr[:, j * FFN_CHUNK:(j + 1) * FFN_CHUNK] = (
            jax.nn.silu(convs[0]) * convs[1]).astype(BF16)
    _down_into(acc_ref, act_scr, wdn_ref, *tail)
    y_ref[...] = x + _read_cols(mod_scr.at[0]) * acc_ref[...]


def _ffn_sample_call(layer, x, mod3, cconv, pp, w_up, w_down, ts, prev):
    nt = x.shape[0]
    sb = SAMPLE_SEQS
    n = sb * ts
    tok_spec = pl.BlockSpec((n, D_MODEL), lambda i: (i, 0))
    hist_spec = pl.BlockSpec((None, sb, CONV_W - 1, 2 * D_FF), lambda i: (layer, i, 0, 0))
    return _layer_call(
        _ffn_sample_kernel, layer, prev,
        grid=(nt // n,),
        in_specs=[tok_spec] + _sample_mod_specs(layer, sb, 3) + [hist_spec]
        + _ffn_weight_specs(layer),
        args=[x, mod3, mod3, mod3, cconv] + _ffn_weight_args(pp, w_up, w_down),
        out_specs=[tok_spec, hist_spec],
        out_shapes=[jax.ShapeDtypeStruct((nt, D_MODEL), F32),
                    jax.ShapeDtypeStruct(cconv.shape, F32)],
        scratch=[
            pltpu.VMEM((2, D_MODEL // LANES, n, LANES), F32),
            pltpu.VMEM((FFN_CHUNK // LANES, n + SUBLANES, LANES), F32),
            pltpu.VMEM((FFN_CHUNK // LANES, n + SUBLANES, LANES), F32),
            pltpu.VMEM((n, D_FF), BF16),
            pltpu.VMEM((n, D_MODEL), F32),
        ],
        name="ffn_sample")


def _rope_tables(pos):
    half = ROPE_DIMS // 2
    inv = ROPE_THETA ** (-jnp.arange(0, ROPE_DIMS, 2, dtype=F32) / ROPE_DIMS)
    ang = pos.astype(F32)[:, None] * inv[None, :]
    cos, sin = jnp.cos(ang), jnp.sin(ang)
    n = pos.shape[0]
    rest = jnp.zeros((n, HEAD_DIM - ROPE_DIMS), F32)
    zeros = jnp.zeros((n, half), F32)
    cos_t = jnp.concatenate([cos, cos, rest + 1.0], axis=1)
    sin_dn = jnp.concatenate([-sin, zeros, rest], axis=1)
    sin_up = jnp.concatenate([zeros, sin, rest], axis=1)
    return jnp.stack([jnp.tile(z, (1, LANES // HEAD_DIM)) for z in (cos_t, sin_dn, sin_up)])


def _block_diag_ones(width):
    idx = jnp.arange(width) // HEAD_DIM
    return (idx[:, None] == idx[None, :]).astype(BF16)


def _prepare_params(dec_seq, g_attn, w_in, g_q, g_k, sinks, ln_g, ln_b, w_s, b_s, w_out,
                    g_ffn, w_ffn_in, conv_w, conv_b, w_ffn_out):
    causal = jnp.tril(jnp.ones((CHUNK, CHUNK), dtype=bool))
    ws = jnp.where(causal, w_s, 0.0)
    seqs_per_chunk = CHUNK // dec_seq
    eye = jnp.eye(seqs_per_chunk, dtype=F32)
    ws_sample = jnp.einsum("ab,lgts->lgatbs", eye, ws[:, :, :dec_seq, :dec_seq]).reshape(w_s.shape)
    return {
        "g_attn": g_attn[:, None, :], "g_ffn": g_ffn[:, None, :],
        "w_in": w_in, "w_out": w_out,
        "g_q": jnp.tile(g_q, (1, N_HEADS))[:, None, :],
        "g_k": jnp.tile(g_k, (1, N_KV_HEADS))[:, None, :],
        "bd_q": _block_diag_ones(ATTN_WIDTH), "bd_k": _block_diag_ones(KV_WIDTH),
        "sinks": sinks,
        "ln_g": ln_g.reshape(DEPTH, 1, GMLP_WIDTH), "ln_b": ln_b.reshape(DEPTH, 1, GMLP_WIDTH),
        "ws_prompt": ws.astype(BF16), "bs_prompt": jnp.swapaxes(b_s, 1, 2),
        "ws_sample": ws_sample.astype(BF16),
        "bs_sample": jnp.swapaxes(jnp.tile(b_s[:, :, :dec_seq], (1, 1, seqs_per_chunk)), 1, 2),
        "w_up": w_ffn_in, "w_down": w_ffn_out,
        "conv_w": conv_w, "conv_b": conv_b[:, None, :],
    }


def kernel(x_prompt, x_sample, cache_k, cache_v, cache_conv, c_prompt, c_sample, w_ada, b_ada,
           g_attn, w_in, g_q, g_k, sinks, ln_g, ln_b, w_s, b_s, w_out, g_ffn, w_ffn_in, conv_w,
           conv_b, w_ffn_out):
    nbp, seq, _ = x_prompt.shape
    nbs, dec_seq, _ = x_sample.shape
    assert seq % PROMPT_TILE == 0 and PROMPT_TILE % CHUNK == 0 and nbs % SAMPLE_SEQS == 0
    assert seq % FFN_TILE == 0 and FFN_TILE % SUBLANES == 0 and D_FF % FFN_CHUNK == 0
    assert dec_seq == SUBLANES and CHUNK % dec_seq == 0 and cache_k.shape[2] == WINDOW
    assert nbs % SUBLANES == 0 and nbp <= SUBLANES

    n_c = nbs + nbp
    c_all = jnp.concatenate([c_sample, c_prompt, jnp.zeros((-n_c % SUBLANES, D_MODEL), F32)])
    mod3 = _ada_call(c_all, w_ada, b_ada)

    pp = _prepare_params(dec_seq, g_attn, w_in, g_q, g_k, sinks, ln_g, ln_b, w_s, b_s, w_out,
                         g_ffn, w_ffn_in, conv_w, conv_b, w_ffn_out)
    rope_p = _rope_tables(jnp.arange(seq, dtype=jnp.int32))
    rope_s = jnp.tile(_rope_tables(PAST_LEN + jnp.arange(dec_seq, dtype=jnp.int32)),
                      (1, SAMPLE_SEQS, 1))
    ck = cache_k.reshape(DEPTH, nbs, WINDOW, KV_WIDTH)
    cv = cache_v.reshape(DEPTH, nbs, WINDOW, KV_WIDTH)

    xp = x_prompt
    xs = x_sample.reshape(nbs * dec_seq, D_MODEL)
    mix_p = ffn_p = mix_s = ffn_s = None
    for l in range(DEPTH):
        xp, w_in_b, w_out_b, w_up_b, w_dn_b, *mix_p = _mix_prompt_call(
            l, xp, mod3, nbs, pp, rope_p, mix_p)
        xp, *ffn_p = _ffn_prompt_call(l, xp, mod3, nbs, pp, w_up_b, w_dn_b, ffn_p)
        xs, *mix_s = _mix_sample_call(l, xs, mod3, ck, cv, pp, w_in_b, w_out_b, rope_s, dec_seq,
                                      mix_s)
        xs, *ffn_s = _ffn_sample_call(l, xs, mod3, cache_conv, pp, w_up_b, w_dn_b, dec_seq, ffn_s)

    kv_p = (DEPTH, nbp, WINDOW, N_KV_HEADS, HEAD_DIM)
    kv_s = (DEPTH, nbs, dec_seq, N_KV_HEADS, HEAD_DIM)
    return (xp, xs.reshape(nbs, dec_seq, D_MODEL),
            mix_p[0].reshape(kv_p), mix_p[1].reshape(kv_p), ffn_p[0],
            mix_s[0].reshape(kv_s), mix_s[1].reshape(kv_s),
            mix_s[2].reshape(DEPTH, nbs, dec_seq, GMLP_WIDTH),
            ffn_s[0])
```

```python
import functools

import jax
import jax.numpy as jnp
from jax import lax
from jax.experimental import pallas as pl
from jax.experimental.pallas import tpu as pltpu

D_MODEL = 1024
DEPTH = 2
HEAD_DIM = 64
N_HEADS = 8
N_KV_HEADS = 2
KV_WIDTH = N_KV_HEADS * HEAD_DIM
WINDOW = 128
ROPE_THETA = 500000.0
ROPE_DIMS = HEAD_DIM // 4
ATTN_WIDTH = N_HEADS * HEAD_DIM
GMLP_WIDTH = D_MODEL - ATTN_WIDTH
GMLP_GROUPS = 4
GMLP_GW = GMLP_WIDTH // GMLP_GROUPS
CHUNK = 128
IN_WIDTH = ATTN_WIDTH + 2 * KV_WIDTH + 2 * GMLP_WIDTH
D_FF = 2816
CONV_W = 3
N_MOD = 6
EPS = 1e-6
NEG = -1e30
PAST_LEN = 16384

LANES = 128
SUBLANES = 8
N_PAIRS = N_HEADS // 2
VMEM_LIMIT = 56 * 1024 * 1024

PROMPT_TILE = 512
OUT_COLS = 256
MIX_INTERLEAVE = 14
FFN_TILE = 1024
SAMPLE_SEQS = 64
SEQ_UNROLL = 8
FFN_CHUNK = 256
DOWN_GROUP = 2
ADA_TILE = 3072

F32 = jnp.float32
BF16 = jnp.bfloat16


def _dot(a, b):
    return jnp.dot(a, b, preferred_element_type=F32)


def _dot_nt(a, b):
    return lax.dot_general(a, b, (((1,), (1,)), ((), ())), preferred_element_type=F32)


def _const_spec(shape):
    nd = len(shape)
    return pl.BlockSpec(shape, lambda *_: (0,) * nd)


def _layer_spec(layer, tail, single_buffer=False):
    nd = len(tail)
    kw = {"pipeline_mode": pl.Buffered(1)} if single_buffer else {}
    return pl.BlockSpec((None,) + tuple(tail), lambda *_: (layer,) + (0,) * nd, **kw)


def _skip_aliased(body, n_in, n_alias):
    if n_alias == 0:
        return body
    return lambda *refs: body(*refs[:n_in], *refs[n_in + n_alias:])


def _layer_call(body, layer, prev, *, grid, in_specs, args, out_specs, out_shapes, scratch, name,
                n_fresh=1):
    n_in = len(args)
    aliased = [] if prev is None else list(prev)
    return pl.pallas_call(
        _skip_aliased(body, n_in, len(aliased)),
        grid=grid,
        in_specs=list(in_specs) + [pl.BlockSpec(memory_space=pl.ANY)] * len(aliased),
        out_specs=out_specs,
        out_shape=out_shapes,
        input_output_aliases={n_in + i: n_fresh + i for i in range(len(aliased))},
        scratch_shapes=scratch,
        compiler_params=pltpu.CompilerParams(
            dimension_semantics=("arbitrary",) * len(grid), vmem_limit_bytes=VMEM_LIMIT),
        name=name,
    )(*args, *aliased)


def _ada_kernel(c_ref, w_ref, b_ref, o_ref):
    c = c_ref[...]
    a = (c * jax.nn.sigmoid(c)).astype(BF16)
    o_ref[0] = _dot(a, w_ref[0].astype(BF16)) + b_ref[0]


def _ada_call(c_all, w_ada, b_ada):
    n = c_all.shape[0]
    return pl.pallas_call(
        _ada_kernel,
        grid=(DEPTH, N_MOD * D_MODEL // ADA_TILE),
        in_specs=[
            pl.BlockSpec((n, D_MODEL), lambda l, j: (0, 0)),
            pl.BlockSpec((1, D_MODEL, ADA_TILE), lambda l, j: (l, 0, j)),
            pl.BlockSpec((1, 1, ADA_TILE), lambda l, j: (l, 0, j)),
        ],
        out_specs=pl.BlockSpec((1, n, ADA_TILE), lambda l, j: (l, 0, j)),
        out_shape=jax.ShapeDtypeStruct((DEPTH, n, N_MOD * D_MODEL), F32),
        compiler_params=pltpu.CompilerParams(
            dimension_semantics=("arbitrary", "arbitrary"), vmem_limit_bytes=VMEM_LIMIT),
        name="ada_mod",
    )(c_all, w_ada, b_ada.reshape(DEPTH, 1, N_MOD * D_MODEL))


def _modulated_rmsnorm(x, gain, scale, shift):
    r = lax.rsqrt(jnp.mean(x * x, axis=-1, keepdims=True) + EPS)
    return (x * r) * (gain * (1.0 + scale)) + shift


def _head_rmsnorm(z, ones_bd, gain):
    ssq = _dot((z * z).astype(BF16), ones_bd)
    return z * lax.rsqrt(ssq * (1.0 / HEAD_DIM) + EPS) * gain


def _rope(z, cos_t, sin_dn, sin_up):
    half = ROPE_DIMS // 2
    cols = []
    for p in range(z.shape[-1] // LANES):
        zp = z[:, p * LANES:(p + 1) * LANES]
        cols.append(zp * cos_t
                    + pltpu.roll(zp, LANES - half, axis=1) * sin_dn
                    + pltpu.roll(zp, half, axis=1) * sin_up)
    return cols[0] if len(cols) == 1 else jnp.concatenate(cols, axis=1)


def _split_heads(z):
    lane = lax.broadcasted_iota(jnp.int32, z.shape, 1)
    lo = lane < HEAD_DIM
    zs = pltpu.roll(z, HEAD_DIM, axis=1)
    zero = jnp.zeros_like(z)
    return (jnp.where(lo, z, zero), jnp.where(lo, zero, zs),
            jnp.where(lo, zs, zero), jnp.where(lo, zero, z))


def _gelu_layernorm(zv, ln_g, ln_b):
    gv = jax.nn.gelu(zv)
    cols = []
    for g in range(GMLP_GROUPS):
        xg = gv[:, g * GMLP_GW:(g + 1) * GMLP_GW]
        mu = jnp.mean(xg, axis=-1, keepdims=True)
        xc = xg - mu
        var = jnp.mean(xc * xc, axis=-1, keepdims=True)
        cols.append(xc * lax.rsqrt(var + EPS))
    return jnp.concatenate(cols, axis=1) * ln_g + ln_b


def _spatial_gate_into(mix_ref, u, vg, ws_ref, bs_ref):
    vgb = vg.astype(BF16)
    for g in range(GMLP_GROUPS):
        w = ws_ref[g]
        bias = bs_ref[:, g:g + 1]
        cs = slice(g * GMLP_GW, (g + 1) * GMLP_GW)
        for c in range(u.shape[0] // CHUNK):
            rs = slice(c * CHUNK, (c + 1) * CHUNK)
            z = _dot(w, vgb[rs, cs]) + bias
            mix_ref[rs, ATTN_WIDTH + g * GMLP_GW:ATTN_WIDTH + (g + 1) * GMLP_GW] = (
                u[rs, cs] * z).astype(BF16)


def _lane_pair(a, b, shape):
    lane = lax.broadcasted_iota(jnp.int32, shape, 1)
    return jnp.where(lane < HEAD_DIM, a, b)


def _prompt_mod_spec(layer, n_skip):
    return pl.BlockSpec((None, SUBLANES, N_MOD * D_MODEL),
                        lambda *_: (layer, n_skip // SUBLANES, 0))


def _mod_row(mod_ref, seq, m):
    return mod_ref[pl.ds(seq, 1), m * D_MODEL:(m + 1) * D_MODEL]


def _expand_rows(dst_ref, src, reps):
    for c in range(dst_ref.shape[0]):
        piece = src[:, c * LANES:(c + 1) * LANES]
        for t in range(reps):
            dst_ref[c, pl.ds(t, src.shape[0], stride=reps), :] = piece


def _read_cols(ref, rows=slice(None)):
    return jnp.concatenate([ref[c, rows, :] for c in range(ref.shape[0])], axis=1)


def _interleave(tasks, width):
    pending = iter(tasks)
    active = []
    while True:
        while len(active) < width:
            task = next(pending, None)
            if task is None:
                break
            active.append(task)
        if not active:
            return
        for task in list(active):
            if next(task, "done") == "done":
                active.remove(task)


def _mix_prompt_kernel(xa_ref, xc_ref, mod_ref, gattn_ref, win32_ref, gq_ref, gk_ref,
                       bdq_ref, bdk_ref, rope_ref, sinks_ref, lng_ref, lnb_ref, ws_ref, bs_ref,
                       wout32_ref, wup32_ref, wdn32_ref,
                       y_ref, win_ref, wout_ref, wup_ref, wdn_ref, kout_ref, vout_ref,
                       q_ring, k_ring, v_ring, u_ring, vg_ring, mix_ring,
                       *, layer, tiles_per_seq, n_tiles):
    tt = xa_ref.shape[1]
    g = pl.program_id(0)

    @pl.when(g == 0)
    def _():
        win_ref[...] = win32_ref[...].astype(BF16)
        wout_ref[...] = wout32_ref[...].astype(BF16)
        for ring in (q_ring, k_ring, v_ring, u_ring, vg_ring, mix_ring):
            ring[...] = jnp.zeros(ring.shape, ring.dtype)

    wup_ref[...] = wup32_ref[...].astype(BF16)
    wdn_ref[...] = wdn32_ref[...].astype(BF16)

    cur, prv = g % 2, (g + 1) % 2
    seq_a = jnp.minimum(g, n_tiles - 1) // tiles_per_seq
    seq_c = jnp.clip(g - 2, 0, n_tiles - 1) // tiles_per_seq
    kv_a, kv_b, kv_p = g % 3, (g + 2) % 3, (g + 1) % 3

    cos_t, sin_dn, sin_up = rope_ref[0], rope_ref[1], rope_ref[2]
    env = {}

    def norm_task():
        env["hb"] = _modulated_rmsnorm(
            xa_ref[0], gattn_ref[...], _mod_row(mod_ref, seq_a, 1),
            _mod_row(mod_ref, seq_a, 0)).astype(BF16)
        yield

    def q_task():
        q = _dot(env["hb"], win_ref[:, :ATTN_WIDTH])
        yield
        ssq = _dot((q * q).astype(BF16), bdq_ref[...])
        yield
        q = q * lax.rsqrt(ssq * (1.0 / HEAD_DIM) + EPS) * gq_ref[...]
        q_ring[cur] = (_rope(q, cos_t, sin_dn, sin_up) * (HEAD_DIM ** -0.5)).astype(BF16)

    def kv_task():
        kv = _dot(env["hb"], win_ref[:, ATTN_WIDTH:ATTN_WIDTH + 2 * KV_WIDTH])
        yield
        k, v = kv[:, :KV_WIDTH], kv[:, KV_WIDTH:]
        ssq = _dot((k * k).astype(BF16), bdk_ref[...])
        vout_ref[0] = v[tt - WINDOW:, :]
        for i, vz in enumerate(_split_heads(v)):
            v_ring[kv_a, i] = vz.astype(BF16)
        yield
        k = _rope(k * lax.rsqrt(ssq * (1.0 / HEAD_DIM) + EPS) * gk_ref[...], cos_t, sin_dn, sin_up)
        kout_ref[0] = k[tt - WINDOW:, :]
        for i, kz in enumerate(_split_heads(k)):
            k_ring[kv_a, i] = kz.astype(BF16)

    def u_task():
        zu = _dot(env["hb"], win_ref[:, ATTN_WIDTH + 2 * KV_WIDTH:IN_WIDTH - GMLP_WIDTH])
        yield
        u_ring[cur] = jax.nn.gelu(zu)

    def vg_task():
        zv = _dot(env["hb"], win_ref[:, IN_WIDTH - GMLP_WIDTH:])
        yield
        vg_ring[cur] = _gelu_layernorm(zv, lng_ref[...], lnb_ref[...]).astype(BF16)

    row = lax.broadcasted_iota(jnp.int32, (WINDOW, 2 * WINDOW), 0)
    col = lax.broadcasted_iota(jnp.int32, (WINDOW, 2 * WINDOW), 1)
    band = (col - row >= 1) & (col - row <= WINDOW)
    seq_start = (g + tiles_per_seq - 1) % tiles_per_seq == 0
    band_first = band & (col >= jnp.where(seq_start, WINDOW, 0))
    tail = slice(tt - WINDOW, tt)

    def band_rows(ring, idx, i):
        if i == 0:
            return [ring[kv_p, idx, tail, :], ring[kv_b, idx, 0:WINDOW, :]]
        return [ring[kv_b, idx, (i - 1) * WINDOW:(i + 1) * WINDOW, :]]

    def attn_task(i, kv):
        mask = band_first if i == 0 else band
        rows = slice(i * WINDOW, (i + 1) * WINDOW)
        pairs = (2 * kv, 2 * kv + 1)
        kblk = jnp.concatenate(band_rows(k_ring, 2 * kv, i) + band_rows(k_ring, 2 * kv + 1, i), axis=0)
        q2 = jnp.concatenate([q_ring[prv, rows, p * LANES:(p + 1) * LANES] for p in pairs], axis=0)
        s = _dot_nt(q2, kblk)
        yield
        es, scales = [], []
        for j, p in enumerate(pairs):
            e_pair, invs = [], []
            for hh in range(2):
                sink = sinks_ref[layer, 2 * p + hh]
                sh_ = jnp.where(mask, s[j * WINDOW:(j + 1) * WINDOW,
                                        hh * 2 * WINDOW:(hh + 1) * 2 * WINDOW], NEG)
                m = jnp.maximum(jnp.max(sh_, axis=-1, keepdims=True), sink)
                e = jnp.exp(sh_ - m)
                e_pair.append(e.astype(BF16))
                invs.append(1.0 / (jnp.sum(e, axis=-1, keepdims=True) + jnp.exp(sink - m)))
            es.append(jnp.concatenate(e_pair, axis=1))
            scales.append(_lane_pair(invs[0], invs[1], (WINDOW, LANES)))
        vblk = jnp.concatenate(band_rows(v_ring, 2 * kv, i) + band_rows(v_ring, 2 * kv + 1, i), axis=0)
        o = _dot(jnp.concatenate(es, axis=0), vblk)
        yield
        for j, p in enumerate(pairs):
            mix_ring[cur, rows, p * LANES:(p + 1) * LANES] = (
                o[j * WINDOW:(j + 1) * WINDOW] * scales[j]).astype(BF16)

    def gate_task(c, grp):
        rs = slice(c * CHUNK, (c + 1) * CHUNK)
        cs = slice(grp * GMLP_GW, (grp + 1) * GMLP_GW)
        z = _dot(ws_ref[grp], vg_ring[prv, rs, cs])
        yield
        z = z + bs_ref[:, grp:grp + 1]
        mix_ring[cur, rs, ATTN_WIDTH + grp * GMLP_GW:ATTN_WIDTH + (grp + 1) * GMLP_GW] = (
            u_ring[prv, rs, cs] * z).astype(BF16)

    def out_task(c):
        cols = slice(c * OUT_COLS, (c + 1) * OUT_COLS)
        r = _dot(mix_ring[prv], wout_ref[:, cols])
        yield
        y_ref[0, :, cols] = xc_ref[0, :, cols] + _mod_row(mod_ref, seq_c, 2)[:, cols] * r

    n_blocks = tt // WINDOW
    attn = [attn_task(i, kv) for i in range(n_blocks) for kv in range(N_KV_HEADS)]
    gate = [gate_task(c, grp) for c in range(n_blocks) for grp in range(GMLP_GROUPS)]
    big = [out_task(0), norm_task(), vg_task(), out_task(1), u_task(), q_task(), out_task(2),
           kv_task(), out_task(3)]
    small = [t for trio in zip(attn, gate[0::2], gate[1::2]) for t in trio]
    per_big = -(-len(small) // len(big))
    order = []
    for b in big:
        order.append(b)
        order.extend(small[:per_big])
        small = small[per_big:]
    _interleave(order + small, MIX_INTERLEAVE)


def _mix_weight_specs(layer, w_in_spec, w_out_spec):
    return [
        _layer_spec(layer, (1, D_MODEL)),
        w_in_spec,
        _layer_spec(layer, (1, ATTN_WIDTH)),
        _layer_spec(layer, (1, KV_WIDTH)),
        _const_spec((ATTN_WIDTH, ATTN_WIDTH)),
        _const_spec((KV_WIDTH, KV_WIDTH)),
    ], [
        pl.BlockSpec(memory_space=pltpu.SMEM),
        _layer_spec(layer, (1, GMLP_WIDTH)),
        _layer_spec(layer, (1, GMLP_WIDTH)),
        _layer_spec(layer, (GMLP_GROUPS, CHUNK, CHUNK)),
        _layer_spec(layer, (CHUNK, GMLP_GROUPS)),
        w_out_spec,
    ]


def _mix_weight_args(pp, kind, w_in, w_out):
    return ([pp["g_attn"], w_in, pp["g_q"], pp["g_k"], pp["bd_q"], pp["bd_k"]],
            [pp["sinks"], pp["ln_g"], pp["ln_b"], pp["ws_" + kind], pp["bs_" + kind], w_out])


def _mix_prompt_call(layer, x, mod3, n_skip, pp, rope, prev):
    b, t, _ = x.shape
    tt = PROMPT_TILE
    nt = t // tt
    last = b * nt - 1
    w_specs_a, w_specs_b = _mix_weight_specs(
        layer, _layer_spec(layer, (D_MODEL, IN_WIDTH), single_buffer=True),
        _layer_spec(layer, (D_MODEL, D_MODEL), single_buffer=True))
    w_args_a, w_args_b = _mix_weight_args(pp, "prompt", pp["w_in"], pp["w_out"])
    up_rows, dn_rows = D_MODEL // (b * nt), D_FF // (b * nt // 2)
    assert up_rows % 16 == 0 and dn_rows % 16 == 0

    def tile(g, lag):
        return jnp.clip(g - lag, 0, last)

    def slab_specs(rows, n_slabs, width, src_layer):
        idx = lambda g: jnp.minimum(g, n_slabs - 1)
        return (pl.BlockSpec((None, rows, width), lambda g: (src_layer, idx(g), 0)),
                pl.BlockSpec((rows, width), lambda g: (idx(g), 0)))

    up_in, up_out = slab_specs(up_rows, D_MODEL // up_rows, 2 * D_FF, layer)
    dn_in, dn_out = slab_specs(dn_rows, D_FF // dn_rows, D_MODEL, layer)

    def x_spec(lag):
        return pl.BlockSpec((1, tt, D_MODEL), lambda g: (tile(g, lag) // nt, tile(g, lag) % nt, 0))

    kv_spec = pl.BlockSpec((None, 1, WINDOW, KV_WIDTH), lambda g: (layer, tile(g, 0) // nt, 0, 0))
    kv_shape = jax.ShapeDtypeStruct((DEPTH, b, WINDOW, KV_WIDTH), F32)
    return _layer_call(
        functools.partial(_mix_prompt_kernel, layer=layer, tiles_per_seq=nt, n_tiles=b * nt),
        layer, prev,
        grid=(b * nt + 2,),
        in_specs=[x_spec(0), x_spec(2), _prompt_mod_spec(layer, n_skip)] + w_specs_a
        + [pl.BlockSpec((3, tt, LANES), lambda g: (0, tile(g, 0) % nt, 0))] + w_specs_b
        + [up_in, dn_in],
        args=[x, x, mod3] + w_args_a + [rope] + w_args_b + [pp["w_up"], pp["w_down"]],
        out_specs=[x_spec(2), _const_spec((D_MODEL, IN_WIDTH)), _const_spec((D_MODEL, D_MODEL)),
                   up_out, dn_out, kv_spec, kv_spec],
        out_shapes=[jax.ShapeDtypeStruct((b, t, D_MODEL), F32),
                    jax.ShapeDtypeStruct((D_MODEL, IN_WIDTH), BF16),
                    jax.ShapeDtypeStruct((D_MODEL, D_MODEL), BF16),
                    jax.ShapeDtypeStruct((D_MODEL, 2 * D_FF), BF16),
                    jax.ShapeDtypeStruct((D_FF, D_MODEL), BF16), kv_shape, kv_shape],
        n_fresh=5,
        scratch=[
            pltpu.VMEM((2, tt, ATTN_WIDTH), BF16),
            pltpu.VMEM((3, 4, tt, LANES), BF16),
            pltpu.VMEM((3, 4, tt, LANES), BF16),
            pltpu.VMEM((2, tt, GMLP_WIDTH), F32),
            pltpu.VMEM((2, tt, GMLP_WIDTH), BF16),
            pltpu.VMEM((2, tt, D_MODEL), BF16),
        ],
        name="mix_prompt")


def _mix_sample_kernel(x_ref, sh_ref, sc_ref, ga_ref, ck_ref, cv_ref, gattn_ref, win_ref, gq_ref,
                       gk_ref, bdq_ref, bdk_ref, rope_ref, sinks_ref, lng_ref, lnb_ref, ws_ref,
                       bs_ref, wout_ref, y_ref, kout_ref, vout_ref, vgout_ref,
                       mod_scr, q_scr, s_scr, mix_ref, *, layer):
    n = x_ref.shape[0]
    sb = sh_ref.shape[0]
    ts = n // sb
    _expand_rows(mod_scr.at[0], sh_ref[...], ts)
    _expand_rows(mod_scr.at[1], sc_ref[...], ts)
    x = x_ref[...]
    h = _modulated_rmsnorm(x, gattn_ref[...], _read_cols(mod_scr.at[1]), _read_cols(mod_scr.at[0]))
    proj = _dot(h.astype(BF16), win_ref[...])
    _expand_rows(mod_scr.at[0], ga_ref[...], ts)

    cos_t, sin_dn, sin_up = rope_ref[0], rope_ref[1], rope_ref[2]
    q = _rope(_head_rmsnorm(proj[:, :ATTN_WIDTH], bdq_ref[...], gq_ref[...]), cos_t, sin_dn, sin_up)
    q = q * (HEAD_DIM ** -0.5)
    k = _rope(_head_rmsnorm(proj[:, ATTN_WIDTH:ATTN_WIDTH + KV_WIDTH], bdk_ref[...], gk_ref[...]),
              cos_t, sin_dn, sin_up)
    v = proj[:, ATTN_WIDTH + KV_WIDTH:ATTN_WIDTH + 2 * KV_WIDTH]
    kout_ref[...] = k
    vout_ref[...] = v

    lane = lax.broadcasted_iota(jnp.int32, (n, LANES), 1)
    group = N_HEADS // N_KV_HEADS
    qh = []
    for hd in range(N_HEADS):
        piece = q[:, (hd // 2) * LANES:(hd // 2 + 1) * LANES]
        if hd % 2 != hd // group:
            piece = pltpu.roll(piece, HEAD_DIM, axis=1)
        keep = (lane < HEAD_DIM) if hd // group == 0 else (lane >= HEAD_DIM)
        piece = jnp.where(keep, piece, 0.0)
        q_scr[:, hd * ts:(hd + 1) * ts, :] = piece.reshape(sb, ts, LANES)
        qh.append(piece.astype(BF16))

    def cache_scores(b, carry):
        s_scr[b] = _dot_nt(q_scr[b].astype(BF16), ck_ref[b].astype(BF16))
        return carry

    lax.fori_loop(0, sb, cache_scores, 0, unroll=SEQ_UNROLL)

    kb, vb = k.astype(BF16), v.astype(BF16)
    grp = CHUNK
    rown = lax.broadcasted_iota(jnp.int32, (grp, grp), 0)
    coln = lax.broadcasted_iota(jnp.int32, (grp, grp), 1)
    seq_bits = ts.bit_length() - 1
    mask_new = ((rown >> seq_bits) == (coln >> seq_bits)) & (coln <= rown)
    mask_cache = (lax.broadcasted_iota(jnp.int32, (grp, WINDOW), 1)
                  > (lax.broadcasted_iota(jnp.int32, (grp, WINDOW), 0) & (ts - 1)))

    for g0 in range(0, n, grp):
        rows = slice(g0, g0 + grp)
        seqs = slice(g0 // ts, (g0 + grp) // ts)
        s_new = _dot_nt(jnp.concatenate([z[rows] for z in qh], axis=0), kb[rows])
        e_new = []
        for hd in range(N_HEADS):
            sink = sinks_ref[layer, hd]
            hrows = slice(hd * ts, (hd + 1) * ts)
            sn = jnp.where(mask_new, s_new[hd * grp:(hd + 1) * grp], NEG)
            sc_ = jnp.where(mask_cache, s_scr[seqs, hrows, :].reshape(grp, WINDOW), NEG)
            m = jnp.maximum(jnp.maximum(jnp.max(sn, axis=-1, keepdims=True),
                                        jnp.max(sc_, axis=-1, keepdims=True)), sink)
            en = jnp.exp(sn - m)
            ec = jnp.exp(sc_ - m)
            inv = 1.0 / (jnp.sum(en, axis=-1, keepdims=True) + jnp.sum(ec, axis=-1, keepdims=True)
                         + jnp.exp(sink - m))
            e_new.append((en * inv).astype(BF16))
            s_scr[seqs, hrows, :] = (ec * inv).reshape(grp // ts, ts, WINDOW)
        o_new = _dot(jnp.concatenate(e_new, axis=0), vb[rows])
        for hd in range(N_HEADS):
            q_scr[seqs, hd * ts:(hd + 1) * ts, :] = o_new[hd * grp:(hd + 1) * grp].reshape(
                grp // ts, ts, LANES)

    def cache_pv(b, carry):
        q_scr[b] += _dot(s_scr[b].astype(BF16), cv_ref[b].astype(BF16))
        return carry

    lax.fori_loop(0, sb, cache_pv, 0, unroll=SEQ_UNROLL)

    for p in range(N_PAIRS):
        halves = []
        for hd in (2 * p, 2 * p + 1):
            o = q_scr[:, hd * ts:(hd + 1) * ts, :].reshape(n, LANES)
            if hd % 2 != hd // group:
                o = pltpu.roll(o, HEAD_DIM, axis=1)
            halves.append(o)
        mix_ref[:, p * LANES:(p + 1) * LANES] = jnp.where(
            lane < HEAD_DIM, halves[0], halves[1]).astype(BF16)

    u = jax.nn.gelu(proj[:, ATTN_WIDTH + 2 * KV_WIDTH:ATTN_WIDTH + 2 * KV_WIDTH + GMLP_WIDTH])
    vg = _gelu_layernorm(proj[:, IN_WIDTH - GMLP_WIDTH:], lng_ref[...], lnb_ref[...])
    vgout_ref[...] = vg
    _spatial_gate_into(mix_ref, u, vg, ws_ref, bs_ref)

    y_ref[...] = x + _read_cols(mod_scr.at[0]) * _dot(mix_ref[...], wout_ref[...])


def _sample_mod_specs(layer, sb, first):
    return [pl.BlockSpec((None, sb, D_MODEL), functools.partial(lambda i, w: (layer, i, w), w=first + m))
            for m in range(3)]


def _single_spec(shape):
    nd = len(shape)
    return pl.BlockSpec(shape, lambda *_: (0,) * nd, pipeline_mode=pl.Buffered(1))


def _mix_sample_call(layer, x, mod3, cache_k, cache_v, pp, w_in, w_out, rope, ts, prev):
    nt = x.shape[0]
    sb = SAMPLE_SEQS
    n = sb * ts
    w_specs_a, w_specs_b = _mix_weight_specs(
        layer, _single_spec((D_MODEL, IN_WIDTH)), _single_spec((D_MODEL, D_MODEL)))
    w_args_a, w_args_b = _mix_weight_args(pp, "sample", w_in, w_out)
    tok_spec = lambda w: pl.BlockSpec((None, n, w), lambda i: (layer, i, 0))
    tok_shape = lambda w: jax.ShapeDtypeStruct((DEPTH, nt, w), F32)
    cache_spec = pl.BlockSpec((None, sb, WINDOW, KV_WIDTH), lambda i: (layer, i, 0, 0))
    return _layer_call(
        functools.partial(_mix_sample_kernel, layer=layer), layer, prev,
        grid=(nt // n,),
        in_specs=[pl.BlockSpec((n, D_MODEL), lambda i: (i, 0))] + _sample_mod_specs(layer, sb, 0)
        + [cache_spec, cache_spec] + w_specs_a + [_const_spec((3, n, LANES))] + w_specs_b,
        args=[x, mod3, mod3, mod3, cache_k, cache_v] + w_args_a + [rope] + w_args_b,
        out_specs=[pl.BlockSpec((n, D_MODEL), lambda i: (i, 0)),
                   tok_spec(KV_WIDTH), tok_spec(KV_WIDTH), tok_spec(GMLP_WIDTH)],
        out_shapes=[jax.ShapeDtypeStruct((nt, D_MODEL), F32),
                    tok_shape(KV_WIDTH), tok_shape(KV_WIDTH), tok_shape(GMLP_WIDTH)],
        scratch=[
            pltpu.VMEM((2, D_MODEL // LANES, n, LANES), F32),
            pltpu.VMEM((sb, N_HEADS * ts, LANES), F32),
            pltpu.VMEM((sb, N_HEADS * ts, WINDOW), F32),
            pltpu.VMEM((n, D_MODEL), BF16),
        ],
        name="mix_sample")


def _shifted_rows(hu, before):
    rows, c = hu.shape
    hu3 = hu.reshape(rows // SUBLANES, SUBLANES, c)
    above = jnp.concatenate([before[None], hu3[:-1]], axis=0)
    sub = lax.broadcasted_iota(jnp.int32, (1, SUBLANES, c), 1)
    p1 = pltpu.roll(jnp.where(sub >= SUBLANES - 1, above, hu3), 1, axis=1)
    p2 = pltpu.roll(jnp.where(sub >= SUBLANES - 2, above, hu3), 2, axis=1)
    return p1.reshape(rows, c), p2.reshape(rows, c)


def _conv(hu, prev1, prev2, cw, cb):
    return cb + cw[0:1] * prev2 + cw[1:2] * prev1 + cw[2:3] * hu


def _ffn_chunks():
    return [(slice(j, j + FFN_CHUNK), slice(D_FF + j, D_FF + j + FFN_CHUNK))
            for j in range(0, D_FF, FFN_CHUNK)]


def _down_plan(n_chunks):
    bounds = list(range(0, n_chunks, DOWN_GROUP)) + [n_chunks]
    groups = [(a * FFN_CHUNK, b * FFN_CHUNK) for a, b in zip(bounds[:-1], bounds[1:])]
    return {b // FFN_CHUNK: (a, b) for a, b in groups[:-1]}, groups[-1]


def _down_into(acc_ref, act_scr, wdn_ref, a, b):
    contrib = _dot(act_scr[:, a:b], wdn_ref[a:b, :])
    if a == 0:
        acc_ref[...] = contrib
    else:
        acc_ref[...] += contrib


def _ffn_prompt_kernel(x_ref, mod_ref, gffn_ref, wup_ref, cw_ref, cb_ref, wdn_ref,
                       y_ref, cout_ref, carry, act_scr, acc_ref):
    tt = x_ref.shape[1]
    t = pl.program_id(1)

    @pl.when(t == 0)
    def _():
        carry[...] = jnp.zeros(carry.shape, F32)

    x = x_ref[0]
    seq = pl.program_id(0)
    hb = _modulated_rmsnorm(x, gffn_ref[...], _mod_row(mod_ref, seq, 4),
                            _mod_row(mod_ref, seq, 3)).astype(BF16)
    chunks = _ffn_chunks()
    after, tail = _down_plan(len(chunks))
    up = lambda halves: [_dot(hb, wup_ref[:, cols]) for cols in halves]
    ahead = up(chunks[0])
    for j, halves in enumerate(chunks):
        hus = ahead
        if j + 1 < len(chunks):
            ahead = up(chunks[j + 1])
        if j in after:
            _down_into(acc_ref, act_scr, wdn_ref, *after[j])
        convs = []
        for hu, cols in zip(hus, halves):
            prev1, prev2 = _shifted_rows(hu, carry[:, cols])
            carry[:, cols] = hu[tt - SUBLANES:, :]
            cout_ref[0, :, cols] = hu[tt - (CONV_W - 1):, :]
            convs.append(_conv(hu, prev1, prev2, cw_ref[:, cols], cb_ref[:, cols]))
        act_scr[:, j * FFN_CHUNK:(j + 1) * FFN_CHUNK] = (
            jax.nn.silu(convs[0]) * convs[1]).astype(BF16)
    _down_into(acc_ref, act_scr, wdn_ref, *tail)
    y_ref[0] = x + _mod_row(mod_ref, seq, 5) * acc_ref[...]


def _ffn_weight_specs(layer):
    return [
        _layer_spec(layer, (1, D_MODEL)),
        _single_spec((D_MODEL, 2 * D_FF)),
        _layer_spec(layer, (CONV_W, 2 * D_FF)),
        _layer_spec(layer, (1, 2 * D_FF)),
        _single_spec((D_FF, D_MODEL)),
    ]


def _ffn_weight_args(pp, w_up, w_down):
    return [pp["g_ffn"], w_up, pp["conv_w"], pp["conv_b"], w_down]


def _ffn_prompt_call(layer, x, mod3, n_skip, pp, w_up, w_down, prev):
    b, t, _ = x.shape
    tt = FFN_TILE
    return _layer_call(
        _ffn_prompt_kernel, layer, prev,
        grid=(b, t // tt),
        in_specs=[pl.BlockSpec((1, tt, D_MODEL), lambda i, j: (i, j, 0)),
                  _prompt_mod_spec(layer, n_skip)]
        + _ffn_weight_specs(layer),
        args=[x, mod3] + _ffn_weight_args(pp, w_up, w_down),
        out_specs=[pl.BlockSpec((1, tt, D_MODEL), lambda i, j: (i, j, 0)),
                   pl.BlockSpec((None, 1, CONV_W - 1, 2 * D_FF), lambda i, j: (layer, i, 0, 0))],
        out_shapes=[jax.ShapeDtypeStruct((b, t, D_MODEL), F32),
                    jax.ShapeDtypeStruct((DEPTH, b, CONV_W - 1, 2 * D_FF), F32)],
        scratch=[
            pltpu.VMEM((SUBLANES, 2 * D_FF), F32),
            pltpu.VMEM((tt, D_FF), BF16),
            pltpu.VMEM((tt, D_MODEL), F32),
        ],
        name="ffn_prompt")


def _ffn_sample_kernel(x_ref, sh_ref, sc_ref, gf_ref, hist_ref, gffn_ref, wup_ref, cw_ref,
                       cb_ref, wdn_ref, y_ref, cout_ref, mod_scr, act_scr, acc_ref):
    n = x_ref.shape[0]
    sb = sh_ref.shape[0]
    ts = n // sb
    sub = lax.broadcasted_iota(jnp.int32, (1, ts, FFN_CHUNK), 1)
    _expand_rows(mod_scr.at[0], sh_ref[...], ts)
    _expand_rows(mod_scr.at[1], sc_ref[...], ts)
    x = x_ref[...]
    hb = _modulated_rmsnorm(x, gffn_ref[...], _read_cols(mod_scr.at[1]),
                            _read_cols(mod_scr.at[0])).astype(BF16)
    _expand_rows(mod_scr.at[0], gf_ref[...], ts)
    chunks = _ffn_chunks()
    after, tail = _down_plan(len(chunks))
    up = lambda halves: [_dot(hb, wup_ref[:, cols]) for cols in halves]
    ahead = up(chunks[0])
    for j, halves in enumerate(chunks):
        hus = ahead
        if j + 1 < len(chunks):
            ahead = up(chunks[j + 1])
        if j in after:
            _down_into(acc_ref, act_scr, wdn_ref, *after[j])
        convs = []
        for hu, cols in zip(hus, halves):
            hu3 = hu.reshape(sb, ts, FFN_CHUNK)
            cout_ref[:, :, cols] = hu3[:, ts - (CONV_W - 1):, :]
            above = jnp.concatenate(
                [jnp.zeros((sb, ts - (CONV_W - 1), FFN_CHUNK), F32), hist_ref[:, :, cols]], axis=1)
            p1 = pltpu.roll(jnp.where(sub >= ts - 1, above, hu3), 1, axis=1)
            p2 = pltpu.roll(jnp.where(sub >= ts - 2, above, hu3), 2, axis=1)
            convs.append(_conv(hu, p1.reshape(n, FFN_CHUNK), p2.reshape(n, FFN_CHUNK),
                               cw_ref[:, cols], cb_ref[:, cols]))
        act_scr[:, j * FFN_CHUNK:(j + 1) * FFN_CHUNK] = (
            jax.nn.silu(convs[0]) * convs[1]).astype(BF16)
    _down_into(acc_ref, act_scr, wdn_ref, *tail)
    y_ref[...] = x + _read_cols(mod_scr.at[0]) * acc_ref[...]


def _ffn_sample_call(layer, x, mod3, cconv, pp, w_up, w_down, ts, prev):
    nt = x.shape[0]
    sb = SAMPLE_SEQS
    n = sb * ts
    tok_spec = pl.BlockSpec((n, D_MODEL), lambda i: (i, 0))
    hist_spec = pl.BlockSpec((None, sb, CONV_W - 1, 2 * D_FF), lambda i: (layer, i, 0, 0))
    return _layer_call(
        _ffn_sample_kernel, layer, prev,
        grid=(nt // n,),
        in_specs=[tok_spec] + _sample_mod_specs(layer, sb, 3) + [hist_spec]
        + _ffn_weight_specs(layer),
        args=[x, mod3, mod3, mod3, cconv] + _ffn_weight_args(pp, w_up, w_down),
        out_specs=[tok_spec, hist_spec],
        out_shapes=[jax.ShapeDtypeStruct((nt, D_MODEL), F32),
                    jax.ShapeDtypeStruct(cconv.shape, F32)],
        scratch=[
            pltpu.VMEM((2, D_MODEL // LANES, n, LANES), F32),
            pltpu.VMEM((n, D_FF), BF16),
            pltpu.VMEM((n, D_MODEL), F32),
        ],
        name="ffn_sample")


def _rope_tables(pos):
    half = ROPE_DIMS // 2
    inv = ROPE_THETA ** (-jnp.arange(0, ROPE_DIMS, 2, dtype=F32) / ROPE_DIMS)
    ang = pos.astype(F32)[:, None] * inv[None, :]
    cos, sin = jnp.cos(ang), jnp.sin(ang)
    n = pos.shape[0]
    rest = jnp.zeros((n, HEAD_DIM - ROPE_DIMS), F32)
    zeros = jnp.zeros((n, half), F32)
    cos_t = jnp.concatenate([cos, cos, rest + 1.0], axis=1)
    sin_dn = jnp.concatenate([-sin, zeros, rest], axis=1)
    sin_up = jnp.concatenate([zeros, sin, rest], axis=1)
    return jnp.stack([jnp.tile(z, (1, LANES // HEAD_DIM)) for z in (cos_t, sin_dn, sin_up)])


def _block_diag_ones(width):
    idx = jnp.arange(width) // HEAD_DIM
    return (idx[:, None] == idx[None, :]).astype(BF16)


def _prepare_params(dec_seq, g_attn, w_in, g_q, g_k, sinks, ln_g, ln_b, w_s, b_s, w_out,
                    g_ffn, w_ffn_in, conv_w, conv_b, w_ffn_out):
    causal = jnp.tril(jnp.ones((CHUNK, CHUNK), dtype=bool))
    ws = jnp.where(causal, w_s, 0.0)
    seqs_per_chunk = CHUNK // dec_seq
    eye = jnp.eye(seqs_per_chunk, dtype=F32)
    ws_sample = jnp.einsum("ab,lgts->lgatbs", eye, ws[:, :, :dec_seq, :dec_seq]).reshape(w_s.shape)
    return {
        "g_attn": g_attn[:, None, :], "g_ffn": g_ffn[:, None, :],
        "w_in": w_in, "w_out": w_out,
        "g_q": jnp.tile(g_q, (1, N_HEADS))[:, None, :],
        "g_k": jnp.tile(g_k, (1, N_KV_HEADS))[:, None, :],
        "bd_q": _block_diag_ones(ATTN_WIDTH), "bd_k": _block_diag_ones(KV_WIDTH),
        "sinks": sinks,
        "ln_g": ln_g.reshape(DEPTH, 1, GMLP_WIDTH), "ln_b": ln_b.reshape(DEPTH, 1, GMLP_WIDTH),
        "ws_prompt": ws.astype(BF16), "bs_prompt": jnp.swapaxes(b_s, 1, 2),
        "ws_sample": ws_sample.astype(BF16),
        "bs_sample": jnp.swapaxes(jnp.tile(b_s[:, :, :dec_seq], (1, 1, seqs_per_chunk)), 1, 2),
        "w_up": w_ffn_in, "w_down": w_ffn_out,
        "conv_w": conv_w, "conv_b": conv_b[:, None, :],
    }


def kernel(x_prompt, x_sample, cache_k, cache_v, cache_conv, c_prompt, c_sample, w_ada, b_ada,
           g_attn, w_in, g_q, g_k, sinks, ln_g, ln_b, w_s, b_s, w_out, g_ffn, w_ffn_in, conv_w,
           conv_b, w_ffn_out):
    nbp, seq, _ = x_prompt.shape
    nbs, dec_seq, _ = x_sample.shape
    assert seq % PROMPT_TILE == 0 and PROMPT_TILE % CHUNK == 0 and nbs % SAMPLE_SEQS == 0
    assert seq % FFN_TILE == 0 and FFN_TILE % SUBLANES == 0 and D_FF % FFN_CHUNK == 0
    assert dec_seq == SUBLANES and CHUNK % dec_seq == 0 and cache_k.shape[2] == WINDOW
    assert nbs % SUBLANES == 0 and nbp <= SUBLANES

    n_c = nbs + nbp
    c_all = jnp.concatenate([c_sample, c_prompt, jnp.zeros((-n_c % SUBLANES, D_MODEL), F32)])
    mod3 = _ada_call(c_all, w_ada, b_ada)

    pp = _prepare_params(dec_seq, g_attn, w_in, g_q, g_k, sinks, ln_g, ln_b, w_s, b_s, w_out,
                         g_ffn, w_ffn_in, conv_w, conv_b, w_ffn_out)
    rope_p = _rope_tables(jnp.arange(seq, dtype=jnp.int32))
    rope_s = jnp.tile(_rope_tables(PAST_LEN + jnp.arange(dec_seq, dtype=jnp.int32)),
                      (1, SAMPLE_SEQS, 1))
    ck = cache_k.reshape(DEPTH, nbs, WINDOW, KV_WIDTH)
    cv = cache_v.reshape(DEPTH, nbs, WINDOW, KV_WIDTH)

    xp = x_prompt
    xs = x_sample.reshape(nbs * dec_seq, D_MODEL)
    mix_p = ffn_p = mix_s = ffn_s = None
    for l in range(DEPTH):
        xp, w_in_b, w_out_b, w_up_b, w_dn_b, *mix_p = _mix_prompt_call(
            l, xp, mod3, nbs, pp, rope_p, mix_p)
        xp, *ffn_p = _ffn_prompt_call(l, xp, mod3, nbs, pp, w_up_b, w_dn_b, ffn_p)
        xs, *mix_s = _mix_sample_call(l, xs, mod3, ck, cv, pp, w_in_b, w_out_b, rope_s, dec_seq,
                                      mix_s)
        xs, *ffn_s = _ffn_sample_call(l, xs, mod3, cache_conv, pp, w_up_b, w_dn_b, dec_seq, ffn_s)

    kv_p = (DEPTH, nbp, WINDOW, N_KV_HEADS, HEAD_DIM)
    kv_s = (DEPTH, nbs, dec_seq, N_KV_HEADS, HEAD_DIM)
    return (xp, xs.reshape(nbs, dec_seq, D_MODEL),
            mix_p[0].reshape(kv_p), mix_p[1].reshape(kv_p), ffn_p[0],
            mix_s[0].reshape(kv_s), mix_s[1].reshape(kv_s),
            mix_s[2].reshape(DEPTH, nbs, dec_seq, GMLP_WIDTH),
            ffn_s[0])
```

```python
import functools

import jax
import jax.numpy as jnp
from jax import lax
from jax.experimental import pallas as pl
from jax.experimental.pallas import tpu as pltpu

D_MODEL = 1024
DEPTH = 2
HEAD_DIM = 64
N_HEADS = 8
N_KV_HEADS = 2
KV_WIDTH = N_KV_HEADS * HEAD_DIM
WINDOW = 128
ROPE_THETA = 500000.0
ROPE_DIMS = HEAD_DIM // 4
ATTN_WIDTH = N_HEADS * HEAD_DIM
GMLP_WIDTH = D_MODEL - ATTN_WIDTH
GMLP_GROUPS = 4
GMLP_GW = GMLP_WIDTH // GMLP_GROUPS
CHUNK = 128
IN_WIDTH = ATTN_WIDTH + 2 * KV_WIDTH + 2 * GMLP_WIDTH
D_FF = 2816
CONV_W = 3
N_MOD = 6
EPS = 1e-6
NEG = -1e30
PAST_LEN = 16384

LANES = 128
SUBLANES = 8
N_PAIRS = N_HEADS // 2
VMEM_LIMIT = 56 * 1024 * 1024

PROMPT_TILE = 512
OUT_COLS = 256
MIX_INTERLEAVE = 14
FFN_TILE = 1024
SAMPLE_SEQS = 64
SEQ_UNROLL = 8
FFN_CHUNK = 256
DOWN_GROUP = 2
ADA_TILE = 3072

F32 = jnp.float32
BF16 = jnp.bfloat16


def _dot(a, b):
    return jnp.dot(a, b, preferred_element_type=F32)


def _dot_nt(a, b):
    return lax.dot_general(a, b, (((1,), (1,)), ((), ())), preferred_element_type=F32)


def _const_spec(shape):
    nd = len(shape)
    return pl.BlockSpec(shape, lambda *_: (0,) * nd)


def _layer_spec(layer, tail, single_buffer=False):
    nd = len(tail)
    kw = {"pipeline_mode": pl.Buffered(1)} if single_buffer else {}
    return pl.BlockSpec((None,) + tuple(tail), lambda *_: (layer,) + (0,) * nd, **kw)


def _skip_aliased(body, n_in, n_alias):
    if n_alias == 0:
        return body
    return lambda *refs: body(*refs[:n_in], *refs[n_in + n_alias:])


def _layer_call(body, layer, prev, *, grid, in_specs, args, out_specs, out_shapes, scratch, name,
                n_fresh=1):
    n_in = len(args)
    aliased = [] if prev is None else list(prev)
    return pl.pallas_call(
        _skip_aliased(body, n_in, len(aliased)),
        grid=grid,
        in_specs=list(in_specs) + [pl.BlockSpec(memory_space=pl.ANY)] * len(aliased),
        out_specs=out_specs,
        out_shape=out_shapes,
        input_output_aliases={n_in + i: n_fresh + i for i in range(len(aliased))},
        scratch_shapes=scratch,
        compiler_params=pltpu.CompilerParams(
            dimension_semantics=("arbitrary",) * len(grid), vmem_limit_bytes=VMEM_LIMIT),
        name=name,
    )(*args, *aliased)


def _ada_kernel(c_ref, w_ref, b_ref, o_ref):
    c = c_ref[...]
    a = (c * jax.nn.sigmoid(c)).astype(BF16)
    o_ref[0] = _dot(a, w_ref[0].astype(BF16)) + b_ref[0]


def _ada_call(c_all, w_ada, b_ada):
    n = c_all.shape[0]
    return pl.pallas_call(
        _ada_kernel,
        grid=(DEPTH, N_MOD * D_MODEL // ADA_TILE),
        in_specs=[
            pl.BlockSpec((n, D_MODEL), lambda l, j: (0, 0)),
            pl.BlockSpec((1, D_MODEL, ADA_TILE), lambda l, j: (l, 0, j)),
            pl.BlockSpec((1, 1, ADA_TILE), lambda l, j: (l, 0, j)),
        ],
        out_specs=pl.BlockSpec((1, n, ADA_TILE), lambda l, j: (l, 0, j)),
        out_shape=jax.ShapeDtypeStruct((DEPTH, n, N_MOD * D_MODEL), F32),
        compiler_params=pltpu.CompilerParams(
            dimension_semantics=("arbitrary", "arbitrary"), vmem_limit_bytes=VMEM_LIMIT),
        name="ada_mod",
    )(c_all, w_ada, b_ada.reshape(DEPTH, 1, N_MOD * D_MODEL))


def _modulated_rmsnorm(x, gain, scale, shift):
    r = lax.rsqrt(jnp.mean(x * x, axis=-1, keepdims=True) + EPS)
    return (x * r) * (gain * (1.0 + scale)) + shift


def _head_rmsnorm(z, ones_bd, gain):
    ssq = _dot((z * z).astype(BF16), ones_bd)
    return z * lax.rsqrt(ssq * (1.0 / HEAD_DIM) + EPS) * gain


def _rope(z, cos_t, sin_dn, sin_up):
    half = ROPE_DIMS // 2
    cols = []
    for p in range(z.shape[-1] // LANES):
        zp = z[:, p * LANES:(p + 1) * LANES]
        cols.append(zp * cos_t
                    + pltpu.roll(zp, LANES - half, axis=1) * sin_dn
                    + pltpu.roll(zp, half, axis=1) * sin_up)
    return cols[0] if len(cols) == 1 else jnp.concatenate(cols, axis=1)


def _split_heads(z):
    lane = lax.broadcasted_iota(jnp.int32, z.shape, 1)
    lo = lane < HEAD_DIM
    zs = pltpu.roll(z, HEAD_DIM, axis=1)
    zero = jnp.zeros_like(z)
    return (jnp.where(lo, z, zero), jnp.where(lo, zero, zs),
            jnp.where(lo, zs, zero), jnp.where(lo, zero, z))


def _gelu_layernorm(zv, ln_g, ln_b):
    gv = jax.nn.gelu(zv)
    cols = []
    for g in range(GMLP_GROUPS):
        xg = gv[:, g * GMLP_GW:(g + 1) * GMLP_GW]
        mu = jnp.mean(xg, axis=-1, keepdims=True)
        xc = xg - mu
        var = jnp.mean(xc * xc, axis=-1, keepdims=True)
        cols.append(xc * lax.rsqrt(var + EPS))
    return jnp.concatenate(cols, axis=1) * ln_g + ln_b


def _spatial_gate_into(mix_ref, u, vg, ws_ref, bs_ref):
    vgb = vg.astype(BF16)
    for g in range(GMLP_GROUPS):
        w = ws_ref[g]
        bias = bs_ref[:, g:g + 1]
        cs = slice(g * GMLP_GW, (g + 1) * GMLP_GW)
        for c in range(u.shape[0] // CHUNK):
            rs = slice(c * CHUNK, (c + 1) * CHUNK)
            z = _dot(w, vgb[rs, cs]) + bias
            mix_ref[rs, ATTN_WIDTH + g * GMLP_GW:ATTN_WIDTH + (g + 1) * GMLP_GW] = (
                u[rs, cs] * z).astype(BF16)


def _lane_pair(a, b, shape):
    lane = lax.broadcasted_iota(jnp.int32, shape, 1)
    return jnp.where(lane < HEAD_DIM, a, b)


def _prompt_mod_spec(layer, n_skip):
    return pl.BlockSpec((None, SUBLANES, N_MOD * D_MODEL),
                        lambda *_: (layer, n_skip // SUBLANES, 0))


def _mod_row(mod_ref, seq, m):
    return mod_ref[pl.ds(seq, 1), m * D_MODEL:(m + 1) * D_MODEL]


def _window_t(ref, b):
    return ref[b].reshape(KV_WIDTH, WINDOW).astype(BF16)


def _expand_rows(dst_ref, src, reps):
    for c in range(dst_ref.shape[0]):
        piece = src[:, c * LANES:(c + 1) * LANES]
        for t in range(reps):
            dst_ref[c, pl.ds(t, src.shape[0], stride=reps), :] = piece


def _read_cols(ref, rows=slice(None)):
    return jnp.concatenate([ref[c, rows, :] for c in range(ref.shape[0])], axis=1)


def _interleave(tasks, width):
    pending = iter(tasks)
    active = []
    while True:
        while len(active) < width:
            task = next(pending, None)
            if task is None:
                break
            active.append(task)
        if not active:
            return
        for task in list(active):
            if next(task, "done") == "done":
                active.remove(task)


def _mix_prompt_kernel(xa_ref, xc_ref, mod_ref, gattn_ref, win32_ref, gq_ref, gk_ref,
                       bdq_ref, bdk_ref, rope_ref, sinks_ref, lng_ref, lnb_ref, ws_ref, bs_ref,
                       wout32_ref, wup32_ref, wdn32_ref,
                       y_ref, win_ref, wout_ref, wup_ref, wdn_ref, kout_ref, vout_ref,
                       q_ring, k_ring, v_ring, u_ring, vg_ring, mix_ring,
                       *, layer, tiles_per_seq, n_tiles):
    tt = xa_ref.shape[1]
    g = pl.program_id(0)

    @pl.when(g == 0)
    def _():
        win_ref[...] = win32_ref[...].astype(BF16)
        wout_ref[...] = wout32_ref[...].astype(BF16)
        for ring in (q_ring, k_ring, v_ring, u_ring, vg_ring, mix_ring):
            ring[...] = jnp.zeros(ring.shape, ring.dtype)

    wup_ref[...] = wup32_ref[...].astype(BF16)
    wdn_ref[...] = wdn32_ref[...].astype(BF16)

    cur, prv = g % 2, (g + 1) % 2
    seq_a = jnp.minimum(g, n_tiles - 1) // tiles_per_seq
    seq_c = jnp.clip(g - 2, 0, n_tiles - 1) // tiles_per_seq
    kv_a, kv_b, kv_p = g % 3, (g + 2) % 3, (g + 1) % 3

    cos_t, sin_dn, sin_up = rope_ref[0], rope_ref[1], rope_ref[2]
    env = {}

    def norm_task():
        env["hb"] = _modulated_rmsnorm(
            xa_ref[0], gattn_ref[...], _mod_row(mod_ref, seq_a, 1),
            _mod_row(mod_ref, seq_a, 0)).astype(BF16)
        yield

    def q_task():
        q = _dot(env["hb"], win_ref[:, :ATTN_WIDTH])
        yield
        ssq = _dot((q * q).astype(BF16), bdq_ref[...])
        yield
        q = q * lax.rsqrt(ssq * (1.0 / HEAD_DIM) + EPS) * gq_ref[...]
        q_ring[cur] = (_rope(q, cos_t, sin_dn, sin_up) * (HEAD_DIM ** -0.5)).astype(BF16)

    def kv_task():
        kv = _dot(env["hb"], win_ref[:, ATTN_WIDTH:ATTN_WIDTH + 2 * KV_WIDTH])
        yield
        k, v = kv[:, :KV_WIDTH], kv[:, KV_WIDTH:]
        ssq = _dot((k * k).astype(BF16), bdk_ref[...])
        vout_ref[0] = v[tt - WINDOW:, :]
        for i, vz in enumerate(_split_heads(v)):
            v_ring[kv_a, i] = vz.astype(BF16)
        yield
        k = _rope(k * lax.rsqrt(ssq * (1.0 / HEAD_DIM) + EPS) * gk_ref[...], cos_t, sin_dn, sin_up)
        kout_ref[0] = k[tt - WINDOW:, :]
        for i, kz in enumerate(_split_heads(k)):
            k_ring[kv_a, i] = kz.astype(BF16)

    def u_task():
        zu = _dot(env["hb"], win_ref[:, ATTN_WIDTH + 2 * KV_WIDTH:IN_WIDTH - GMLP_WIDTH])
        yield
        u_ring[cur] = jax.nn.gelu(zu)

    def vg_task():
        zv = _dot(env["hb"], win_ref[:, IN_WIDTH - GMLP_WIDTH:])
        yield
        vg_ring[cur] = _gelu_layernorm(zv, lng_ref[...], lnb_ref[...]).astype(BF16)

    row = lax.broadcasted_iota(jnp.int32, (WINDOW, 2 * WINDOW), 0)
    col = lax.broadcasted_iota(jnp.int32, (WINDOW, 2 * WINDOW), 1)
    band = (col - row >= 1) & (col - row <= WINDOW)
    seq_start = (g + tiles_per_seq - 1) % tiles_per_seq == 0
    band_first = band & (col >= jnp.where(seq_start, WINDOW, 0))
    tail = slice(tt - WINDOW, tt)

    def band_rows(ring, idx, i):
        if i == 0:
            return [ring[kv_p, idx, tail, :], ring[kv_b, idx, 0:WINDOW, :]]
        return [ring[kv_b, idx, (i - 1) * WINDOW:(i + 1) * WINDOW, :]]

    def attn_task(i, kv):
        mask = band_first if i == 0 else band
        rows = slice(i * WINDOW, (i + 1) * WINDOW)
        pairs = (2 * kv, 2 * kv + 1)
        kblk = jnp.concatenate(band_rows(k_ring, 2 * kv, i) + band_rows(k_ring, 2 * kv + 1, i), axis=0)
        q2 = jnp.concatenate([q_ring[prv, rows, p * LANES:(p + 1) * LANES] for p in pairs], axis=0)
        s = _dot_nt(q2, kblk)
        yield
        es, scales = [], []
        for j, p in enumerate(pairs):
            e_pair, invs = [], []
            for hh in range(2):
                sink = sinks_ref[layer, 2 * p + hh]
                sh_ = jnp.where(mask, s[j * WINDOW:(j + 1) * WINDOW,
                                        hh * 2 * WINDOW:(hh + 1) * 2 * WINDOW], NEG)
                m = jnp.maximum(jnp.max(sh_, axis=-1, keepdims=True), sink)
                e = jnp.exp(sh_ - m)
                e_pair.append(e.astype(BF16))
                invs.append(1.0 / (jnp.sum(e, axis=-1, keepdims=True) + jnp.exp(sink - m)))
            es.append(jnp.concatenate(e_pair, axis=1))
            scales.append(_lane_pair(invs[0], invs[1], (WINDOW, LANES)))
        vblk = jnp.concatenate(band_rows(v_ring, 2 * kv, i) + band_rows(v_ring, 2 * kv + 1, i), axis=0)
        o = _dot(jnp.concatenate(es, axis=0), vblk)
        yield
        for j, p in enumerate(pairs):
            mix_ring[cur, rows, p * LANES:(p + 1) * LANES] = (
                o[j * WINDOW:(j + 1) * WINDOW] * scales[j]).astype(BF16)

    def gate_task(c, grp):
        rs = slice(c * CHUNK, (c + 1) * CHUNK)
        cs = slice(grp * GMLP_GW, (grp + 1) * GMLP_GW)
        z = _dot(ws_ref[grp], vg_ring[prv, rs, cs])
        yield
        z = z + bs_ref[:, grp:grp + 1]
        mix_ring[cur, rs, ATTN_WIDTH + grp * GMLP_GW:ATTN_WIDTH + (grp + 1) * GMLP_GW] = (
            u_ring[prv, rs, cs] * z).astype(BF16)

    def out_task(c):
        cols = slice(c * OUT_COLS, (c + 1) * OUT_COLS)
        r = _dot(mix_ring[prv], wout_ref[:, cols])
        yield
        y_ref[0, :, cols] = xc_ref[0, :, cols] + _mod_row(mod_ref, seq_c, 2)[:, cols] * r

    n_blocks = tt // WINDOW
    attn = [attn_task(i, kv) for i in range(n_blocks) for kv in range(N_KV_HEADS)]
    gate = [gate_task(c, grp) for c in range(n_blocks) for grp in range(GMLP_GROUPS)]
    big = [out_task(0), norm_task(), vg_task(), out_task(1), u_task(), q_task(), out_task(2),
           kv_task(), out_task(3)]
    small = [t for trio in zip(attn, gate[0::2], gate[1::2]) for t in trio]
    per_big = -(-len(small) // len(big))
    order = []
    for b in big:
        order.append(b)
        order.extend(small[:per_big])
        small = small[per_big:]
    _interleave(order + small, MIX_INTERLEAVE)


def _mix_weight_specs(layer, w_in_spec, w_out_spec):
    return [
        _layer_spec(layer, (1, D_MODEL)),
        w_in_spec,
        _layer_spec(layer, (1, ATTN_WIDTH)),
        _layer_spec(layer, (1, KV_WIDTH)),
        _const_spec((ATTN_WIDTH, ATTN_WIDTH)),
        _const_spec((KV_WIDTH, KV_WIDTH)),
    ], [
        pl.BlockSpec(memory_space=pltpu.SMEM),
        _layer_spec(layer, (1, GMLP_WIDTH)),
        _layer_spec(layer, (1, GMLP_WIDTH)),
        _layer_spec(layer, (GMLP_GROUPS, CHUNK, CHUNK)),
        _layer_spec(layer, (CHUNK, GMLP_GROUPS)),
        w_out_spec,
    ]


def _mix_weight_args(pp, kind, w_in, w_out):
    return ([pp["g_attn"], w_in, pp["g_q"], pp["g_k"], pp["bd_q"], pp["bd_k"]],
            [pp["sinks"], pp["ln_g"], pp["ln_b"], pp["ws_" + kind], pp["bs_" + kind], w_out])


def _mix_prompt_call(layer, x, mod3, n_skip, pp, rope, prev):
    b, t, _ = x.shape
    tt = PROMPT_TILE
    nt = t // tt
    last = b * nt - 1
    w_specs_a, w_specs_b = _mix_weight_specs(
        layer, _layer_spec(layer, (D_MODEL, IN_WIDTH), single_buffer=True),
        _layer_spec(layer, (D_MODEL, D_MODEL), single_buffer=True))
    w_args_a, w_args_b = _mix_weight_args(pp, "prompt", pp["w_in"], pp["w_out"])
    up_rows, dn_rows = D_MODEL // (b * nt), D_FF // (b * nt // 2)
    assert up_rows % 16 == 0 and dn_rows % 16 == 0

    def tile(g, lag):
        return jnp.clip(g - lag, 0, last)

    def slab_specs(rows, n_slabs, width, src_layer):
        idx = lambda g: jnp.minimum(g, n_slabs - 1)
        return (pl.BlockSpec((None, rows, width), lambda g: (src_layer, idx(g), 0)),
                pl.BlockSpec((rows, width), lambda g: (idx(g), 0)))

    up_in, up_out = slab_specs(up_rows, D_MODEL // up_rows, 2 * D_FF, layer)
    dn_in, dn_out = slab_specs(dn_rows, D_FF // dn_rows, D_MODEL, layer)

    def x_spec(lag):
        return pl.BlockSpec((1, tt, D_MODEL), lambda g: (tile(g, lag) // nt, tile(g, lag) % nt, 0))

    kv_spec = pl.BlockSpec((None, 1, WINDOW, KV_WIDTH), lambda g: (layer, tile(g, 0) // nt, 0, 0))
    kv_shape = jax.ShapeDtypeStruct((DEPTH, b, WINDOW, KV_WIDTH), F32)
    return _layer_call(
        functools.partial(_mix_prompt_kernel, layer=layer, tiles_per_seq=nt, n_tiles=b * nt),
        layer, prev,
        grid=(b * nt + 2,),
        in_specs=[x_spec(0), x_spec(2), _prompt_mod_spec(layer, n_skip)] + w_specs_a
        + [pl.BlockSpec((3, tt, LANES), lambda g: (0, tile(g, 0) % nt, 0))] + w_specs_b
        + [up_in, dn_in],
        args=[x, x, mod3] + w_args_a + [rope] + w_args_b + [pp["w_up"], pp["w_down"]],
        out_specs=[x_spec(2), _const_spec((D_MODEL, IN_WIDTH)), _const_spec((D_MODEL, D_MODEL)),
                   up_out, dn_out, kv_spec, kv_spec],
        out_shapes=[jax.ShapeDtypeStruct((b, t, D_MODEL), F32),
                    jax.ShapeDtypeStruct((D_MODEL, IN_WIDTH), BF16),
                    jax.ShapeDtypeStruct((D_MODEL, D_MODEL), BF16),
                    jax.ShapeDtypeStruct((D_MODEL, 2 * D_FF), BF16),
                    jax.ShapeDtypeStruct((D_FF, D_MODEL), BF16), kv_shape, kv_shape],
        n_fresh=5,
        scratch=[
            pltpu.VMEM((2, tt, ATTN_WIDTH), BF16),
            pltpu.VMEM((3, 4, tt, LANES), BF16),
            pltpu.VMEM((3, 4, tt, LANES), BF16),
            pltpu.VMEM((2, tt, GMLP_WIDTH), F32),
            pltpu.VMEM((2, tt, GMLP_WIDTH), BF16),
            pltpu.VMEM((2, tt, D_MODEL), BF16),
        ],
        name="mix_prompt")


def _mix_sample_kernel(x_ref, sh_ref, sc_ref, ga_ref, ck_ref, cv_ref, gattn_ref, win_ref, gq_ref,
                       gk_ref, bdq_ref, bdk_ref, rope_ref, sinks_ref, lng_ref, lnb_ref, ws_ref,
                       bs_ref, wout_ref, y_ref, kout_ref, vout_ref, vgout_ref,
                       mod_scr, q_scr, s_scr, mix_ref, *, layer):
    n = x_ref.shape[0]
    sb = sh_ref.shape[0]
    ts = n // sb
    _expand_rows(mod_scr.at[0], sh_ref[...], ts)
    _expand_rows(mod_scr.at[1], sc_ref[...], ts)
    x = x_ref[...]
    h = _modulated_rmsnorm(x, gattn_ref[...], _read_cols(mod_scr.at[1]), _read_cols(mod_scr.at[0]))
    proj = _dot(h.astype(BF16), win_ref[...])
    _expand_rows(mod_scr.at[0], ga_ref[...], ts)

    cos_t, sin_dn, sin_up = rope_ref[0], rope_ref[1], rope_ref[2]
    q = _rope(_head_rmsnorm(proj[:, :ATTN_WIDTH], bdq_ref[...], gq_ref[...]), cos_t, sin_dn, sin_up)
    q = q * (HEAD_DIM ** -0.5)
    k = _rope(_head_rmsnorm(proj[:, ATTN_WIDTH:ATTN_WIDTH + KV_WIDTH], bdk_ref[...], gk_ref[...]),
              cos_t, sin_dn, sin_up)
    v = proj[:, ATTN_WIDTH + KV_WIDTH:ATTN_WIDTH + 2 * KV_WIDTH]
    kout_ref[...] = k
    vout_ref[...] = v

    lane = lax.broadcasted_iota(jnp.int32, (n, LANES), 1)
    group = N_HEADS // N_KV_HEADS
    qh = []
    for hd in range(N_HEADS):
        piece = q[:, (hd // 2) * LANES:(hd // 2 + 1) * LANES]
        if hd % 2 != hd // group:
            piece = pltpu.roll(piece, HEAD_DIM, axis=1)
        keep = (lane < HEAD_DIM) if hd // group == 0 else (lane >= HEAD_DIM)
        piece = jnp.where(keep, piece, 0.0)
        q_scr[:, hd * ts:(hd + 1) * ts, :] = piece.reshape(sb, ts, LANES)
        qh.append(piece.astype(BF16))

    def cache_scores(b, carry):
        s_scr[b] = _dot(q_scr[b].astype(BF16), _window_t(ck_ref, b))
        return carry

    lax.fori_loop(0, sb, cache_scores, 0, unroll=SEQ_UNROLL)

    kb, vb = k.astype(BF16), v.astype(BF16)
    grp = CHUNK
    rown = lax.broadcasted_iota(jnp.int32, (grp, grp), 0)
    coln = lax.broadcasted_iota(jnp.int32, (grp, grp), 1)
    seq_bits = ts.bit_length() - 1
    mask_new = ((rown >> seq_bits) == (coln >> seq_bits)) & (coln <= rown)
    mask_cache = (lax.broadcasted_iota(jnp.int32, (grp, WINDOW), 1)
                  > (lax.broadcasted_iota(jnp.int32, (grp, WINDOW), 0) & (ts - 1)))

    for g0 in range(0, n, grp):
        rows = slice(g0, g0 + grp)
        seqs = slice(g0 // ts, (g0 + grp) // ts)
        s_new = _dot_nt(jnp.concatenate([z[rows] for z in qh], axis=0), kb[rows])
        e_new = []
        for hd in range(N_HEADS):
            sink = sinks_ref[layer, hd]
            hrows = slice(hd * ts, (hd + 1) * ts)
            sn = jnp.where(mask_new, s_new[hd * grp:(hd + 1) * grp], NEG)
            sc_ = jnp.where(mask_cache, s_scr[seqs, hrows, :].reshape(grp, WINDOW), NEG)
            m = jnp.maximum(jnp.maximum(jnp.max(sn, axis=-1, keepdims=True),
                                        jnp.max(sc_, axis=-1, keepdims=True)), sink)
            en = jnp.exp(sn - m)
            ec = jnp.exp(sc_ - m)
            inv = 1.0 / (jnp.sum(en, axis=-1, keepdims=True) + jnp.sum(ec, axis=-1, keepdims=True)
                         + jnp.exp(sink - m))
            e_new.append((en * inv).astype(BF16))
            s_scr[seqs, hrows, :] = (ec * inv).reshape(grp // ts, ts, WINDOW)
        o_new = _dot(jnp.concatenate(e_new, axis=0), vb[rows])
        for hd in range(N_HEADS):
            q_scr[seqs, hd * ts:(hd + 1) * ts, :] = o_new[hd * grp:(hd + 1) * grp].reshape(
                grp // ts, ts, LANES)

    def cache_pv(b, carry):
        q_scr[b] += _dot_nt(s_scr[b].astype(BF16), _window_t(cv_ref, b))
        return carry

    lax.fori_loop(0, sb, cache_pv, 0, unroll=SEQ_UNROLL)

    for p in range(N_PAIRS):
        halves = []
        for hd in (2 * p, 2 * p + 1):
            o = q_scr[:, hd * ts:(hd + 1) * ts, :].reshape(n, LANES)
            if hd % 2 != hd // group:
                o = pltpu.roll(o, HEAD_DIM, axis=1)
            halves.append(o)
        mix_ref[:, p * LANES:(p + 1) * LANES] = jnp.where(
            lane < HEAD_DIM, halves[0], halves[1]).astype(BF16)

    u = jax.nn.gelu(proj[:, ATTN_WIDTH + 2 * KV_WIDTH:ATTN_WIDTH + 2 * KV_WIDTH + GMLP_WIDTH])
    vg = _gelu_layernorm(proj[:, IN_WIDTH - GMLP_WIDTH:], lng_ref[...], lnb_ref[...])
    vgout_ref[...] = vg
    _spatial_gate_into(mix_ref, u, vg, ws_ref, bs_ref)

    y_ref[...] = x + _read_cols(mod_scr.at[0]) * _dot(mix_ref[...], wout_ref[...])


def _sample_mod_specs(layer, sb, first):
    return [pl.BlockSpec((None, sb, D_MODEL), functools.partial(lambda i, w: (layer, i, w), w=first + m))
            for m in range(3)]


def _single_spec(shape):
    nd = len(shape)
    return pl.BlockSpec(shape, lambda *_: (0,) * nd, pipeline_mode=pl.Buffered(1))


def _mix_sample_call(layer, x, mod3, cache_k, cache_v, pp, w_in, w_out, rope, ts, prev):
    nt = x.shape[0]
    sb = SAMPLE_SEQS
    n = sb * ts
    w_specs_a, w_specs_b = _mix_weight_specs(
        layer, _single_spec((D_MODEL, IN_WIDTH)), _single_spec((D_MODEL, D_MODEL)))
    w_args_a, w_args_b = _mix_weight_args(pp, "sample", w_in, w_out)
    tok_spec = lambda w: pl.BlockSpec((None, n, w), lambda i: (layer, i, 0))
    tok_shape = lambda w: jax.ShapeDtypeStruct((DEPTH, nt, w), F32)
    cache_spec = pl.BlockSpec((None, sb, N_KV_HEADS, HEAD_DIM, WINDOW),
                              lambda i: (layer, i, 0, 0, 0))
    return _layer_call(
        functools.partial(_mix_sample_kernel, layer=layer), layer, prev,
        grid=(nt // n,),
        in_specs=[pl.BlockSpec((n, D_MODEL), lambda i: (i, 0))] + _sample_mod_specs(layer, sb, 0)
        + [cache_spec, cache_spec] + w_specs_a + [_const_spec((3, n, LANES))] + w_specs_b,
        args=[x, mod3, mod3, mod3, cache_k, cache_v] + w_args_a + [rope] + w_args_b,
        out_specs=[pl.BlockSpec((n, D_MODEL), lambda i: (i, 0)),
                   tok_spec(KV_WIDTH), tok_spec(KV_WIDTH), tok_spec(GMLP_WIDTH)],
        out_shapes=[jax.ShapeDtypeStruct((nt, D_MODEL), F32),
                    tok_shape(KV_WIDTH), tok_shape(KV_WIDTH), tok_shape(GMLP_WIDTH)],
        scratch=[
            pltpu.VMEM((2, D_MODEL // LANES, n, LANES), F32),
            pltpu.VMEM((sb, N_HEADS * ts, LANES), F32),
            pltpu.VMEM((sb, N_HEADS * ts, WINDOW), F32),
            pltpu.VMEM((n, D_MODEL), BF16),
        ],
        name="mix_sample")


def _shifted_rows(hu, before):
    rows, c = hu.shape
    hu3 = hu.reshape(rows // SUBLANES, SUBLANES, c)
    above = jnp.concatenate([before[None], hu3[:-1]], axis=0)
    sub = lax.broadcasted_iota(jnp.int32, (1, SUBLANES, c), 1)
    p1 = pltpu.roll(jnp.where(sub >= SUBLANES - 1, above, hu3), 1, axis=1)
    p2 = pltpu.roll(jnp.where(sub >= SUBLANES - 2, above, hu3), 2, axis=1)
    return p1.reshape(rows, c), p2.reshape(rows, c)


def _conv(hu, prev1, prev2, cw, cb):
    return cb + cw[0:1] * prev2 + cw[1:2] * prev1 + cw[2:3] * hu


def _ffn_chunks():
    return [(slice(j, j + FFN_CHUNK), slice(D_FF + j, D_FF + j + FFN_CHUNK))
            for j in range(0, D_FF, FFN_CHUNK)]


def _down_plan(n_chunks):
    bounds = list(range(0, n_chunks, DOWN_GROUP)) + [n_chunks]
    groups = [(a * FFN_CHUNK, b * FFN_CHUNK) for a, b in zip(bounds[:-1], bounds[1:])]
    return {b // FFN_CHUNK: (a, b) for a, b in groups[:-1]}, groups[-1]


def _down_into(acc_ref, act_scr, wdn_ref, a, b):
    contrib = _dot(act_scr[:, a:b], wdn_ref[a:b, :])
    if a == 0:
        acc_ref[...] = contrib
    else:
        acc_ref[...] += contrib


def _ffn_prompt_kernel(x_ref, mod_ref, gffn_ref, wup_ref, cw_ref, cb_ref, wdn_ref,
                       y_ref, cout_ref, carry, act_scr, acc_ref):
    tt = x_ref.shape[1]
    t = pl.program_id(1)

    @pl.when(t == 0)
    def _():
        carry[...] = jnp.zeros(carry.shape, F32)

    x = x_ref[0]
    seq = pl.program_id(0)
    chunks = _ffn_chunks()
    after, tail = _down_plan(len(chunks))
    hbs, firsts = [], []
    for rows in (slice(0, tt // 2), slice(tt // 2, tt)):
        hbs.append(_modulated_rmsnorm(x[rows], gffn_ref[...], _mod_row(mod_ref, seq, 4),
                                      _mod_row(mod_ref, seq, 3)).astype(BF16))
        firsts.append([_dot(hbs[-1], wup_ref[:, cols]) for cols in chunks[0]])
    hb = jnp.concatenate(hbs, axis=0)
    up = lambda halves: [_dot(hb, wup_ref[:, cols]) for cols in halves]
    ahead = [jnp.concatenate(pieces, axis=0) for pieces in zip(*firsts)]
    for j, halves in enumerate(chunks):
        hus = ahead
        if j + 1 < len(chunks):
            ahead = up(chunks[j + 1])
        if j in after:
            _down_into(acc_ref, act_scr, wdn_ref, *after[j])
        convs = []
        for hu, cols in zip(hus, halves):
            prev1, prev2 = _shifted_rows(hu, carry[:, cols])
            carry[:, cols] = hu[tt - SUBLANES:, :]
            cout_ref[0, :, cols] = hu[tt - (CONV_W - 1):, :]
            convs.append(_conv(hu, prev1, prev2, cw_ref[:, cols], cb_ref[:, cols]))
        act_scr[:, j * FFN_CHUNK:(j + 1) * FFN_CHUNK] = (
            jax.nn.silu(convs[0]) * convs[1]).astype(BF16)
    gate = _mod_row(mod_ref, seq, 5)
    for cols in (slice(0, D_MODEL // 2), slice(D_MODEL // 2, D_MODEL)):
        last = _dot(act_scr[:, tail[0]:tail[1]], wdn_ref[tail[0]:tail[1], cols])
        y_ref[0, :, cols] = x[:, cols] + gate[:, cols] * (acc_ref[:, cols] + last)


def _ffn_weight_specs(layer):
    return [
        _layer_spec(layer, (1, D_MODEL)),
        _single_spec((D_MODEL, 2 * D_FF)),
        _layer_spec(layer, (CONV_W, 2 * D_FF)),
        _layer_spec(layer, (1, 2 * D_FF)),
        _single_spec((D_FF, D_MODEL)),
    ]


def _ffn_weight_args(pp, w_up, w_down):
    return [pp["g_ffn"], w_up, pp["conv_w"], pp["conv_b"], w_down]


def _ffn_prompt_call(layer, x, mod3, n_skip, pp, w_up, w_down, prev):
    b, t, _ = x.shape
    tt = FFN_TILE
    return _layer_call(
        _ffn_prompt_kernel, layer, prev,
        grid=(b, t // tt),
        in_specs=[pl.BlockSpec((1, tt, D_MODEL), lambda i, j: (i, j, 0)),
                  _prompt_mod_spec(layer, n_skip)]
        + _ffn_weight_specs(layer),
        args=[x, mod3] + _ffn_weight_args(pp, w_up, w_down),
        out_specs=[pl.BlockSpec((1, tt, D_MODEL), lambda i, j: (i, j, 0)),
                   pl.BlockSpec((None, 1, CONV_W - 1, 2 * D_FF), lambda i, j: (layer, i, 0, 0))],
        out_shapes=[jax.ShapeDtypeStruct((b, t, D_MODEL), F32),
                    jax.ShapeDtypeStruct((DEPTH, b, CONV_W - 1, 2 * D_FF), F32)],
        scratch=[
            pltpu.VMEM((SUBLANES, 2 * D_FF), F32),
            pltpu.VMEM((tt, D_FF), BF16),
            pltpu.VMEM((tt, D_MODEL), F32),
        ],
        name="ffn_prompt")


def _ffn_sample_kernel(x_ref, sh_ref, sc_ref, gf_ref, hist_ref, gffn_ref, wup_ref, cw_ref,
                       cb_ref, wdn_ref, y_ref, cout_ref, mod_scr, act_scr, acc_ref):
    n = x_ref.shape[0]
    sb = sh_ref.shape[0]
    ts = n // sb
    sub = lax.broadcasted_iota(jnp.int32, (1, ts, FFN_CHUNK), 1)
    _expand_rows(mod_scr.at[0], sh_ref[...], ts)
    _expand_rows(mod_scr.at[1], sc_ref[...], ts)
    x = x_ref[...]
    hb = _modulated_rmsnorm(x, gffn_ref[...], _read_cols(mod_scr.at[1]),
                            _read_cols(mod_scr.at[0])).astype(BF16)
    _expand_rows(mod_scr.at[0], gf_ref[...], ts)
    chunks = _ffn_chunks()
    after, tail = _down_plan(len(chunks))
    up = lambda halves: [_dot(hb, wup_ref[:, cols]) for cols in halves]
    ahead = up(chunks[0])
    for j, halves in enumerate(chunks):
        hus = ahead
        if j + 1 < len(chunks):
            ahead = up(chunks[j + 1])
        if j in after:
            _down_into(acc_ref, act_scr, wdn_ref, *after[j])
        convs = []
        for hu, cols in zip(hus, halves):
            hu3 = hu.reshape(sb, ts, FFN_CHUNK)
            cout_ref[:, :, cols] = hu3[:, ts - (CONV_W - 1):, :]
            above = jnp.concatenate(
                [jnp.zeros((sb, ts - (CONV_W - 1), FFN_CHUNK), F32), hist_ref[:, :, cols]], axis=1)
            p1 = pltpu.roll(jnp.where(sub >= ts - 1, above, hu3), 1, axis=1)
            p2 = pltpu.roll(jnp.where(sub >= ts - 2, above, hu3), 2, axis=1)
            convs.append(_conv(hu, p1.reshape(n, FFN_CHUNK), p2.reshape(n, FFN_CHUNK),
                               cw_ref[:, cols], cb_ref[:, cols]))
        act_scr[:, j * FFN_CHUNK:(j + 1) * FFN_CHUNK] = (
            jax.nn.silu(convs[0]) * convs[1]).astype(BF16)
    _down_into(acc_ref, act_scr, wdn_ref, *tail)
    y_ref[...] = x + _read_cols(mod_scr.at[0]) * acc_ref[...]


def _ffn_sample_call(layer, x, mod3, cconv, pp, w_up, w_down, ts, prev):
    nt = x.shape[0]
    sb = SAMPLE_SEQS
    n = sb * ts
    tok_spec = pl.BlockSpec((n, D_MODEL), lambda i: (i, 0))
    hist_spec = pl.BlockSpec((None, sb, CONV_W - 1, 2 * D_FF), lambda i: (layer, i, 0, 0))
    return _layer_call(
        _ffn_sample_kernel, layer, prev,
        grid=(nt // n,),
        in_specs=[tok_spec] + _sample_mod_specs(layer, sb, 3) + [hist_spec]
        + _ffn_weight_specs(layer),
        args=[x, mod3, mod3, mod3, cconv] + _ffn_weight_args(pp, w_up, w_down),
        out_specs=[tok_spec, hist_spec],
        out_shapes=[jax.ShapeDtypeStruct((nt, D_MODEL), F32),
                    jax.ShapeDtypeStruct(cconv.shape, F32)],
        scratch=[
            pltpu.VMEM((2, D_MODEL // LANES, n, LANES), F32),
            pltpu.VMEM((n, D_FF), BF16),
            pltpu.VMEM((n, D_MODEL), F32),
        ],
        name="ffn_sample")


def _rope_tables(pos):
    half = ROPE_DIMS // 2
    inv = ROPE_THETA ** (-jnp.arange(0, ROPE_DIMS, 2, dtype=F32) / ROPE_DIMS)
    ang = pos.astype(F32)[:, None] * inv[None, :]
    cos, sin = jnp.cos(ang), jnp.sin(ang)
    n = pos.shape[0]
    rest = jnp.zeros((n, HEAD_DIM - ROPE_DIMS), F32)
    zeros = jnp.zeros((n, half), F32)
    cos_t = jnp.concatenate([cos, cos, rest + 1.0], axis=1)
    sin_dn = jnp.concatenate([-sin, zeros, rest], axis=1)
    sin_up = jnp.concatenate([zeros, sin, rest], axis=1)
    return jnp.stack([jnp.tile(z, (1, LANES // HEAD_DIM)) for z in (cos_t, sin_dn, sin_up)])


def _block_diag_ones(width):
    idx = jnp.arange(width) // HEAD_DIM
    return (idx[:, None] == idx[None, :]).astype(BF16)


def _prepare_params(dec_seq, g_attn, w_in, g_q, g_k, sinks, ln_g, ln_b, w_s, b_s, w_out,
                    g_ffn, w_ffn_in, conv_w, conv_b, w_ffn_out):
    causal = jnp.tril(jnp.ones((CHUNK, CHUNK), dtype=bool))
    ws = jnp.where(causal, w_s, 0.0)
    seqs_per_chunk = CHUNK // dec_seq
    eye = jnp.eye(seqs_per_chunk, dtype=F32)
    ws_sample = jnp.einsum("ab,lgts->lgatbs", eye, ws[:, :, :dec_seq, :dec_seq]).reshape(w_s.shape)
    return {
        "g_attn": g_attn[:, None, :], "g_ffn": g_ffn[:, None, :],
        "w_in": w_in, "w_out": w_out,
        "g_q": jnp.tile(g_q, (1, N_HEADS))[:, None, :],
        "g_k": jnp.tile(g_k, (1, N_KV_HEADS))[:, None, :],
        "bd_q": _block_diag_ones(ATTN_WIDTH), "bd_k": _block_diag_ones(KV_WIDTH),
        "sinks": sinks,
        "ln_g": ln_g.reshape(DEPTH, 1, GMLP_WIDTH), "ln_b": ln_b.reshape(DEPTH, 1, GMLP_WIDTH),
        "ws_prompt": ws.astype(BF16), "bs_prompt": jnp.swapaxes(b_s, 1, 2),
        "ws_sample": ws_sample.astype(BF16),
        "bs_sample": jnp.swapaxes(jnp.tile(b_s[:, :, :dec_seq], (1, 1, seqs_per_chunk)), 1, 2),
        "w_up": w_ffn_in, "w_down": w_ffn_out,
        "conv_w": conv_w, "conv_b": conv_b[:, None, :],
    }


def kernel(x_prompt, x_sample, cache_k, cache_v, cache_conv, c_prompt, c_sample, w_ada, b_ada,
           g_attn, w_in, g_q, g_k, sinks, ln_g, ln_b, w_s, b_s, w_out, g_ffn, w_ffn_in, conv_w,
           conv_b, w_ffn_out):
    nbp, seq, _ = x_prompt.shape
    nbs, dec_seq, _ = x_sample.shape
    assert seq % PROMPT_TILE == 0 and PROMPT_TILE % CHUNK == 0 and nbs % SAMPLE_SEQS == 0
    assert seq % FFN_TILE == 0 and FFN_TILE % SUBLANES == 0 and D_FF % FFN_CHUNK == 0
    assert dec_seq == SUBLANES and CHUNK % dec_seq == 0 and cache_k.shape[2] == WINDOW
    assert nbs % SUBLANES == 0 and nbp <= SUBLANES

    n_c = nbs + nbp
    c_all = jnp.concatenate([c_sample, c_prompt, jnp.zeros((-n_c % SUBLANES, D_MODEL), F32)])
    mod3 = _ada_call(c_all, w_ada, b_ada)

    pp = _prepare_params(dec_seq, g_attn, w_in, g_q, g_k, sinks, ln_g, ln_b, w_s, b_s, w_out,
                         g_ffn, w_ffn_in, conv_w, conv_b, w_ffn_out)
    rope_p = _rope_tables(jnp.arange(seq, dtype=jnp.int32))
    rope_s = jnp.tile(_rope_tables(PAST_LEN + jnp.arange(dec_seq, dtype=jnp.int32)),
                      (1, SAMPLE_SEQS, 1))
    ck = jnp.transpose(cache_k, (0, 1, 3, 4, 2))
    cv = jnp.transpose(cache_v, (0, 1, 3, 4, 2))

    xp = x_prompt
    xs = x_sample.reshape(nbs * dec_seq, D_MODEL)
    mix_p = ffn_p = mix_s = ffn_s = None
    for l in range(DEPTH):
        xp, w_in_b, w_out_b, w_up_b, w_dn_b, *mix_p = _mix_prompt_call(
            l, xp, mod3, nbs, pp, rope_p, mix_p)
        xp, *ffn_p = _ffn_prompt_call(l, xp, mod3, nbs, pp, w_up_b, w_dn_b, ffn_p)
        xs, *mix_s = _mix_sample_call(l, xs, mod3, ck, cv, pp, w_in_b, w_out_b, rope_s, dec_seq,
                                      mix_s)
        xs, *ffn_s = _ffn_sample_call(l, xs, mod3, cache_conv, pp, w_up_b, w_dn_b, dec_seq, ffn_s)

    kv_p = (DEPTH, nbp, WINDOW, N_KV_HEADS, HEAD_DIM)
    kv_s = (DEPTH, nbs, dec_seq, N_KV_HEADS, HEAD_DIM)
    return (xp, xs.reshape(nbs, dec_seq, D_MODEL),
            mix_p[0].reshape(kv_p), mix_p[1].reshape(kv_p), ffn_p[0],
            mix_s[0].reshape(kv_s), mix_s[1].reshape(kv_s),
            mix_s[2].reshape(DEPTH, nbs, dec_seq, GMLP_WIDTH),
            ffn_s[0])
```

```python
import functools

import jax
import jax.numpy as jnp
from jax import lax
from jax.experimental import pallas as pl
from jax.experimental.pallas import tpu as pltpu

D_MODEL = 1024
DEPTH = 2
HEAD_DIM = 64
N_HEADS = 8
N_KV_HEADS = 2
KV_WIDTH = N_KV_HEADS * HEAD_DIM
WINDOW = 128
ROPE_THETA = 500000.0
ROPE_DIMS = HEAD_DIM // 4
ATTN_WIDTH = N_HEADS * HEAD_DIM
GMLP_WIDTH = D_MODEL - ATTN_WIDTH
GMLP_GROUPS = 4
GMLP_GW = GMLP_WIDTH // GMLP_GROUPS
CHUNK = 128
IN_WIDTH = ATTN_WIDTH + 2 * KV_WIDTH + 2 * GMLP_WIDTH
D_FF = 2816
CONV_W = 3
N_MOD = 6
EPS = 1e-6
NEG = -1e30
PAST_LEN = 16384

LANES = 128
SUBLANES = 8
BF16_SUBLANES = 16
N_PAIRS = N_HEADS // 2
VMEM_LIMIT = 56 * 1024 * 1024

PROMPT_TILE = 512
CAST_SLABS_UP = 32
CAST_SLABS_DOWN = 16
OUT_COLS = 256
MIX_INTERLEAVE = 14
FFN_TILE = 1024
SAMPLE_SEQS = 64
SEQ_UNROLL = 8
FFN_CHUNK = 256
DOWN_GROUP = 2
ADA_TILE = 3072

F32 = jnp.float32
BF16 = jnp.bfloat16


def _dot(a, b):
    return jnp.dot(a, b, preferred_element_type=F32)


def _dot_nt(a, b):
    return lax.dot_general(a, b, (((1,), (1,)), ((), ())), preferred_element_type=F32)


def _const_spec(shape):
    nd = len(shape)
    return pl.BlockSpec(shape, lambda *_: (0,) * nd)


def _layer_spec(layer, tail, single_buffer=False):
    nd = len(tail)
    kw = {"pipeline_mode": pl.Buffered(1)} if single_buffer else {}
    return pl.BlockSpec((None,) + tuple(tail), lambda *_: (layer,) + (0,) * nd, **kw)


def _skip_aliased(body, n_in, n_alias):
    if n_alias == 0:
        return body
    return lambda *refs: body(*refs[:n_in], *refs[n_in + n_alias:])


def _layer_call(body, layer, prev, *, grid, in_specs, args, out_specs, out_shapes, scratch, name,
                n_fresh=1):
    n_in = len(args)
    aliased = [] if prev is None else list(prev)
    return pl.pallas_call(
        _skip_aliased(body, n_in, len(aliased)),
        grid=grid,
        in_specs=list(in_specs) + [pl.BlockSpec(memory_space=pl.ANY)] * len(aliased),
        out_specs=out_specs,
        out_shape=out_shapes,
        input_output_aliases={n_in + i: n_fresh + i for i in range(len(aliased))},
        scratch_shapes=scratch,
        compiler_params=pltpu.CompilerParams(
            dimension_semantics=("arbitrary",) * len(grid), vmem_limit_bytes=VMEM_LIMIT),
        name=name,
    )(*args, *aliased)


def _ada_kernel(c_ref, w_ref, b_ref, o_ref):
    c = c_ref[...]
    a = (c * jax.nn.sigmoid(c)).astype(BF16)
    o_ref[0] = _dot(a, w_ref[0].astype(BF16)) + b_ref[0]


def _ada_call(c_all, w_ada, b_ada):
    n = c_all.shape[0]
    return pl.pallas_call(
        _ada_kernel,
        grid=(DEPTH, N_MOD * D_MODEL // ADA_TILE),
        in_specs=[
            pl.BlockSpec((n, D_MODEL), lambda l, j: (0, 0)),
            pl.BlockSpec((1, D_MODEL, ADA_TILE), lambda l, j: (l, 0, j)),
            pl.BlockSpec((1, 1, ADA_TILE), lambda l, j: (l, 0, j)),
        ],
        out_specs=pl.BlockSpec((1, n, ADA_TILE), lambda l, j: (l, 0, j)),
        out_shape=jax.ShapeDtypeStruct((DEPTH, n, N_MOD * D_MODEL), F32),
        compiler_params=pltpu.CompilerParams(
            dimension_semantics=("arbitrary", "arbitrary"), vmem_limit_bytes=VMEM_LIMIT),
        name="ada_mod",
    )(c_all, w_ada, b_ada.reshape(DEPTH, 1, N_MOD * D_MODEL))


def _modulated_rmsnorm(x, gain, scale, shift):
    r = lax.rsqrt(jnp.mean(x * x, axis=-1, keepdims=True) + EPS)
    return (x * r) * (gain * (1.0 + scale)) + shift


def _head_rmsnorm(z, ones_bd, gain):
    ssq = _dot((z * z).astype(BF16), ones_bd)
    return z * lax.rsqrt(ssq * (1.0 / HEAD_DIM) + EPS) * gain


def _rope(z, cos_t, sin_dn, sin_up):
    half = ROPE_DIMS // 2
    cols = []
    for p in range(z.shape[-1] // LANES):
        zp = z[:, p * LANES:(p + 1) * LANES]
        cols.append(zp * cos_t
                    + pltpu.roll(zp, LANES - half, axis=1) * sin_dn
                    + pltpu.roll(zp, half, axis=1) * sin_up)
    return cols[0] if len(cols) == 1 else jnp.concatenate(cols, axis=1)


def _split_heads(z):
    lane = lax.broadcasted_iota(jnp.int32, z.shape, 1)
    lo = lane < HEAD_DIM
    zs = pltpu.roll(z, HEAD_DIM, axis=1)
    zero = jnp.zeros_like(z)
    return (jnp.where(lo, z, zero), jnp.where(lo, zero, zs),
            jnp.where(lo, zs, zero), jnp.where(lo, zero, z))


def _gelu_layernorm(zv, ln_g, ln_b):
    gv = jax.nn.gelu(zv)
    cols = []
    for g in range(GMLP_GROUPS):
        xg = gv[:, g * GMLP_GW:(g + 1) * GMLP_GW]
        mu = jnp.mean(xg, axis=-1, keepdims=True)
        xc = xg - mu
        var = jnp.mean(xc * xc, axis=-1, keepdims=True)
        cols.append(xc * lax.rsqrt(var + EPS))
    return jnp.concatenate(cols, axis=1) * ln_g + ln_b


def _spatial_gate_into(mix_ref, u, vg, ws_ref, bs_ref):
    vgb = vg.astype(BF16)
    for g in range(GMLP_GROUPS):
        w = ws_ref[g]
        bias = bs_ref[:, g:g + 1]
        cs = slice(g * GMLP_GW, (g + 1) * GMLP_GW)
        for c in range(u.shape[0] // CHUNK):
            rs = slice(c * CHUNK, (c + 1) * CHUNK)
            z = _dot(w, vgb[rs, cs]) + bias
            mix_ref[rs, ATTN_WIDTH + g * GMLP_GW:ATTN_WIDTH + (g + 1) * GMLP_GW] = (
                u[rs, cs] * z).astype(BF16)


def _lane_pair(a, b, shape):
    lane = lax.broadcasted_iota(jnp.int32, shape, 1)
    return jnp.where(lane < HEAD_DIM, a, b)


def _prompt_mod_spec(layer, n_skip):
    return pl.BlockSpec((None, SUBLANES, N_MOD * D_MODEL),
                        lambda *_: (layer, n_skip // SUBLANES, 0))


def _mod_row(mod_ref, seq, m):
    return mod_ref[pl.ds(seq, 1), m * D_MODEL:(m + 1) * D_MODEL]


def _window_t(ref, b):
    return ref[b].reshape(KV_WIDTH, WINDOW).astype(BF16)


def _expand_rows(dst_ref, src, reps):
    for c in range(dst_ref.shape[0]):
        piece = src[:, c * LANES:(c + 1) * LANES]
        for t in range(reps):
            dst_ref[c, pl.ds(t, src.shape[0], stride=reps), :] = piece


def _read_cols(ref, rows=slice(None)):
    return jnp.concatenate([ref[c, rows, :] for c in range(ref.shape[0])], axis=1)


def _interleave(tasks, width):
    pending = iter(tasks)
    active = []
    while True:
        while len(active) < width:
            task = next(pending, None)
            if task is None:
                break
            active.append(task)
        if not active:
            return
        for task in list(active):
            if next(task, "done") == "done":
                active.remove(task)


def _mix_prompt_kernel(xa_ref, xc_ref, mod_ref, gattn_ref, win32_ref, gq_ref, gk_ref,
                       bdq_ref, bdk_ref, rope_ref, sinks_ref, lng_ref, lnb_ref, ws_ref, bs_ref,
                       wout32_ref, wup32_ref, wdn32_ref,
                       y_ref, win_ref, wout_ref, wup_ref, wdn_ref, kout_ref, vout_ref,
                       q_ring, k_ring, v_ring, u_ring, vg_ring, mix_ring,
                       *, layer, tiles_per_seq, n_tiles):
    tt = xa_ref.shape[1]
    g = pl.program_id(0)

    @pl.when(g == 0)
    def _():
        win_ref[...] = win32_ref[...].astype(BF16)
        wout_ref[...] = wout32_ref[...].astype(BF16)
        for ring in (q_ring, k_ring, v_ring, u_ring, vg_ring, mix_ring):
            ring[...] = jnp.zeros(ring.shape, ring.dtype)

    wup_ref[...] = wup32_ref[...].astype(BF16)
    wdn_ref[...] = wdn32_ref[...].astype(BF16)

    cur, prv = g % 2, (g + 1) % 2
    seq_a = jnp.minimum(g, n_tiles - 1) // tiles_per_seq
    seq_c = jnp.clip(g - 2, 0, n_tiles - 1) // tiles_per_seq
    kv_a, kv_b, kv_p = g % 3, (g + 2) % 3, (g + 1) % 3

    cos_t, sin_dn, sin_up = rope_ref[0], rope_ref[1], rope_ref[2]
    env = {}

    def norm_task():
        env["hb"] = _modulated_rmsnorm(
            xa_ref[0], gattn_ref[...], _mod_row(mod_ref, seq_a, 1),
            _mod_row(mod_ref, seq_a, 0)).astype(BF16)
        yield

    def q_task():
        q = _dot(env["hb"], win_ref[:, :ATTN_WIDTH])
        yield
        ssq = _dot((q * q).astype(BF16), bdq_ref[...])
        yield
        q = q * lax.rsqrt(ssq * (1.0 / HEAD_DIM) + EPS) * gq_ref[...]
        q_ring[cur] = (_rope(q, cos_t, sin_dn, sin_up) * (HEAD_DIM ** -0.5)).astype(BF16)

    def kv_task():
        kv = _dot(env["hb"], win_ref[:, ATTN_WIDTH:ATTN_WIDTH + 2 * KV_WIDTH])
        yield
        k, v = kv[:, :KV_WIDTH], kv[:, KV_WIDTH:]
        ssq = _dot((k * k).astype(BF16), bdk_ref[...])
        vout_ref[0] = v[tt - WINDOW:, :]
        for i, vz in enumerate(_split_heads(v)):
            v_ring[kv_a, i] = vz.astype(BF16)
        yield
        k = _rope(k * lax.rsqrt(ssq * (1.0 / HEAD_DIM) + EPS) * gk_ref[...], cos_t, sin_dn, sin_up)
        kout_ref[0] = k[tt - WINDOW:, :]
        for i, kz in enumerate(_split_heads(k)):
            k_ring[kv_a, i] = kz.astype(BF16)

    def u_task():
        zu = _dot(env["hb"], win_ref[:, ATTN_WIDTH + 2 * KV_WIDTH:IN_WIDTH - GMLP_WIDTH])
        yield
        u_ring[cur] = jax.nn.gelu(zu)

    def vg_task():
        zv = _dot(env["hb"], win_ref[:, IN_WIDTH - GMLP_WIDTH:])
        yield
        vg_ring[cur] = _gelu_layernorm(zv, lng_ref[...], lnb_ref[...]).astype(BF16)

    row = lax.broadcasted_iota(jnp.int32, (WINDOW, 2 * WINDOW), 0)
    col = lax.broadcasted_iota(jnp.int32, (WINDOW, 2 * WINDOW), 1)
    band = (col - row >= 1) & (col - row <= WINDOW)
    seq_start = (g + tiles_per_seq - 1) % tiles_per_seq == 0
    band_first = band & (col >= jnp.where(seq_start, WINDOW, 0))
    tail = slice(tt - WINDOW, tt)

    def band_rows(ring, idx, i):
        if i == 0:
            return [ring[kv_p, idx, tail, :], ring[kv_b, idx, 0:WINDOW, :]]
        return [ring[kv_b, idx, (i - 1) * WINDOW:(i + 1) * WINDOW, :]]

    def attn_task(i, kv):
        mask = band_first if i == 0 else band
        rows = slice(i * WINDOW, (i + 1) * WINDOW)
        pairs = (2 * kv, 2 * kv + 1)
        kblk = jnp.concatenate(band_rows(k_ring, 2 * kv, i) + band_rows(k_ring, 2 * kv + 1, i), axis=0)
        q2 = jnp.concatenate([q_ring[prv, rows, p * LANES:(p + 1) * LANES] for p in pairs], axis=0)
        s = _dot_nt(q2, kblk)
        yield
        es, scales = [], []
        for j, p in enumerate(pairs):
            e_pair, invs = [], []
            for hh in range(2):
                sink = sinks_ref[layer, 2 * p + hh]
                sh_ = jnp.where(mask, s[j * WINDOW:(j + 1) * WINDOW,
                                        hh * 2 * WINDOW:(hh + 1) * 2 * WINDOW], NEG)
                m = jnp.maximum(jnp.max(sh_, axis=-1, keepdims=True), sink)
                e = jnp.exp(sh_ - m)
                e_pair.append(e.astype(BF16))
                invs.append(1.0 / (jnp.sum(e, axis=-1, keepdims=True) + jnp.exp(sink - m)))
            es.append(jnp.concatenate(e_pair, axis=1))
            scales.append(_lane_pair(invs[0], invs[1], (WINDOW, LANES)))
        vblk = jnp.concatenate(band_rows(v_ring, 2 * kv, i) + band_rows(v_ring, 2 * kv + 1, i), axis=0)
        o = _dot(jnp.concatenate(es, axis=0), vblk)
        yield
        for j, p in enumerate(pairs):
            mix_ring[cur, rows, p * LANES:(p + 1) * LANES] = (
                o[j * WINDOW:(j + 1) * WINDOW] * scales[j]).astype(BF16)

    def gate_task(c, grp):
        rs = slice(c * CHUNK, (c + 1) * CHUNK)
        cs = slice(grp * GMLP_GW, (grp + 1) * GMLP_GW)
        z = _dot(ws_ref[grp], vg_ring[prv, rs, cs])
        yield
        z = z + bs_ref[:, grp:grp + 1]
        mix_ring[cur, rs, ATTN_WIDTH + grp * GMLP_GW:ATTN_WIDTH + (grp + 1) * GMLP_GW] = (
            u_ring[prv, rs, cs] * z).astype(BF16)

    def out_task(c):
        cols = slice(c * OUT_COLS, (c + 1) * OUT_COLS)
        r = _dot(mix_ring[prv], wout_ref[:, cols])
        yield
        y_ref[0, :, cols] = xc_ref[0, :, cols] + _mod_row(mod_ref, seq_c, 2)[:, cols] * r

    def trace(stages):
        n_blocks = tt // WINDOW
        big = [(out_task, (0,), "C"), (norm_task, (), "A"), (vg_task, (), "A"),
               (out_task, (1,), "C"), (u_task, (), "A"), (q_task, (), "A"),
               (out_task, (2,), "C"), (kv_task, (), "A"), (out_task, (3,), "C")]
        big = [make(*args) for make, args, stage in big if stage in stages]
        small = []
        if "B" in stages:
            attn = [attn_task(i, kv) for i in range(n_blocks) for kv in range(N_KV_HEADS)]
            gate = [gate_task(c, grp) for c in range(n_blocks) for grp in range(GMLP_GROUPS)]
            small = [t for trio in zip(attn, gate[0::2], gate[1::2]) for t in trio]
        per_big = -(-len(small) // len(big))
        order = []
        for task in big:
            order.append(task)
            order.extend(small[:per_big])
            small = small[per_big:]
        _interleave(order + small, MIX_INTERLEAVE)

    head, tail_end = g < 2, g >= n_tiles
    pl.when(head)(lambda: trace("AB"))
    pl.when(jnp.logical_not(head | tail_end))(lambda: trace("ABC"))
    pl.when(tail_end)(lambda: trace("BC"))


def _mix_weight_specs(layer, w_in_spec, w_out_spec):
    return [
        _layer_spec(layer, (1, D_MODEL)),
        w_in_spec,
        _layer_spec(layer, (1, ATTN_WIDTH)),
        _layer_spec(layer, (1, KV_WIDTH)),
        _const_spec((ATTN_WIDTH, ATTN_WIDTH)),
        _const_spec((KV_WIDTH, KV_WIDTH)),
    ], [
        pl.BlockSpec(memory_space=pltpu.SMEM),
        _layer_spec(layer, (1, GMLP_WIDTH)),
        _layer_spec(layer, (1, GMLP_WIDTH)),
        _layer_spec(layer, (GMLP_GROUPS, CHUNK, CHUNK)),
        _layer_spec(layer, (CHUNK, GMLP_GROUPS)),
        w_out_spec,
    ]


def _mix_weight_args(pp, kind, w_in, w_out):
    return ([pp["g_attn"], w_in, pp["g_q"], pp["g_k"], pp["bd_q"], pp["bd_k"]],
            [pp["sinks"], pp["ln_g"], pp["ln_b"], pp["ws_" + kind], pp["bs_" + kind], w_out])


def _mix_prompt_call(layer, x, mod3, n_skip, pp, rope, prev):
    b, t, _ = x.shape
    tt = PROMPT_TILE
    nt = t // tt
    last = b * nt - 1
    w_specs_a, w_specs_b = _mix_weight_specs(
        layer, _layer_spec(layer, (D_MODEL, IN_WIDTH), single_buffer=True),
        _layer_spec(layer, (D_MODEL, D_MODEL), single_buffer=True))
    w_args_a, w_args_b = _mix_weight_args(pp, "prompt", pp["w_in"], pp["w_out"])
    up_rows, dn_rows = D_MODEL // CAST_SLABS_UP, D_FF // CAST_SLABS_DOWN
    assert up_rows % BF16_SUBLANES == 0 and dn_rows % BF16_SUBLANES == 0
    assert b * nt >= max(CAST_SLABS_UP, CAST_SLABS_DOWN)

    def tile(g, lag):
        return jnp.clip(g - lag, 0, last)

    def slab_specs(rows, n_slabs, width, src_layer):
        idx = lambda g: jnp.minimum(g, n_slabs - 1)
        return (pl.BlockSpec((None, rows, width), lambda g: (src_layer, idx(g), 0)),
                pl.BlockSpec((rows, width), lambda g: (idx(g), 0)))

    up_in, up_out = slab_specs(up_rows, D_MODEL // up_rows, 2 * D_FF, layer)
    dn_in, dn_out = slab_specs(dn_rows, D_FF // dn_rows, D_MODEL, layer)

    def x_spec(lag):
        return pl.BlockSpec((1, tt, D_MODEL), lambda g: (tile(g, lag) // nt, tile(g, lag) % nt, 0))

    kv_spec = pl.BlockSpec((None, 1, WINDOW, KV_WIDTH), lambda g: (layer, tile(g, 0) // nt, 0, 0))
    kv_shape = jax.ShapeDtypeStruct((DEPTH, b, WINDOW, KV_WIDTH), F32)
    return _layer_call(
        functools.partial(_mix_prompt_kernel, layer=layer, tiles_per_seq=nt, n_tiles=b * nt),
        layer, prev,
        grid=(b * nt + 2,),
        in_specs=[x_spec(0), x_spec(2), _prompt_mod_spec(layer, n_skip)] + w_specs_a
        + [pl.BlockSpec((3, tt, LANES), lambda g: (0, tile(g, 0) % nt, 0))] + w_specs_b
        + [up_in, dn_in],
        args=[x, x, mod3] + w_args_a + [rope] + w_args_b + [pp["w_up"], pp["w_down"]],
        out_specs=[x_spec(2), _const_spec((D_MODEL, IN_WIDTH)), _const_spec((D_MODEL, D_MODEL)),
                   up_out, dn_out, kv_spec, kv_spec],
        out_shapes=[jax.ShapeDtypeStruct((b, t, D_MODEL), F32),
                    jax.ShapeDtypeStruct((D_MODEL, IN_WIDTH), BF16),
                    jax.ShapeDtypeStruct((D_MODEL, D_MODEL), BF16),
                    jax.ShapeDtypeStruct((D_MODEL, 2 * D_FF), BF16),
                    jax.ShapeDtypeStruct((D_FF, D_MODEL), BF16), kv_shape, kv_shape],
        n_fresh=5,
        scratch=[
            pltpu.VMEM((2, tt, ATTN_WIDTH), BF16),
            pltpu.VMEM((3, 4, tt, LANES), BF16),
            pltpu.VMEM((3, 4, tt, LANES), BF16),
            pltpu.VMEM((2, tt, GMLP_WIDTH), F32),
            pltpu.VMEM((2, tt, GMLP_WIDTH), BF16),
            pltpu.VMEM((2, tt, D_MODEL), BF16),
        ],
        name="mix_prompt")


def _mix_sample_kernel(x_ref, sh_ref, sc_ref, ga_ref, ck_ref, cv_ref, gattn_ref, win_ref, gq_ref,
                       gk_ref, bdq_ref, bdk_ref, rope_ref, sinks_ref, lng_ref, lnb_ref, ws_ref,
                       bs_ref, wout_ref, y_ref, kout_ref, vout_ref, vgout_ref,
                       mod_scr, q_scr, s_scr, mix_ref, *, layer):
    n = x_ref.shape[0]
    sb = sh_ref.shape[0]
    ts = n // sb
    _expand_rows(mod_scr.at[0], sh_ref[...], ts)
    _expand_rows(mod_scr.at[1], sc_ref[...], ts)
    x = x_ref[...]
    h = _modulated_rmsnorm(x, gattn_ref[...], _read_cols(mod_scr.at[1]), _read_cols(mod_scr.at[0]))
    proj = _dot(h.astype(BF16), win_ref[...])
    _expand_rows(mod_scr.at[0], ga_ref[...], ts)

    cos_t, sin_dn, sin_up = rope_ref[0], rope_ref[1], rope_ref[2]
    q = _rope(_head_rmsnorm(proj[:, :ATTN_WIDTH], bdq_ref[...], gq_ref[...]), cos_t, sin_dn, sin_up)
    q = q * (HEAD_DIM ** -0.5)
    k = _rope(_head_rmsnorm(proj[:, ATTN_WIDTH:ATTN_WIDTH + KV_WIDTH], bdk_ref[...], gk_ref[...]),
              cos_t, sin_dn, sin_up)
    v = proj[:, ATTN_WIDTH + KV_WIDTH:ATTN_WIDTH + 2 * KV_WIDTH]
    kout_ref[...] = k
    vout_ref[...] = v

    lane = lax.broadcasted_iota(jnp.int32, (n, LANES), 1)
    group = N_HEADS // N_KV_HEADS
    qh = []
    for hd in range(N_HEADS):
        piece = q[:, (hd // 2) * LANES:(hd // 2 + 1) * LANES]
        if hd % 2 != hd // group:
            piece = pltpu.roll(piece, HEAD_DIM, axis=1)
        keep = (lane < HEAD_DIM) if hd // group == 0 else (lane >= HEAD_DIM)
        piece = jnp.where(keep, piece, 0.0)
        q_scr[:, hd * ts:(hd + 1) * ts, :] = piece.reshape(sb, ts, LANES)
        qh.append(piece.astype(BF16))

    def cache_scores(b, carry):
        s_scr[b] = _dot(q_scr[b].astype(BF16), _window_t(ck_ref, b))
        return carry

    lax.fori_loop(0, sb, cache_scores, 0, unroll=SEQ_UNROLL)

    kb, vb = k.astype(BF16), v.astype(BF16)
    grp = CHUNK
    rown = lax.broadcasted_iota(jnp.int32, (grp, grp), 0)
    coln = lax.broadcasted_iota(jnp.int32, (grp, grp), 1)
    seq_bits = ts.bit_length() - 1
    mask_new = ((rown >> seq_bits) == (coln >> seq_bits)) & (coln <= rown)
    mask_cache = (lax.broadcasted_iota(jnp.int32, (grp, WINDOW), 1)
                  > (lax.broadcasted_iota(jnp.int32, (grp, WINDOW), 0) & (ts - 1)))

    for g0 in range(0, n, grp):
        rows = slice(g0, g0 + grp)
        seqs = slice(g0 // ts, (g0 + grp) // ts)
        s_new = _dot_nt(jnp.concatenate([z[rows] for z in qh], axis=0), kb[rows])
        e_new = []
        for hd in range(N_HEADS):
            sink = sinks_ref[layer, hd]
            hrows = slice(hd * ts, (hd + 1) * ts)
            sn = jnp.where(mask_new, s_new[hd * grp:(hd + 1) * grp], NEG)
            sc_ = jnp.where(mask_cache, s_scr[seqs, hrows, :].reshape(grp, WINDOW), NEG)
            m = jnp.maximum(jnp.maximum(jnp.max(sn, axis=-1, keepdims=True),
                                        jnp.max(sc_, axis=-1, keepdims=True)), sink)
            en = jnp.exp(sn - m)
            ec = jnp.exp(sc_ - m)
            inv = 1.0 / (jnp.sum(en, axis=-1, keepdims=True) + jnp.sum(ec, axis=-1, keepdims=True)
                         + jnp.exp(sink - m))
            e_new.append((en * inv).astype(BF16))
            s_scr[seqs, hrows, :] = (ec * inv).reshape(grp // ts, ts, WINDOW)
        o_new = _dot(jnp.concatenate(e_new, axis=0), vb[rows])
        for hd in range(N_HEADS):
            q_scr[seqs, hd * ts:(hd + 1) * ts, :] = o_new[hd * grp:(hd + 1) * grp].reshape(
                grp // ts, ts, LANES)

    def cache_pv(b, carry):
        q_scr[b] += _dot_nt(s_scr[b].astype(BF16), _window_t(cv_ref, b))
        return carry

    lax.fori_loop(0, sb, cache_pv, 0, unroll=SEQ_UNROLL)

    for p in range(N_PAIRS):
        halves = []
        for hd in (2 * p, 2 * p + 1):
            o = q_scr[:, hd * ts:(hd + 1) * ts, :].reshape(n, LANES)
            if hd % 2 != hd // group:
                o = pltpu.roll(o, HEAD_DIM, axis=1)
            halves.append(o)
        mix_ref[:, p * LANES:(p + 1) * LANES] = jnp.where(
            lane < HEAD_DIM, halves[0], halves[1]).astype(BF16)

    u = jax.nn.gelu(proj[:, ATTN_WIDTH + 2 * KV_WIDTH:ATTN_WIDTH + 2 * KV_WIDTH + GMLP_WIDTH])
    vg = _gelu_layernorm(proj[:, IN_WIDTH - GMLP_WIDTH:], lng_ref[...], lnb_ref[...])
    vgout_ref[...] = vg
    _spatial_gate_into(mix_ref, u, vg, ws_ref, bs_ref)

    y_ref[...] = x + _read_cols(mod_scr.at[0]) * _dot(mix_ref[...], wout_ref[...])


def _sample_mod_specs(layer, sb, first):
    return [pl.BlockSpec((None, sb, D_MODEL), functools.partial(lambda i, w: (layer, i, w), w=first + m))
            for m in range(3)]


def _single_spec(shape):
    nd = len(shape)
    return pl.BlockSpec(shape, lambda *_: (0,) * nd, pipeline_mode=pl.Buffered(1))


def _mix_sample_call(layer, x, mod3, cache_k, cache_v, pp, w_in, w_out, rope, ts, prev):
    nt = x.shape[0]
    sb = SAMPLE_SEQS
    n = sb * ts
    w_specs_a, w_specs_b = _mix_weight_specs(
        layer, _single_spec((D_MODEL, IN_WIDTH)), _single_spec((D_MODEL, D_MODEL)))
    w_args_a, w_args_b = _mix_weight_args(pp, "sample", w_in, w_out)
    tok_spec = lambda w: pl.BlockSpec((None, n, w), lambda i: (layer, i, 0))
    tok_shape = lambda w: jax.ShapeDtypeStruct((DEPTH, nt, w), F32)
    cache_spec = pl.BlockSpec((None, sb, N_KV_HEADS, HEAD_DIM, WINDOW),
                              lambda i: (layer, i, 0, 0, 0))
    return _layer_call(
        functools.partial(_mix_sample_kernel, layer=layer), layer, prev,
        grid=(nt // n,),
        in_specs=[pl.BlockSpec((n, D_MODEL), lambda i: (i, 0))] + _sample_mod_specs(layer, sb, 0)
        + [cache_spec, cache_spec] + w_specs_a + [_const_spec((3, n, LANES))] + w_specs_b,
        args=[x, mod3, mod3, mod3, cache_k, cache_v] + w_args_a + [rope] + w_args_b,
        out_specs=[pl.BlockSpec((n, D_MODEL), lambda i: (i, 0)),
                   tok_spec(KV_WIDTH), tok_spec(KV_WIDTH), tok_spec(GMLP_WIDTH)],
        out_shapes=[jax.ShapeDtypeStruct((nt, D_MODEL), F32),
                    tok_shape(KV_WIDTH), tok_shape(KV_WIDTH), tok_shape(GMLP_WIDTH)],
        scratch=[
            pltpu.VMEM((2, D_MODEL // LANES, n, LANES), F32),
            pltpu.VMEM((sb, N_HEADS * ts, LANES), F32),
            pltpu.VMEM((sb, N_HEADS * ts, WINDOW), F32),
            pltpu.VMEM((n, D_MODEL), BF16),
        ],
        name="mix_sample")


def _shifted_rows(hu, before):
    rows, c = hu.shape
    hu3 = hu.reshape(rows // SUBLANES, SUBLANES, c)
    above = jnp.concatenate([before[None], hu3[:-1]], axis=0)
    sub = lax.broadcasted_iota(jnp.int32, (1, SUBLANES, c), 1)
    p1 = pltpu.roll(jnp.where(sub >= SUBLANES - 1, above, hu3), 1, axis=1)
    p2 = pltpu.roll(jnp.where(sub >= SUBLANES - 2, above, hu3), 2, axis=1)
    return p1.reshape(rows, c), p2.reshape(rows, c)


def _conv(hu, prev1, prev2, cw, cb):
    return cb + cw[0:1] * prev2 + cw[1:2] * prev1 + cw[2:3] * hu


def _ffn_chunks():
    return [(slice(j, j + FFN_CHUNK), slice(D_FF + j, D_FF + j + FFN_CHUNK))
            for j in range(0, D_FF, FFN_CHUNK)]


def _down_plan(n_chunks):
    bounds = list(range(0, n_chunks, DOWN_GROUP)) + [n_chunks]
    groups = [(a * FFN_CHUNK, b * FFN_CHUNK) for a, b in zip(bounds[:-1], bounds[1:])]
    return {b // FFN_CHUNK: (a, b) for a, b in groups[:-1]}, groups[-1]


def _down_into(acc_ref, act_scr, wdn_ref, a, b):
    contrib = _dot(act_scr[:, a:b], wdn_ref[a:b, :])
    if a == 0:
        acc_ref[...] = contrib
    else:
        acc_ref[...] += contrib


def _ffn_prompt_kernel(x_ref, mod_ref, gffn_ref, wup_ref, cw_ref, cb_ref, wdn_ref,
                       y_ref, cout_ref, carry, act_scr, acc_ref):
    tt = x_ref.shape[1]
    t = pl.program_id(1)

    @pl.when(t == 0)
    def _():
        carry[...] = jnp.zeros(carry.shape, F32)

    x = x_ref[0]
    seq = pl.program_id(0)
    chunks = _ffn_chunks()
    after, tail = _down_plan(len(chunks))
    hbs, firsts = [], []
    for rows in (slice(0, tt // 2), slice(tt // 2, tt)):
        hbs.append(_modulated_rmsnorm(x[rows], gffn_ref[...], _mod_row(mod_ref, seq, 4),
                                      _mod_row(mod_ref, seq, 3)).astype(BF16))
        firsts.append([_dot(hbs[-1], wup_ref[:, cols]) for cols in chunks[0]])
    hb = jnp.concatenate(hbs, axis=0)
    up = lambda halves: [_dot(hb, wup_ref[:, cols]) for cols in halves]
    ahead = [jnp.concatenate(pieces, axis=0) for pieces in zip(*firsts)]
    for j, halves in enumerate(chunks):
        hus = ahead
        if j + 1 < len(chunks):
            ahead = up(chunks[j + 1])
        if j in after:
            _down_into(acc_ref, act_scr, wdn_ref, *after[j])
        convs = []
        for hu, cols in zip(hus, halves):
            prev1, prev2 = _shifted_rows(hu, carry[:, cols])
            carry[:, cols] = hu[tt - SUBLANES:, :]
            cout_ref[0, :, cols] = hu[tt - (CONV_W - 1):, :]
            convs.append(_conv(hu, prev1, prev2, cw_ref[:, cols], cb_ref[:, cols]))
        act_scr[:, j * FFN_CHUNK:(j + 1) * FFN_CHUNK] = (
            jax.nn.silu(convs[0]) * convs[1]).astype(BF16)
    gate = _mod_row(mod_ref, seq, 5)
    for cols in (slice(0, D_MODEL // 2), slice(D_MODEL // 2, D_MODEL)):
        last = _dot(act_scr[:, tail[0]:tail[1]], wdn_ref[tail[0]:tail[1], cols])
        y_ref[0, :, cols] = x[:, cols] + gate[:, cols] * (acc_ref[:, cols] + last)


def _ffn_weight_specs(layer):
    return [
        _layer_spec(layer, (1, D_MODEL)),
        _single_spec((D_MODEL, 2 * D_FF)),
        _layer_spec(layer, (CONV_W, 2 * D_FF)),
        _layer_spec(layer, (1, 2 * D_FF)),
        _single_spec((D_FF, D_MODEL)),
    ]


def _ffn_weight_args(pp, w_up, w_down):
    return [pp["g_ffn"], w_up, pp["conv_w"], pp["conv_b"], w_down]


def _ffn_prompt_call(layer, x, mod3, n_skip, pp, w_up, w_down, prev):
    b, t, _ = x.shape
    tt = FFN_TILE
    return _layer_call(
        _ffn_prompt_kernel, layer, prev,
        grid=(b, t // tt),
        in_specs=[pl.BlockSpec((1, tt, D_MODEL), lambda i, j: (i, j, 0)),
                  _prompt_mod_spec(layer, n_skip)]
        + _ffn_weight_specs(layer),
        args=[x, mod3] + _ffn_weight_args(pp, w_up, w_down),
        out_specs=[pl.BlockSpec((1, tt, D_MODEL), lambda i, j: (i, j, 0)),
                   pl.BlockSpec((None, 1, CONV_W - 1, 2 * D_FF), lambda i, j: (layer, i, 0, 0))],
        out_shapes=[jax.ShapeDtypeStruct((b, t, D_MODEL), F32),
                    jax.ShapeDtypeStruct((DEPTH, b, CONV_W - 1, 2 * D_FF), F32)],
        scratch=[
            pltpu.VMEM((SUBLANES, 2 * D_FF), F32),
            pltpu.VMEM((tt, D_FF), BF16),
            pltpu.VMEM((tt, D_MODEL), F32),
        ],
        name="ffn_prompt")


def _ffn_sample_kernel(x_ref, sh_ref, sc_ref, gf_ref, hist_ref, gffn_ref, wup_ref, cw_ref,
                       cb_ref, wdn_ref, y_ref, cout_ref, mod_scr, act_scr, acc_ref):
    n = x_ref.shape[0]
    sb = sh_ref.shape[0]
    ts = n // sb
    sub = lax.broadcasted_iota(jnp.int32, (1, ts, FFN_CHUNK), 1)
    _expand_rows(mod_scr.at[0], sh_ref[...], ts)
    _expand_rows(mod_scr.at[1], sc_ref[...], ts)
    x = x_ref[...]
    hb = _modulated_rmsnorm(x, gffn_ref[...], _read_cols(mod_scr.at[1]),
                            _read_cols(mod_scr.at[0])).astype(BF16)
    _expand_rows(mod_scr.at[0], gf_ref[...], ts)
    chunks = _ffn_chunks()
    after, tail = _down_plan(len(chunks))
    up = lambda halves: [_dot(hb, wup_ref[:, cols]) for cols in halves]
    ahead = up(chunks[0])
    for j, halves in enumerate(chunks):
        hus = ahead
        if j + 1 < len(chunks):
            ahead = up(chunks[j + 1])
        if j in after:
            _down_into(acc_ref, act_scr, wdn_ref, *after[j])
        convs = []
        for hu, cols in zip(hus, halves):
            hu3 = hu.reshape(sb, ts, FFN_CHUNK)
            cout_ref[:, :, cols] = hu3[:, ts - (CONV_W - 1):, :]
            above = jnp.concatenate(
                [jnp.zeros((sb, ts - (CONV_W - 1), FFN_CHUNK), F32), hist_ref[:, :, cols]], axis=1)
            p1 = pltpu.roll(jnp.where(sub >= ts - 1, above, hu3), 1, axis=1)
            p2 = pltpu.roll(jnp.where(sub >= ts - 2, above, hu3), 2, axis=1)
            convs.append(_conv(hu, p1.reshape(n, FFN_CHUNK), p2.reshape(n, FFN_CHUNK),
                               cw_ref[:, cols], cb_ref[:, cols]))
        act_scr[:, j * FFN_CHUNK:(j + 1) * FFN_CHUNK] = (
            jax.nn.silu(convs[0]) * convs[1]).astype(BF16)
    _down_into(acc_ref, act_scr, wdn_ref, *tail)
    y_ref[...] = x + _read_cols(mod_scr.at[0]) * acc_ref[...]


def _ffn_sample_call(layer, x, mod3, cconv, pp, w_up, w_down, ts, prev):
    nt = x.shape[0]
    sb = SAMPLE_SEQS
    n = sb * ts
    tok_spec = pl.BlockSpec((n, D_MODEL), lambda i: (i, 0))
    hist_spec = pl.BlockSpec((None, sb, CONV_W - 1, 2 * D_FF), lambda i: (layer, i, 0, 0))
    return _layer_call(
        _ffn_sample_kernel, layer, prev,
        grid=(nt // n,),
        in_specs=[tok_spec] + _sample_mod_specs(layer, sb, 3) + [hist_spec]
        + _ffn_weight_specs(layer),
        args=[x, mod3, mod3, mod3, cconv] + _ffn_weight_args(pp, w_up, w_down),
        out_specs=[tok_spec, hist_spec],
        out_shapes=[jax.ShapeDtypeStruct((nt, D_MODEL), F32),
                    jax.ShapeDtypeStruct(cconv.shape, F32)],
        scratch=[
            pltpu.VMEM((2, D_MODEL // LANES, n, LANES), F32),
            pltpu.VMEM((n, D_FF), BF16),
            pltpu.VMEM((n, D_MODEL), F32),
        ],
        name="ffn_sample")


def _rope_tables(pos):
    half = ROPE_DIMS // 2
    inv = ROPE_THETA ** (-jnp.arange(0, ROPE_DIMS, 2, dtype=F32) / ROPE_DIMS)
    ang = pos.astype(F32)[:, None] * inv[None, :]
    cos, sin = jnp.cos(ang), jnp.sin(ang)
    n = pos.shape[0]
    rest = jnp.zeros((n, HEAD_DIM - ROPE_DIMS), F32)
    zeros = jnp.zeros((n, half), F32)
    cos_t = jnp.concatenate([cos, cos, rest + 1.0], axis=1)
    sin_dn = jnp.concatenate([-sin, zeros, rest], axis=1)
    sin_up = jnp.concatenate([zeros, sin, rest], axis=1)
    return jnp.stack([jnp.tile(z, (1, LANES // HEAD_DIM)) for z in (cos_t, sin_dn, sin_up)])


def _block_diag_ones(width):
    idx = jnp.arange(width) // HEAD_DIM
    return (idx[:, None] == idx[None, :]).astype(BF16)


def _prepare_params(dec_seq, g_attn, w_in, g_q, g_k, sinks, ln_g, ln_b, w_s, b_s, w_out,
                    g_ffn, w_ffn_in, conv_w, conv_b, w_ffn_out):
    causal = jnp.tril(jnp.ones((CHUNK, CHUNK), dtype=bool))
    ws = jnp.where(causal, w_s, 0.0)
    seqs_per_chunk = CHUNK // dec_seq
    eye = jnp.eye(seqs_per_chunk, dtype=F32)
    ws_sample = jnp.einsum("ab,lgts->lgatbs", eye, ws[:, :, :dec_seq, :dec_seq]).reshape(w_s.shape)
    return {
        "g_attn": g_attn[:, None, :], "g_ffn": g_ffn[:, None, :],
        "w_in": w_in, "w_out": w_out,
        "g_q": jnp.tile(g_q, (1, N_HEADS))[:, None, :],
        "g_k": jnp.tile(g_k, (1, N_KV_HEADS))[:, None, :],
        "bd_q": _block_diag_ones(ATTN_WIDTH), "bd_k": _block_diag_ones(KV_WIDTH),
        "sinks": sinks,
        "ln_g": ln_g.reshape(DEPTH, 1, GMLP_WIDTH), "ln_b": ln_b.reshape(DEPTH, 1, GMLP_WIDTH),
        "ws_prompt": ws.astype(BF16), "bs_prompt": jnp.swapaxes(b_s, 1, 2),
        "ws_sample": ws_sample.astype(BF16),
        "bs_sample": jnp.swapaxes(jnp.tile(b_s[:, :, :dec_seq], (1, 1, seqs_per_chunk)), 1, 2),
        "w_up": w_ffn_in, "w_down": w_ffn_out,
        "conv_w": conv_w, "conv_b": conv_b[:, None, :],
    }


def kernel(x_prompt, x_sample, cache_k, cache_v, cache_conv, c_prompt, c_sample, w_ada, b_ada,
           g_attn, w_in, g_q, g_k, sinks, ln_g, ln_b, w_s, b_s, w_out, g_ffn, w_ffn_in, conv_w,
           conv_b, w_ffn_out):
    nbp, seq, _ = x_prompt.shape
    nbs, dec_seq, _ = x_sample.shape
    assert seq % PROMPT_TILE == 0 and PROMPT_TILE % CHUNK == 0 and nbs % SAMPLE_SEQS == 0
    assert seq % FFN_TILE == 0 and FFN_TILE % SUBLANES == 0 and D_FF % FFN_CHUNK == 0
    assert dec_seq == SUBLANES and CHUNK % dec_seq == 0 and cache_k.shape[2] == WINDOW
    assert nbs % SUBLANES == 0 and nbp <= SUBLANES

    n_c = nbs + nbp
    c_all = jnp.concatenate([c_sample, c_prompt, jnp.zeros((-n_c % SUBLANES, D_MODEL), F32)])
    mod3 = _ada_call(c_all, w_ada, b_ada)

    pp = _prepare_params(dec_seq, g_attn, w_in, g_q, g_k, sinks, ln_g, ln_b, w_s, b_s, w_out,
                         g_ffn, w_ffn_in, conv_w, conv_b, w_ffn_out)
    rope_p = _rope_tables(jnp.arange(seq, dtype=jnp.int32))
    rope_s = jnp.tile(_rope_tables(PAST_LEN + jnp.arange(dec_seq, dtype=jnp.int32)),
                      (1, SAMPLE_SEQS, 1))
    ck = jnp.transpose(cache_k, (0, 1, 3, 4, 2))
    cv = jnp.transpose(cache_v, (0, 1, 3, 4, 2))

    xp = x_prompt
    xs = x_sample.reshape(nbs * dec_seq, D_MODEL)
    mix_p = ffn_p = mix_s = ffn_s = None
    for l in range(DEPTH):
        xp, w_in_b, w_out_b, w_up_b, w_dn_b, *mix_p = _mix_prompt_call(
            l, xp, mod3, nbs, pp, rope_p, mix_p)
        xp, *ffn_p = _ffn_prompt_call(l, xp, mod3, nbs, pp, w_up_b, w_dn_b, ffn_p)
        xs, *mix_s = _mix_sample_call(l, xs, mod3, ck, cv, pp, w_in_b, w_out_b, rope_s, dec_seq,
                                      mix_s)
        xs, *ffn_s = _ffn_sample_call(l, xs, mod3, cache_conv, pp, w_up_b, w_dn_b, dec_seq, ffn_s)

    kv_p = (DEPTH, nbp, WINDOW, N_KV_HEADS, HEAD_DIM)
    kv_s = (DEPTH, nbs, dec_seq, N_KV_HEADS, HEAD_DIM)
    return (xp, xs.reshape(nbs, dec_seq, D_MODEL),
            mix_p[0].reshape(kv_p), mix_p[1].reshape(kv_p), ffn_p[0],
            mix_s[0].reshape(kv_s), mix_s[1].reshape(kv_s),
            mix_s[2].reshape(DEPTH, nbs, dec_seq, GMLP_WIDTH),
            ffn_s[0])
```

```python
import functools

import jax
import jax.numpy as jnp
from jax import lax
from jax.experimental import pallas as pl
from jax.experimental.pallas import tpu as pltpu

D_MODEL = 1024
DEPTH = 2
HEAD_DIM = 64
N_HEADS = 8
N_KV_HEADS = 2
KV_WIDTH = N_KV_HEADS * HEAD_DIM
WINDOW = 128
ROPE_THETA = 500000.0
ROPE_DIMS = HEAD_DIM // 4
ATTN_WIDTH = N_HEADS * HEAD_DIM
GMLP_WIDTH = D_MODEL - ATTN_WIDTH
GMLP_GROUPS = 4
GMLP_GW = GMLP_WIDTH // GMLP_GROUPS
CHUNK = 128
IN_WIDTH = ATTN_WIDTH + 2 * KV_WIDTH + 2 * GMLP_WIDTH
D_FF = 2816
CONV_W = 3
N_MOD = 6
EPS = 1e-6
NEG = -1e30
PAST_LEN = 16384

LANES = 128
SUBLANES = 8
N_PAIRS = N_HEADS // 2
VMEM_LIMIT = 56 * 1024 * 1024

PROMPT_TILE = 512
OUT_COLS = 256
MIX_INTERLEAVE = 33
FFN_TILE = 1024
SAMPLE_SEQS = 64
SEQ_UNROLL = 8
FFN_CHUNK = 256
DOWN_GROUP = 2
ADA_TILE = 3072

F32 = jnp.float32
BF16 = jnp.bfloat16


def _dot(a, b):
    return jnp.dot(a, b, preferred_element_type=F32)


def _dot_nt(a, b):
    return lax.dot_general(a, b, (((1,), (1,)), ((), ())), preferred_element_type=F32)


def _const_spec(shape):
    nd = len(shape)
    return pl.BlockSpec(shape, lambda *_: (0,) * nd)


def _layer_spec(layer, tail, single_buffer=False):
    nd = len(tail)
    kw = {"pipeline_mode": pl.Buffered(1)} if single_buffer else {}
    return pl.BlockSpec((None,) + tuple(tail), lambda *_: (layer,) + (0,) * nd, **kw)


def _skip_aliased(body, n_in, n_alias):
    if n_alias == 0:
        return body
    return lambda *refs: body(*refs[:n_in], *refs[n_in + n_alias:])


def _layer_call(body, layer, prev, *, grid, in_specs, args, out_specs, out_shapes, scratch, name,
                n_fresh=1):
    n_in = len(args)
    aliased = [] if prev is None else list(prev)
    return pl.pallas_call(
        _skip_aliased(body, n_in, len(aliased)),
        grid=grid,
        in_specs=list(in_specs) + [pl.BlockSpec(memory_space=pl.ANY)] * len(aliased),
        out_specs=out_specs,
        out_shape=out_shapes,
        input_output_aliases={n_in + i: n_fresh + i for i in range(len(aliased))},
        scratch_shapes=scratch,
        compiler_params=pltpu.CompilerParams(
            dimension_semantics=("arbitrary",) * len(grid), vmem_limit_bytes=VMEM_LIMIT),
        name=name,
    )(*args, *aliased)


def _ada_kernel(c_ref, w_ref, b_ref, o_ref):
    c = c_ref[...]
    a = (c * jax.nn.sigmoid(c)).astype(BF16)
    o_ref[0] = _dot(a, w_ref[0].astype(BF16)) + b_ref[0]


def _ada_call(c_all, w_ada, b_ada):
    n = c_all.shape[0]
    return pl.pallas_call(
        _ada_kernel,
        grid=(DEPTH, N_MOD * D_MODEL // ADA_TILE),
        in_specs=[
            pl.BlockSpec((n, D_MODEL), lambda l, j: (0, 0)),
            pl.BlockSpec((1, D_MODEL, ADA_TILE), lambda l, j: (l, 0, j)),
            pl.BlockSpec((1, 1, ADA_TILE), lambda l, j: (l, 0, j)),
        ],
        out_specs=pl.BlockSpec((1, n, ADA_TILE), lambda l, j: (l, 0, j)),
        out_shape=jax.ShapeDtypeStruct((DEPTH, n, N_MOD * D_MODEL), F32),
        compiler_params=pltpu.CompilerParams(
            dimension_semantics=("arbitrary", "arbitrary"), vmem_limit_bytes=VMEM_LIMIT),
        name="ada_mod",
    )(c_all, w_ada, b_ada.reshape(DEPTH, 1, N_MOD * D_MODEL))


def _modulated_rmsnorm(x, gain, scale, shift):
    r = lax.rsqrt(jnp.mean(x * x, axis=-1, keepdims=True) + EPS)
    return (x * r) * (gain * (1.0 + scale)) + shift


def _head_rmsnorm(z, ones_bd, gain):
    ssq = _dot((z * z).astype(BF16), ones_bd)
    return z * lax.rsqrt(ssq * (1.0 / HEAD_DIM) + EPS) * gain


def _rope(z, cos_t, sin_dn, sin_up):
    half = ROPE_DIMS // 2
    cols = []
    for p in range(z.shape[-1] // LANES):
        zp = z[:, p * LANES:(p + 1) * LANES]
        cols.append(zp * cos_t
                    + pltpu.roll(zp, LANES - half, axis=1) * sin_dn
                    + pltpu.roll(zp, half, axis=1) * sin_up)
    return cols[0] if len(cols) == 1 else jnp.concatenate(cols, axis=1)


def _split_heads(z):
    lane = lax.broadcasted_iota(jnp.int32, z.shape, 1)
    lo = lane < HEAD_DIM
    zs = pltpu.roll(z, HEAD_DIM, axis=1)
    zero = jnp.zeros_like(z)
    return (jnp.where(lo, z, zero), jnp.where(lo, zero, zs),
            jnp.where(lo, zs, zero), jnp.where(lo, zero, z))


def _gelu_layernorm(zv, ln_g, ln_b):
    gv = jax.nn.gelu(zv)
    cols = []
    for g in range(GMLP_GROUPS):
        xg = gv[:, g * GMLP_GW:(g + 1) * GMLP_GW]
        mu = jnp.mean(xg, axis=-1, keepdims=True)
        xc = xg - mu
        var = jnp.mean(xc * xc, axis=-1, keepdims=True)
        cols.append(xc * lax.rsqrt(var + EPS))
    return jnp.concatenate(cols, axis=1) * ln_g + ln_b


def _spatial_gate_into(mix_ref, u, vg, ws_ref, bs_ref):
    vgb = vg.astype(BF16)
    for g in range(GMLP_GROUPS):
        w = ws_ref[g]
        bias = bs_ref[:, g:g + 1]
        cs = slice(g * GMLP_GW, (g + 1) * GMLP_GW)
        for c in range(u.shape[0] // CHUNK):
            rs = slice(c * CHUNK, (c + 1) * CHUNK)
            z = _dot(w, vgb[rs, cs]) + bias
            mix_ref[rs, ATTN_WIDTH + g * GMLP_GW:ATTN_WIDTH + (g + 1) * GMLP_GW] = (
                u[rs, cs] * z).astype(BF16)


def _lane_pair(a, b, shape):
    lane = lax.broadcasted_iota(jnp.int32, shape, 1)
    return jnp.where(lane < HEAD_DIM, a, b)


def _prompt_mod_spec(layer, n_skip):
    return pl.BlockSpec((None, SUBLANES, N_MOD * D_MODEL),
                        lambda *_: (layer, n_skip // SUBLANES, 0))


def _mod_row(mod_ref, seq, m):
    return mod_ref[pl.ds(seq, 1), m * D_MODEL:(m + 1) * D_MODEL]


def _window_t(ref, b):
    return ref[b].reshape(KV_WIDTH, WINDOW).astype(BF16)


def _expand_rows(dst_ref, src, reps):
    for c in range(dst_ref.shape[0]):
        piece = src[:, c * LANES:(c + 1) * LANES]
        for t in range(reps):
            dst_ref[c, pl.ds(t, src.shape[0], stride=reps), :] = piece


def _read_cols(ref, rows=slice(None)):
    return jnp.concatenate([ref[c, rows, :] for c in range(ref.shape[0])], axis=1)


def _interleave(tasks, width):
    pending = iter(tasks)
    active = []
    while True:
        while len(active) < width:
            task = next(pending, None)
            if task is None:
                break
            active.append(task)
        if not active:
            return
        for task in list(active):
            if next(task, "done") == "done":
                active.remove(task)


def _mix_prompt_kernel(xa_ref, xc_ref, mod_ref, gattn_ref, win32_ref, gq_ref, gk_ref,
                       bdq_ref, bdk_ref, rope_ref, sinks_ref, lng_ref, lnb_ref, ws_ref, bs_ref,
                       wout32_ref, wup32_ref, wdn32_ref,
                       y_ref, win_ref, wout_ref, wup_ref, wdn_ref, kout_ref, vout_ref,
                       q_ring, k_ring, v_ring, u_ring, vg_ring, mix_ring,
                       *, layer, tiles_per_seq, n_tiles):
    tt = xa_ref.shape[1]
    g = pl.program_id(0)

    @pl.when(g == 0)
    def _():
        win_ref[...] = win32_ref[...].astype(BF16)
        wout_ref[...] = wout32_ref[...].astype(BF16)
        for ring in (q_ring, k_ring, v_ring, u_ring, vg_ring, mix_ring):
            ring[...] = jnp.zeros(ring.shape, ring.dtype)

    wup_ref[...] = wup32_ref[...].astype(BF16)
    wdn_ref[...] = wdn32_ref[...].astype(BF16)

    cur, prv = g % 2, (g + 1) % 2
    seq_a = jnp.minimum(g, n_tiles - 1) // tiles_per_seq
    seq_c = jnp.clip(g - 2, 0, n_tiles - 1) // tiles_per_seq
    kv_a, kv_b, kv_p = g % 3, (g + 2) % 3, (g + 1) % 3

    cos_t, sin_dn, sin_up = rope_ref[0], rope_ref[1], rope_ref[2]
    env = {}

    def norm_task():
        env["hb"] = _modulated_rmsnorm(
            xa_ref[0], gattn_ref[...], _mod_row(mod_ref, seq_a, 1),
            _mod_row(mod_ref, seq_a, 0)).astype(BF16)
        yield

    def q_task():
        q = _dot(env["hb"], win_ref[:, :ATTN_WIDTH])
        yield
        ssq = _dot((q * q).astype(BF16), bdq_ref[...])
        yield
        q = q * lax.rsqrt(ssq * (1.0 / HEAD_DIM) + EPS) * gq_ref[...]
        q_ring[cur] = (_rope(q, cos_t, sin_dn, sin_up) * (HEAD_DIM ** -0.5)).astype(BF16)

    def kv_task():
        kv = _dot(env["hb"], win_ref[:, ATTN_WIDTH:ATTN_WIDTH + 2 * KV_WIDTH])
        yield
        k, v = kv[:, :KV_WIDTH], kv[:, KV_WIDTH:]
        ssq = _dot((k * k).astype(BF16), bdk_ref[...])
        vout_ref[0] = v[tt - WINDOW:, :]
        for i, vz in enumerate(_split_heads(v)):
            v_ring[kv_a, i] = vz.astype(BF16)
        yield
        k = _rope(k * lax.rsqrt(ssq * (1.0 / HEAD_DIM) + EPS) * gk_ref[...], cos_t, sin_dn, sin_up)
        kout_ref[0] = k[tt - WINDOW:, :]
        for i, kz in enumerate(_split_heads(k)):
            k_ring[kv_a, i] = kz.astype(BF16)

    def u_task():
        zu = _dot(env["hb"], win_ref[:, ATTN_WIDTH + 2 * KV_WIDTH:IN_WIDTH - GMLP_WIDTH])
        yield
        u_ring[cur] = jax.nn.gelu(zu)

    def vg_task():
        zv = _dot(env["hb"], win_ref[:, IN_WIDTH - GMLP_WIDTH:])
        yield
        vg_ring[cur] = _gelu_layernorm(zv, lng_ref[...], lnb_ref[...]).astype(BF16)

    row = lax.broadcasted_iota(jnp.int32, (WINDOW, 2 * WINDOW), 0)
    col = lax.broadcasted_iota(jnp.int32, (WINDOW, 2 * WINDOW), 1)
    band = (col - row >= 1) & (col - row <= WINDOW)
    seq_start = (g + tiles_per_seq - 1) % tiles_per_seq == 0
    band_first = band & (col >= jnp.where(seq_start, WINDOW, 0))
    tail = slice(tt - WINDOW, tt)

    def band_rows(ring, idx, i):
        if i == 0:
            return [ring[kv_p, idx, tail, :], ring[kv_b, idx, 0:WINDOW, :]]
        return [ring[kv_b, idx, (i - 1) * WINDOW:(i + 1) * WINDOW, :]]

    def attn_task(i, kv):
        mask = band_first if i == 0 else band
        rows = slice(i * WINDOW, (i + 1) * WINDOW)
        pairs = (2 * kv, 2 * kv + 1)
        kblk = jnp.concatenate(band_rows(k_ring, 2 * kv, i) + band_rows(k_ring, 2 * kv + 1, i), axis=0)
        q2 = jnp.concatenate([q_ring[prv, rows, p * LANES:(p + 1) * LANES] for p in pairs], axis=0)
        s = _dot_nt(q2, kblk)
        yield
        es, scales = [], []
        for j, p in enumerate(pairs):
            e_pair, invs = [], []
            for hh in range(2):
                sink = sinks_ref[layer, 2 * p + hh]
                sh_ = jnp.where(mask, s[j * WINDOW:(j + 1) * WINDOW,
                                        hh * 2 * WINDOW:(hh + 1) * 2 * WINDOW], NEG)
                m = jnp.maximum(jnp.max(sh_, axis=-1, keepdims=True), sink)
                e = jnp.exp(sh_ - m)
                e_pair.append(e.astype(BF16))
                invs.append(1.0 / (jnp.sum(e, axis=-1, keepdims=True) + jnp.exp(sink - m)))
            es.append(jnp.concatenate(e_pair, axis=1))
            scales.append(_lane_pair(invs[0], invs[1], (WINDOW, LANES)))
        vblk = jnp.concatenate(band_rows(v_ring, 2 * kv, i) + band_rows(v_ring, 2 * kv + 1, i), axis=0)
        o = _dot(jnp.concatenate(es, axis=0), vblk)
        yield
        for j, p in enumerate(pairs):
            mix_ring[cur, rows, p * LANES:(p + 1) * LANES] = (
                o[j * WINDOW:(j + 1) * WINDOW] * scales[j]).astype(BF16)

    def gate_task(c, grp):
        rs = slice(c * CHUNK, (c + 1) * CHUNK)
        cs = slice(grp * GMLP_GW, (grp + 1) * GMLP_GW)
        z = _dot(ws_ref[grp], vg_ring[prv, rs, cs])
        yield
        z = z + bs_ref[:, grp:grp + 1]
        mix_ring[cur, rs, ATTN_WIDTH + grp * GMLP_GW:ATTN_WIDTH + (grp + 1) * GMLP_GW] = (
            u_ring[prv, rs, cs] * z).astype(BF16)

    def out_task(c):
        cols = slice(c * OUT_COLS, (c + 1) * OUT_COLS)
        r = _dot(mix_ring[prv], wout_ref[:, cols])
        yield
        y_ref[0, :, cols] = xc_ref[0, :, cols] + _mod_row(mod_ref, seq_c, 2)[:, cols] * r

    n_blocks = tt // WINDOW
    attn = [attn_task(i, kv) for i in range(n_blocks) for kv in range(N_KV_HEADS)]
    gate = [gate_task(c, grp) for c in range(n_blocks) for grp in range(GMLP_GROUPS)]
    big = [out_task(0), norm_task(), vg_task(), out_task(1), u_task(), q_task(), out_task(2),
           kv_task(), out_task(3)]
    small = [t for trio in zip(attn, gate[0::2], gate[1::2]) for t in trio]
    per_big = -(-len(small) // len(big))
    order = []
    for b in big:
        order.append(b)
        order.extend(small[:per_big])
        small = small[per_big:]
    _interleave(order + small, MIX_INTERLEAVE)


def _mix_weight_specs(layer, w_in_spec, w_out_spec):
    return [
        _layer_spec(layer, (1, D_MODEL)),
        w_in_spec,
        _layer_spec(layer, (1, ATTN_WIDTH)),
        _layer_spec(layer, (1, KV_WIDTH)),
        _const_spec((ATTN_WIDTH, ATTN_WIDTH)),
        _const_spec((KV_WIDTH, KV_WIDTH)),
    ], [
        pl.BlockSpec(memory_space=pltpu.SMEM),
        _layer_spec(layer, (1, GMLP_WIDTH)),
        _layer_spec(layer, (1, GMLP_WIDTH)),
        _layer_spec(layer, (GMLP_GROUPS, CHUNK, CHUNK)),
        _layer_spec(layer, (CHUNK, GMLP_GROUPS)),
        w_out_spec,
    ]


def _mix_weight_args(pp, kind, w_in, w_out):
    return ([pp["g_attn"], w_in, pp["g_q"], pp["g_k"], pp["bd_q"], pp["bd_k"]],
            [pp["sinks"], pp["ln_g"], pp["ln_b"], pp["ws_" + kind], pp["bs_" + kind], w_out])


def _mix_prompt_call(layer, x, mod3, n_skip, pp, rope, prev):
    b, t, _ = x.shape
    tt = PROMPT_TILE
    nt = t // tt
    last = b * nt - 1
    w_specs_a, w_specs_b = _mix_weight_specs(
        layer, _layer_spec(layer, (D_MODEL, IN_WIDTH), single_buffer=True),
        _layer_spec(layer, (D_MODEL, D_MODEL), single_buffer=True))
    w_args_a, w_args_b = _mix_weight_args(pp, "prompt", pp["w_in"], pp["w_out"])
    up_rows, dn_rows = D_MODEL // (b * nt), D_FF // (b * nt // 2)
    assert up_rows % 16 == 0 and dn_rows % 16 == 0

    def tile(g, lag):
        return jnp.clip(g - lag, 0, last)

    def slab_specs(rows, n_slabs, width, src_layer):
        idx = lambda g: jnp.minimum(g, n_slabs - 1)
        return (pl.BlockSpec((None, rows, width), lambda g: (src_layer, idx(g), 0)),
                pl.BlockSpec((rows, width), lambda g: (idx(g), 0)))

    up_in, up_out = slab_specs(up_rows, D_MODEL // up_rows, 2 * D_FF, layer)
    dn_in, dn_out = slab_specs(dn_rows, D_FF // dn_rows, D_MODEL, layer)

    def x_spec(lag):
        return pl.BlockSpec((1, tt, D_MODEL), lambda g: (tile(g, lag) // nt, tile(g, lag) % nt, 0))

    kv_spec = pl.BlockSpec((None, 1, WINDOW, KV_WIDTH), lambda g: (layer, tile(g, 0) // nt, 0, 0))
    kv_shape = jax.ShapeDtypeStruct((DEPTH, b, WINDOW, KV_WIDTH), F32)
    return _layer_call(
        functools.partial(_mix_prompt_kernel, layer=layer, tiles_per_seq=nt, n_tiles=b * nt),
        layer, prev,
        grid=(b * nt + 2,),
        in_specs=[x_spec(0), x_spec(2), _prompt_mod_spec(layer, n_skip)] + w_specs_a
        + [pl.BlockSpec((3, tt, LANES), lambda g: (0, tile(g, 0) % nt, 0))] + w_specs_b
        + [up_in, dn_in],
        args=[x, x, mod3] + w_args_a + [rope] + w_args_b + [pp["w_up"], pp["w_down"]],
        out_specs=[x_spec(2), _const_spec((D_MODEL, IN_WIDTH)), _const_spec((D_MODEL, D_MODEL)),
                   up_out, dn_out, kv_spec, kv_spec],
        out_shapes=[jax.ShapeDtypeStruct((b, t, D_MODEL), F32),
                    jax.ShapeDtypeStruct((D_MODEL, IN_WIDTH), BF16),
                    jax.ShapeDtypeStruct((D_MODEL, D_MODEL), BF16),
                    jax.ShapeDtypeStruct((D_MODEL, 2 * D_FF), BF16),
                    jax.ShapeDtypeStruct((D_FF, D_MODEL), BF16), kv_shape, kv_shape],
        n_fresh=5,
        scratch=[
            pltpu.VMEM((2, tt, ATTN_WIDTH), BF16),
            pltpu.VMEM((3, 4, tt, LANES), BF16),
            pltpu.VMEM((3, 4, tt, LANES), BF16),
            pltpu.VMEM((2, tt, GMLP_WIDTH), F32),
            pltpu.VMEM((2, tt, GMLP_WIDTH), BF16),
            pltpu.VMEM((2, tt, D_MODEL), BF16),
        ],
        name="mix_prompt")


def _mix_sample_kernel(x_ref, sh_ref, sc_ref, ga_ref, ck_ref, cv_ref, gattn_ref, win_ref, gq_ref,
                       gk_ref, bdq_ref, bdk_ref, rope_ref, sinks_ref, lng_ref, lnb_ref, ws_ref,
                       bs_ref, wout_ref, y_ref, kout_ref, vout_ref, vgout_ref,
                       mod_scr, q_scr, s_scr, mix_ref, *, layer):
    n = x_ref.shape[0]
    sb = sh_ref.shape[0]
    ts = n // sb
    _expand_rows(mod_scr.at[0], sh_ref[...], ts)
    _expand_rows(mod_scr.at[1], sc_ref[...], ts)
    x = x_ref[...]
    h = _modulated_rmsnorm(x, gattn_ref[...], _read_cols(mod_scr.at[1]), _read_cols(mod_scr.at[0]))
    proj = _dot(h.astype(BF16), win_ref[...])
    _expand_rows(mod_scr.at[0], ga_ref[...], ts)

    cos_t, sin_dn, sin_up = rope_ref[0], rope_ref[1], rope_ref[2]
    q = _rope(_head_rmsnorm(proj[:, :ATTN_WIDTH], bdq_ref[...], gq_ref[...]), cos_t, sin_dn, sin_up)
    q = q * (HEAD_DIM ** -0.5)
    k = _rope(_head_rmsnorm(proj[:, ATTN_WIDTH:ATTN_WIDTH + KV_WIDTH], bdk_ref[...], gk_ref[...]),
              cos_t, sin_dn, sin_up)
    v = proj[:, ATTN_WIDTH + KV_WIDTH:ATTN_WIDTH + 2 * KV_WIDTH]
    kout_ref[...] = k
    vout_ref[...] = v

    lane = lax.broadcasted_iota(jnp.int32, (n, LANES), 1)
    group = N_HEADS // N_KV_HEADS
    qh = []
    for hd in range(N_HEADS):
        piece = q[:, (hd // 2) * LANES:(hd // 2 + 1) * LANES]
        if hd % 2 != hd // group:
            piece = pltpu.roll(piece, HEAD_DIM, axis=1)
        keep = (lane < HEAD_DIM) if hd // group == 0 else (lane >= HEAD_DIM)
        piece = jnp.where(keep, piece, 0.0)
        q_scr[:, hd * ts:(hd + 1) * ts, :] = piece.reshape(sb, ts, LANES)
        qh.append(piece.astype(BF16))

    def cache_scores(b, carry):
        s_scr[b] = _dot(q_scr[b].astype(BF16), _window_t(ck_ref, b))
        return carry

    lax.fori_loop(0, sb, cache_scores, 0, unroll=SEQ_UNROLL)

    kb, vb = k.astype(BF16), v.astype(BF16)
    grp = CHUNK
    rown = lax.broadcasted_iota(jnp.int32, (grp, grp), 0)
    coln = lax.broadcasted_iota(jnp.int32, (grp, grp), 1)
    seq_bits = ts.bit_length() - 1
    mask_new = ((rown >> seq_bits) == (coln >> seq_bits)) & (coln <= rown)
    mask_cache = (lax.broadcasted_iota(jnp.int32, (grp, WINDOW), 1)
                  > (lax.broadcasted_iota(jnp.int32, (grp, WINDOW), 0) & (ts - 1)))

    for g0 in range(0, n, grp):
        rows = slice(g0, g0 + grp)
        seqs = slice(g0 // ts, (g0 + grp) // ts)
        s_new = _dot_nt(jnp.concatenate([z[rows] for z in qh], axis=0), kb[rows])
        e_new = []
        for hd in range(N_HEADS):
            sink = sinks_ref[layer, hd]
            hrows = slice(hd * ts, (hd + 1) * ts)
            sn = jnp.where(mask_new, s_new[hd * grp:(hd + 1) * grp], NEG)
            sc_ = jnp.where(mask_cache, s_scr[seqs, hrows, :].reshape(grp, WINDOW), NEG)
            m = jnp.maximum(jnp.maximum(jnp.max(sn, axis=-1, keepdims=True),
                                        jnp.max(sc_, axis=-1, keepdims=True)), sink)
            en = jnp.exp(sn - m)
            ec = jnp.exp(sc_ - m)
            inv = 1.0 / (jnp.sum(en, axis=-1, keepdims=True) + jnp.sum(ec, axis=-1, keepdims=True)
                         + jnp.exp(sink - m))
            e_new.append((en * inv).astype(BF16))
            s_scr[seqs, hrows, :] = (ec * inv).reshape(grp // ts, ts, WINDOW)
        o_new = _dot(jnp.concatenate(e_new, axis=0), vb[rows])
        for hd in range(N_HEADS):
            q_scr[seqs, hd * ts:(hd + 1) * ts, :] = o_new[hd * grp:(hd + 1) * grp].reshape(
                grp // ts, ts, LANES)

    def cache_pv(b, carry):
        q_scr[b] += _dot_nt(s_scr[b].astype(BF16), _window_t(cv_ref, b))
        return carry

    lax.fori_loop(0, sb, cache_pv, 0, unroll=SEQ_UNROLL)

    for p in range(N_PAIRS):
        halves = []
        for hd in (2 * p, 2 * p + 1):
            o = q_scr[:, hd * ts:(hd + 1) * ts, :].reshape(n, LANES)
            if hd % 2 != hd // group:
                o = pltpu.roll(o, HEAD_DIM, axis=1)
            halves.append(o)
        mix_ref[:, p * LANES:(p + 1) * LANES] = jnp.where(
            lane < HEAD_DIM, halves[0], halves[1]).astype(BF16)

    u = jax.nn.gelu(proj[:, ATTN_WIDTH + 2 * KV_WIDTH:ATTN_WIDTH + 2 * KV_WIDTH + GMLP_WIDTH])
    vg = _gelu_layernorm(proj[:, IN_WIDTH - GMLP_WIDTH:], lng_ref[...], lnb_ref[...])
    vgout_ref[...] = vg
    _spatial_gate_into(mix_ref, u, vg, ws_ref, bs_ref)

    y_ref[...] = x + _read_cols(mod_scr.at[0]) * _dot(mix_ref[...], wout_ref[...])


def _sample_mod_specs(layer, sb, first):
    return [pl.BlockSpec((None, sb, D_MODEL), functools.partial(lambda i, w: (layer, i, w), w=first + m))
            for m in range(3)]


def _single_spec(shape):
    nd = len(shape)
    return pl.BlockSpec(shape, lambda *_: (0,) * nd, pipeline_mode=pl.Buffered(1))


def _mix_sample_call(layer, x, mod3, cache_k, cache_v, pp, w_in, w_out, rope, ts, prev):
    nt = x.shape[0]
    sb = SAMPLE_SEQS
    n = sb * ts
    w_specs_a, w_specs_b = _mix_weight_specs(
        layer, _single_spec((D_MODEL, IN_WIDTH)), _single_spec((D_MODEL, D_MODEL)))
    w_args_a, w_args_b = _mix_weight_args(pp, "sample", w_in, w_out)
    tok_spec = lambda w: pl.BlockSpec((None, n, w), lambda i: (layer, i, 0))
    tok_shape = lambda w: jax.ShapeDtypeStruct((DEPTH, nt, w), F32)
    cache_spec = pl.BlockSpec((None, sb, N_KV_HEADS, HEAD_DIM, WINDOW),
                              lambda i: (layer, i, 0, 0, 0))
    return _layer_call(
        functools.partial(_mix_sample_kernel, layer=layer), layer, prev,
        grid=(nt // n,),
        in_specs=[pl.BlockSpec((n, D_MODEL), lambda i: (i, 0))] + _sample_mod_specs(layer, sb, 0)
        + [cache_spec, cache_spec] + w_specs_a + [_const_spec((3, n, LANES))] + w_specs_b,
        args=[x, mod3, mod3, mod3, cache_k, cache_v] + w_args_a + [rope] + w_args_b,
        out_specs=[pl.BlockSpec((n, D_MODEL), lambda i: (i, 0)),
                   tok_spec(KV_WIDTH), tok_spec(KV_WIDTH), tok_spec(GMLP_WIDTH)],
        out_shapes=[jax.ShapeDtypeStruct((nt, D_MODEL), F32),
                    tok_shape(KV_WIDTH), tok_shape(KV_WIDTH), tok_shape(GMLP_WIDTH)],
        scratch=[
            pltpu.VMEM((2, D_MODEL // LANES, n, LANES), F32),
            pltpu.VMEM((sb, N_HEADS * ts, LANES), F32),
            pltpu.VMEM((sb, N_HEADS * ts, WINDOW), F32),
            pltpu.VMEM((n, D_MODEL), BF16),
        ],
        name="mix_sample")


def _shifted_rows(hu, before):
    rows, c = hu.shape
    hu3 = hu.reshape(rows // SUBLANES, SUBLANES, c)
    above = jnp.concatenate([before[None], hu3[:-1]], axis=0)
    sub = lax.broadcasted_iota(jnp.int32, (1, SUBLANES, c), 1)
    p1 = pltpu.roll(jnp.where(sub >= SUBLANES - 1, above, hu3), 1, axis=1)
    p2 = pltpu.roll(jnp.where(sub >= SUBLANES - 2, above, hu3), 2, axis=1)
    return p1.reshape(rows, c), p2.reshape(rows, c)


def _conv(hu, prev1, prev2, cw, cb):
    return cb + cw[0:1] * prev2 + cw[1:2] * prev1 + cw[2:3] * hu


def _ffn_chunks():
    return [(slice(j, j + FFN_CHUNK), slice(D_FF + j, D_FF + j + FFN_CHUNK))
            for j in range(0, D_FF, FFN_CHUNK)]


def _down_plan(n_chunks):
    bounds = list(range(0, n_chunks, DOWN_GROUP)) + [n_chunks]
    groups = [(a * FFN_CHUNK, b * FFN_CHUNK) for a, b in zip(bounds[:-1], bounds[1:])]
    return {b // FFN_CHUNK: (a, b) for a, b in groups[:-1]}, groups[-1]


def _down_into(acc_ref, act_scr, wdn_ref, a, b):
    contrib = _dot(act_scr[:, a:b], wdn_ref[a:b, :])
    if a == 0:
        acc_ref[...] = contrib
    else:
        acc_ref[...] += contrib


def _ffn_prompt_kernel(x_ref, mod_ref, gffn_ref, wup_ref, cw_ref, cb_ref, wdn_ref,
                       y_ref, cout_ref, carry, act_scr, acc_ref):
    tt = x_ref.shape[1]
    t = pl.program_id(1)

    @pl.when(t == 0)
    def _():
        carry[...] = jnp.zeros(carry.shape, F32)

    x = x_ref[0]
    seq = pl.program_id(0)
    chunks = _ffn_chunks()
    after, tail = _down_plan(len(chunks))
    hbs, firsts = [], []
    for rows in (slice(0, tt // 2), slice(tt // 2, tt)):
        hbs.append(_modulated_rmsnorm(x[rows], gffn_ref[...], _mod_row(mod_ref, seq, 4),
                                      _mod_row(mod_ref, seq, 3)).astype(BF16))
        firsts.append([_dot(hbs[-1], wup_ref[:, cols]) for cols in chunks[0]])
    hb = jnp.concatenate(hbs, axis=0)
    up = lambda halves: [_dot(hb, wup_ref[:, cols]) for cols in halves]
    ahead = [jnp.concatenate(pieces, axis=0) for pieces in zip(*firsts)]
    for j, halves in enumerate(chunks):
        hus = ahead
        if j + 1 < len(chunks):
            ahead = up(chunks[j + 1])
        if j in after:
            _down_into(acc_ref, act_scr, wdn_ref, *after[j])
        convs = []
        for hu, cols in zip(hus, halves):
            prev1, prev2 = _shifted_rows(hu, carry[:, cols])
            carry[:, cols] = hu[tt - SUBLANES:, :]
            cout_ref[0, :, cols] = hu[tt - (CONV_W - 1):, :]
            convs.append(_conv(hu, prev1, prev2, cw_ref[:, cols], cb_ref[:, cols]))
        act_scr[:, j * FFN_CHUNK:(j + 1) * FFN_CHUNK] = (
            jax.nn.silu(convs[0]) * convs[1]).astype(BF16)
    gate = _mod_row(mod_ref, seq, 5)
    for cols in (slice(0, D_MODEL // 2), slice(D_MODEL // 2, D_MODEL)):
        last = _dot(act_scr[:, tail[0]:tail[1]], wdn_ref[tail[0]:tail[1], cols])
        y_ref[0, :, cols] = x[:, cols] + gate[:, cols] * (acc_ref[:, cols] + last)


def _ffn_weight_specs(layer):
    return [
        _layer_spec(layer, (1, D_MODEL)),
        _single_spec((D_MODEL, 2 * D_FF)),
        _layer_spec(layer, (CONV_W, 2 * D_FF)),
        _layer_spec(layer, (1, 2 * D_FF)),
        _single_spec((D_FF, D_MODEL)),
    ]


def _ffn_weight_args(pp, w_up, w_down):
    return [pp["g_ffn"], w_up, pp["conv_w"], pp["conv_b"], w_down]


def _ffn_prompt_call(layer, x, mod3, n_skip, pp, w_up, w_down, prev):
    b, t, _ = x.shape
    tt = FFN_TILE
    return _layer_call(
        _ffn_prompt_kernel, layer, prev,
        grid=(b, t // tt),
        in_specs=[pl.BlockSpec((1, tt, D_MODEL), lambda i, j: (i, j, 0)),
                  _prompt_mod_spec(layer, n_skip)]
        + _ffn_weight_specs(layer),
        args=[x, mod3] + _ffn_weight_args(pp, w_up, w_down),
        out_specs=[pl.BlockSpec((1, tt, D_MODEL), lambda i, j: (i, j, 0)),
                   pl.BlockSpec((None, 1, CONV_W - 1, 2 * D_FF), lambda i, j: (layer, i, 0, 0))],
        out_shapes=[jax.ShapeDtypeStruct((b, t, D_MODEL), F32),
                    jax.ShapeDtypeStruct((DEPTH, b, CONV_W - 1, 2 * D_FF), F32)],
        scratch=[
            pltpu.VMEM((SUBLANES, 2 * D_FF), F32),
            pltpu.VMEM((tt, D_FF), BF16),
            pltpu.VMEM((tt, D_MODEL), F32),
        ],
        name="ffn_prompt")


def _ffn_sample_kernel(x_ref, sh_ref, sc_ref, gf_ref, hist_ref, gffn_ref, wup_ref, cw_ref,
                       cb_ref, wdn_ref, y_ref, cout_ref, mod_scr, act_scr, acc_ref):
    n = x_ref.shape[0]
    sb = sh_ref.shape[0]
    ts = n // sb
    sub = lax.broadcasted_iota(jnp.int32, (1, ts, FFN_CHUNK), 1)
    _expand_rows(mod_scr.at[0], sh_ref[...], ts)
    _expand_rows(mod_scr.at[1], sc_ref[...], ts)
    x = x_ref[...]
    hb = _modulated_rmsnorm(x, gffn_ref[...], _read_cols(mod_scr.at[1]),
                            _read_cols(mod_scr.at[0])).astype(BF16)
    _expand_rows(mod_scr.at[0], gf_ref[...], ts)
    chunks = _ffn_chunks()
    after, tail = _down_plan(len(chunks))
    up = lambda halves: [_dot(hb, wup_ref[:, cols]) for cols in halves]
    ahead = up(chunks[0])
    for j, halves in enumerate(chunks):
        hus = ahead
        if j + 1 < len(chunks):
            ahead = up(chunks[j + 1])
        if j in after:
            _down_into(acc_ref, act_scr, wdn_ref, *after[j])
        convs = []
        for hu, cols in zip(hus, halves):
            hu3 = hu.reshape(sb, ts, FFN_CHUNK)
            cout_ref[:, :, cols] = hu3[:, ts - (CONV_W - 1):, :]
            above = jnp.concatenate(
                [jnp.zeros((sb, ts - (CONV_W - 1), FFN_CHUNK), F32), hist_ref[:, :, cols]], axis=1)
            p1 = pltpu.roll(jnp.where(sub >= ts - 1, above, hu3), 1, axis=1)
            p2 = pltpu.roll(jnp.where(sub >= ts - 2, above, hu3), 2, axis=1)
            convs.append(_conv(hu, p1.reshape(n, FFN_CHUNK), p2.reshape(n, FFN_CHUNK),
                               cw_ref[:, cols], cb_ref[:, cols]))
        act_scr[:, j * FFN_CHUNK:(j + 1) * FFN_CHUNK] = (
            jax.nn.silu(convs[0]) * convs[1]).astype(BF16)
    _down_into(acc_ref, act_scr, wdn_ref, *tail)
    y_ref[...] = x + _read_cols(mod_scr.at[0]) * acc_ref[...]


def _ffn_sample_call(layer, x, mod3, cconv, pp, w_up, w_down, ts, prev):
    nt = x.shape[0]
    sb = SAMPLE_SEQS
    n = sb * ts
    tok_spec = pl.BlockSpec((n, D_MODEL), lambda i: (i, 0))
    hist_spec = pl.BlockSpec((None, sb, CONV_W - 1, 2 * D_FF), lambda i: (layer, i, 0, 0))
    return _layer_call(
        _ffn_sample_kernel, layer, prev,
        grid=(nt // n,),
        in_specs=[tok_spec] + _sample_mod_specs(layer, sb, 3) + [hist_spec]
        + _ffn_weight_specs(layer),
        args=[x, mod3, mod3, mod3, cconv] + _ffn_weight_args(pp, w_up, w_down),
        out_specs=[tok_spec, hist_spec],
        out_shapes=[jax.ShapeDtypeStruct((nt, D_MODEL), F32),
                    jax.ShapeDtypeStruct(cconv.shape, F32)],
        scratch=[
            pltpu.VMEM((2, D_MODEL // LANES, n, LANES), F32),
            pltpu.VMEM((n, D_FF), BF16),
            pltpu.VMEM((n, D_MODEL), F32),
        ],
        name="ffn_sample")


def _rope_tables(pos):
    half = ROPE_DIMS // 2
    inv = ROPE_THETA ** (-jnp.arange(0, ROPE_DIMS, 2, dtype=F32) / ROPE_DIMS)
    ang = pos.astype(F32)[:, None] * inv[None, :]
    cos, sin = jnp.cos(ang), jnp.sin(ang)
    n = pos.shape[0]
    rest = jnp.zeros((n, HEAD_DIM - ROPE_DIMS), F32)
    zeros = jnp.zeros((n, half), F32)
    cos_t = jnp.concatenate([cos, cos, rest + 1.0], axis=1)
    sin_dn = jnp.concatenate([-sin, zeros, rest], axis=1)
    sin_up = jnp.concatenate([zeros, sin, rest], axis=1)
    return jnp.stack([jnp.tile(z, (1, LANES // HEAD_DIM)) for z in (cos_t, sin_dn, sin_up)])


def _block_diag_ones(width):
    idx = jnp.arange(width) // HEAD_DIM
    return (idx[:, None] == idx[None, :]).astype(BF16)


def _prepare_params(dec_seq, g_attn, w_in, g_q, g_k, sinks, ln_g, ln_b, w_s, b_s, w_out,
                    g_ffn, w_ffn_in, conv_w, conv_b, w_ffn_out):
    causal = jnp.tril(jnp.ones((CHUNK, CHUNK), dtype=bool))
    ws = jnp.where(causal, w_s, 0.0)
    seqs_per_chunk = CHUNK // dec_seq
    eye = jnp.eye(seqs_per_chunk, dtype=F32)
    ws_sample = jnp.einsum("ab,lgts->lgatbs", eye, ws[:, :, :dec_seq, :dec_seq]).reshape(w_s.shape)
    return {
        "g_attn": g_attn[:, None, :], "g_ffn": g_ffn[:, None, :],
        "w_in": w_in, "w_out": w_out,
        "g_q": jnp.tile(g_q, (1, N_HEADS))[:, None, :],
        "g_k": jnp.tile(g_k, (1, N_KV_HEADS))[:, None, :],
        "bd_q": _block_diag_ones(ATTN_WIDTH), "bd_k": _block_diag_ones(KV_WIDTH),
        "sinks": sinks,
        "ln_g": ln_g.reshape(DEPTH, 1, GMLP_WIDTH), "ln_b": ln_b.reshape(DEPTH, 1, GMLP_WIDTH),
        "ws_prompt": ws.astype(BF16), "bs_prompt": jnp.swapaxes(b_s, 1, 2),
        "ws_sample": ws_sample.astype(BF16),
        "bs_sample": jnp.swapaxes(jnp.tile(b_s[:, :, :dec_seq], (1, 1, seqs_per_chunk)), 1, 2),
        "w_up": w_ffn_in, "w_down": w_ffn_out,
        "conv_w": conv_w, "conv_b": conv_b[:, None, :],
    }


def kernel(x_prompt, x_sample, cache_k, cache_v, cache_conv, c_prompt, c_sample, w_ada, b_ada,
           g_attn, w_in, g_q, g_k, sinks, ln_g, ln_b, w_s, b_s, w_out, g_ffn, w_ffn_in, conv_w,
           conv_b, w_ffn_out):
    nbp, seq, _ = x_prompt.shape
    nbs, dec_seq, _ = x_sample.shape
    assert seq % PROMPT_TILE == 0 and PROMPT_TILE % CHUNK == 0 and nbs % SAMPLE_SEQS == 0
    assert seq % FFN_TILE == 0 and FFN_TILE % SUBLANES == 0 and D_FF % FFN_CHUNK == 0
    assert dec_seq == SUBLANES and CHUNK % dec_seq == 0 and cache_k.shape[2] == WINDOW
    assert nbs % SUBLANES == 0 and nbp <= SUBLANES

    n_c = nbs + nbp
    c_all = jnp.concatenate([c_sample, c_prompt, jnp.zeros((-n_c % SUBLANES, D_MODEL), F32)])
    mod3 = _ada_call(c_all, w_ada, b_ada)

    pp = _prepare_params(dec_seq, g_attn, w_in, g_q, g_k, sinks, ln_g, ln_b, w_s, b_s, w_out,
                         g_ffn, w_ffn_in, conv_w, conv_b, w_ffn_out)
    rope_p = _rope_tables(jnp.arange(seq, dtype=jnp.int32))
    rope_s = jnp.tile(_rope_tables(PAST_LEN + jnp.arange(dec_seq, dtype=jnp.int32)),
                      (1, SAMPLE_SEQS, 1))
    ck = jnp.transpose(cache_k, (0, 1, 3, 4, 2))
    cv = jnp.transpose(cache_v, (0, 1, 3, 4, 2))

    xp = x_prompt
    xs = x_sample.reshape(nbs * dec_seq, D_MODEL)
    mix_p = ffn_p = mix_s = ffn_s = None
    for l in range(DEPTH):
        xp, w_in_b, w_out_b, w_up_b, w_dn_b, *mix_p = _mix_prompt_call(
            l, xp, mod3, nbs, pp, rope_p, mix_p)
        xp, *ffn_p = _ffn_prompt_call(l, xp, mod3, nbs, pp, w_up_b, w_dn_b, ffn_p)
        xs, *mix_s = _mix_sample_call(l, xs, mod3, ck, cv, pp, w_in_b, w_out_b, rope_s, dec_seq,
                                      mix_s)
        xs, *ffn_s = _ffn_sample_call(l, xs, mod3, cache_conv, pp, w_up_b, w_dn_b, dec_seq, ffn_s)

    kv_p = (DEPTH, nbp, WINDOW, N_KV_HEADS, HEAD_DIM)
    kv_s = (DEPTH, nbs, dec_seq, N_KV_HEADS, HEAD_DIM)
    return (xp, xs.reshape(nbs, dec_seq, D_MODEL),
            mix_p[0].reshape(kv_p), mix_p[1].reshape(kv_p), ffn_p[0],
            mix_s[0].reshape(kv_s), mix_s[1].reshape(kv_s),
            mix_s[2].reshape(DEPTH, nbs, dec_seq, GMLP_WIDTH),
            ffn_s[0])
```
